```python
import math
import jax, jax.numpy as jnp
from jax import lax
import numpy as np


D_MODEL = 1024
BATCH = 16
SEQ = 4096
DEPTH = 1

POOL_WIDTH = D_MODEL // 2
POOL_WINDOWS = (2, 4, 8, 16)
N_POOL_GROUPS = len(POOL_WINDOWS)
POOL_GROUP = POOL_WIDTH // N_POOL_GROUPS
HEAD_DIM = 64
N_HEADS = (D_MODEL // 2) // HEAD_DIM
N_KV = 2
HEADS_PER_KV = N_HEADS // N_KV
Q_WIDTH = N_HEADS * HEAD_DIM
KV_WIDTH = N_KV * HEAD_DIM
CMP_STRIDE = 16
CMP_BLOCK = 2 * CMP_STRIDE
CMP_HIDDEN = 4 * HEAD_DIM
SLC_BLOCK = 64
N_SELECT = 16
WINDOW = 512
Q_BLOCK = SLC_BLOCK
ALIBI_MAX = 8.0
MIX_SIZES = (POOL_WIDTH, Q_WIDTH, KV_WIDTH, KV_WIDTH, KV_WIDTH, KV_WIDTH, KV_WIDTH, KV_WIDTH, 3 * N_HEADS, 2 * D_MODEL)
MIX_IN = sum(MIX_SIZES)
V_SLOTS = (3, 5, 7)
D_FF = 2816
ALPHA = (2.0 * DEPTH) ** 0.25
BETA = (8.0 * DEPTH) ** -0.25
LN_EPS = 1e-5
NEG = -1e30
FORCE = 1e9

kernel_name = 'hybrid_pool_nsa_macaron_block'


def layer_norm(x, g, b):
    xf = x.astype(jnp.float32)
    mu = jnp.mean(xf, axis=-1, keepdims=True)
    var = jnp.mean(jnp.square(xf - mu), axis=-1, keepdims=True)
    return ((xf - mu) * lax.rsqrt(var + LN_EPS) * g + b).astype(x.dtype)


def swiglu(h, w_in, w_out):
    gate, up = jnp.split(h @ w_in, 2, axis=-1)
    return (jax.nn.silu(gate) * up) @ w_out


def masked_softmax(s, mask):
    p = jax.nn.softmax(jnp.where(mask, s, NEG), axis=-1)
    return jnp.where(mask, p, 0.0)


def alibi_slopes(n_heads):
    return (2.0 ** (-ALIBI_MAX * np.arange(1, n_heads + 1) / n_heads)).astype(np.float32)


def cmp_to_slc_overlap(n_cmp, n_slc):
    start = np.arange(n_cmp)[:, None] * CMP_STRIDE
    blk = np.arange(n_slc)[None, :] * SLC_BLOCK
    return ((start < blk + SLC_BLOCK) & (start + CMP_BLOCK > blk)).astype(np.float32)


def pool_mixer(u, pool_w, pool_scale):
    b_, s_, p_ = u.shape
    uf = u.astype(jnp.float32)
    cs = jnp.concatenate([jnp.zeros((b_, 1, p_), jnp.float32), jnp.cumsum(uf, axis=1)], axis=1)
    t = jnp.arange(s_)
    means = []
    for gi, w in enumerate(POOL_WINDOWS):
        csg = cs[:, :, gi * POOL_GROUP:(gi + 1) * POOL_GROUP]
        lo = jnp.maximum(t + 1 - w, 0)
        cnt = (t + 1 - lo).astype(jnp.float32)
        means.append((csg[:, 1:] - csg[:, lo]) / cnt[None, :, None])
    pooled = jnp.stack(means, axis=2)
    delta = (pooled - uf.reshape(b_, s_, N_POOL_GROUPS, POOL_GROUP)).astype(u.dtype)
    mixed = jnp.einsum('bsgc,gcd->bsgd', delta, pool_w)
    return mixed.reshape(b_, s_, p_) * pool_scale


def compress(k, pos, w1, w2):
    b_, g_, s_, d_ = k.shape
    kc = k.reshape(b_, g_, s_ // CMP_STRIDE, CMP_STRIDE, d_)
    blocks = jnp.concatenate([kc[:, :, :-1], kc[:, :, 1:]], axis=3) + pos
    blocks = blocks.reshape(b_, g_, -1, CMP_BLOCK * d_)
    return jax.nn.gelu(blocks @ w1) @ w2


def nsa_attention(q, k_cmp, v_cmp, k_slc, v_slc, k_win, v_win, g_nsa,
                  cmp_pos_k, cmp_k_w1, cmp_k_w2, cmp_pos_v, cmp_v_w1, cmp_v_w2):
    b_, s_, _ = q.shape
    dt = q.dtype
    f32 = jnp.float32
    q = (q * HEAD_DIM ** -0.5).reshape(b_, s_, N_KV, HEADS_PER_KV, HEAD_DIM).transpose(0, 2, 3, 1, 4)
    to_kv = lambda a: a.reshape(b_, s_, N_KV, HEAD_DIM).transpose(0, 2, 1, 3)
    k_cmp, v_cmp, k_slc, v_slc, k_win, v_win = [to_kv(a) for a in (k_cmp, v_cmp, k_slc, v_slc, k_win, v_win)]
    gates = jax.nn.sigmoid(g_nsa).reshape(b_, s_, N_KV, HEADS_PER_KV, 3).transpose(0, 2, 3, 1, 4)
    kc = compress(k_cmp, cmp_pos_k, cmp_k_w1, cmp_k_w2)
    vc = compress(v_cmp, cmp_pos_v, cmp_v_w1, cmp_v_w2)
    n_cmp = kc.shape[2]
    n_slc = s_ // SLC_BLOCK
    n_sel = min(N_SELECT, n_slc)
    ks_blk = k_slc.reshape(b_, N_KV, n_slc, SLC_BLOCK, HEAD_DIM)
    vs_blk = v_slc.reshape(b_, N_KV, n_slc, SLC_BLOCK, HEAD_DIM)
    pad = ((0, 0), (0, 0), (WINDOW, 0), (0, 0))
    kw_pad = jnp.pad(k_win, pad)
    vw_pad = jnp.pad(v_win, pad)
    slopes = jnp.asarray(alibi_slopes(N_HEADS)).reshape(N_KV, HEADS_PER_KV)[None, :, :, None, None]
    cmp_end = jnp.arange(n_cmp) * CMP_STRIDE + CMP_BLOCK - 1
    overlap = jnp.asarray(cmp_to_slc_overlap(n_cmp, n_slc))
    bi = jnp.arange(b_)[:, None, None, None]
    gi = jnp.arange(N_KV)[None, :, None, None]
    blk_ids = jnp.arange(n_slc)
    in_blk = jnp.arange(SLC_BLOCK)
    win_off = jnp.arange(WINDOW + Q_BLOCK) - WINDOW
    m_sel = n_sel * SLC_BLOCK

    def query_block(qb):
        q0 = qb * Q_BLOCK
        t = q0 + jnp.arange(Q_BLOCK)
        qq = lax.dynamic_slice_in_dim(q, q0, Q_BLOCK, axis=3)
        dist = (t[:, None] - cmp_end[None, :]).astype(f32)
        s = jnp.einsum('bghqd,bgnd->bghqn', qq, kc, preferred_element_type=f32)
        p_cmp = masked_softmax(s - slopes * dist, dist >= 0)
        o_cmp = jnp.einsum('bghqn,bgnd->bghqd', p_cmp.astype(dt), vc)
        imp = jnp.einsum('bghqn,nj->bgqj', p_cmp, overlap)
        cur = t // SLC_BLOCK
        forced = (blk_ids[None, :] == 0) | (blk_ids[None, :] == cur[:, None]) | (blk_ids[None, :] == cur[:, None] - 1)
        future = blk_ids[None, :] > cur[:, None]
        score = jnp.where(future, NEG, jnp.where(forced, FORCE, imp))
        _, idx = lax.top_k(score, n_sel)
        k_sel = ks_blk[bi, gi, idx].reshape(b_, N_KV, Q_BLOCK, m_sel, HEAD_DIM)
        v_sel = vs_blk[bi, gi, idx].reshape(b_, N_KV, Q_BLOCK, m_sel, HEAD_DIM)
        pos = (idx[..., None] * SLC_BLOCK + in_blk).reshape(b_, N_KV, Q_BLOCK, m_sel)
        dist = (t[:, None] - pos).astype(f32)[:, :, None]
        s = jnp.einsum('bghqd,bgqmd->bghqm', qq, k_sel, preferred_element_type=f32)
        p = masked_softmax(s - slopes * dist, dist >= 0)
        o_slc = jnp.einsum('bghqm,bgqmd->bghqd', p.astype(dt), v_sel)
        kw = lax.dynamic_slice_in_dim(kw_pad, q0, WINDOW + Q_BLOCK, axis=2)
        vw = lax.dynamic_slice_in_dim(vw_pad, q0, WINDOW + Q_BLOCK, axis=2)
        sp = q0 + win_off
        dist = t[:, None] - sp[None, :]
        valid = (dist >= 0) & (dist < WINDOW) & (sp[None, :] >= 0)
        s = jnp.einsum('bghqd,bgkd->bghqk', qq, kw, preferred_element_type=f32)
        p = masked_softmax(s - slopes * dist.astype(f32), valid)
        o_win = jnp.einsum('bghqk,bgkd->bghqd', p.astype(dt), vw)
        g = lax.dynamic_slice_in_dim(gates, q0, Q_BLOCK, axis=3)
        return g[..., 0:1] * o_cmp + g[..., 1:2] * o_slc + g[..., 2:3] * o_win

    out = lax.map(query_block, jnp.arange(s_ // Q_BLOCK))
    return out.transpose(1, 0, 4, 2, 3, 5).reshape(b_, s_, Q_WIDTH)


def token_mixer(h, w_mix_in, pool_w, pool_scale, cmp_pos_k, cmp_k_w1, cmp_k_w2, cmp_pos_v, cmp_v_w1, cmp_v_w2,
                w_branch_a, w_branch_b, w_mix_out):
    z = h @ w_mix_in
    splits = np.cumsum(MIX_SIZES)[:-1].tolist()
    u_pool, q, k_cmp, v_cmp, k_slc, v_slc, k_win, v_win, g_nsa, g_br = jnp.split(z, splits, axis=-1)
    y_a = pool_mixer(u_pool, pool_w, pool_scale) @ w_branch_a
    y_b = nsa_attention(q, k_cmp, v_cmp, k_slc, v_slc, k_win, v_win, g_nsa,
                        cmp_pos_k, cmp_k_w1, cmp_k_w2, cmp_pos_v, cmp_v_w1, cmp_v_w2) @ w_branch_b
    g_a, g_b = jnp.split(jax.nn.sigmoid(g_br), 2, axis=-1)
    return (g_a * y_a + g_b * y_b) @ w_mix_out


def setup_inputs(seed: int = 0) -> dict:
    key = jax.random.key(seed)
    keys = jax.random.split(key, 40)
    cnt = [0]

    def nrm(shape, scale):
        k = keys[cnt[0]]
        cnt[0] += 1
        return jax.random.normal(k, shape, jnp.float32) * scale

    L, D = DEPTH, D_MODEL
    col_scale = np.concatenate([np.full((n,), BETA if i in V_SLOTS else 1.0) for i, n in enumerate(MIX_SIZES)])
    col_scale = jnp.asarray((col_scale * D ** -0.5).astype(np.float32))
    return {
        'x': nrm((BATCH, SEQ, D), 1.0),
        'c': nrm((BATCH, D), 1.0),
        'ln_in_g': 1.0 + nrm((D,), 0.02),
        'ln_in_b': nrm((D,), 0.02),
        'w_ada': nrm((L, D, 9 * D), D ** -0.5),
        'b_ada': nrm((L, 9 * D), 0.02),
        'ffn1_w_in': nrm((L, D, 2 * D_FF), D ** -0.5),
        'ffn1_w_out': nrm((L, D_FF, D), BETA * D_FF ** -0.5),
        'ln1_g': 1.0 + nrm((L, D), 0.02),
        'ln1_b': nrm((L, D), 0.02),
        'w_mix_in': nrm((L, D, MIX_IN), 1.0) * col_scale,
        'pool_w': nrm((L, N_POOL_GROUPS, POOL_GROUP, POOL_GROUP), POOL_GROUP ** -0.5),
        'pool_scale': 1.0 + nrm((L, POOL_WIDTH), 0.1),
        'cmp_pos_k': nrm((L, CMP_BLOCK, HEAD_DIM), 0.02),
        'cmp_k_w1': nrm((L, CMP_BLOCK * HEAD_DIM, CMP_HIDDEN), (CMP_BLOCK * HEAD_DIM) ** -0.5),
        'cmp_k_w2': nrm((L, CMP_HIDDEN, HEAD_DIM), CMP_HIDDEN ** -0.5),
        'cmp_pos_v': nrm((L, CMP_BLOCK, HEAD_DIM), 0.02),
        'cmp_v_w1': nrm((L, CMP_BLOCK * HEAD_DIM, CMP_HIDDEN), (CMP_BLOCK * HEAD_DIM) ** -0.5),
        'cmp_v_w2': nrm((L, CMP_HIDDEN, HEAD_DIM), CMP_HIDDEN ** -0.5),
        'w_branch_a': nrm((L, POOL_WIDTH, D), BETA * POOL_WIDTH ** -0.5),
        'w_branch_b': nrm((L, Q_WIDTH, D), BETA * Q_WIDTH ** -0.5),
        'w_mix_out': nrm((L, D, D), BETA * D ** -0.5),
        'ln2_g': 1.0 + nrm((L, D), 0.02),
        'ln2_b': nrm((L, D), 0.02),
        'ffn2_w_in': nrm((L, D, 2 * D_FF), D ** -0.5),
        'ffn2_w_out': nrm((L, D_FF, D), BETA * D_FF ** -0.5),
        'ln3_g': 1.0 + nrm((L, D), 0.02),
        'ln3_b': nrm((L, D), 0.02),
    }


def reference(x, c, ln_in_g, ln_in_b, w_ada, b_ada, ffn1_w_in, ffn1_w_out, ln1_g, ln1_b,
              w_mix_in, pool_w, pool_scale, cmp_pos_k, cmp_k_w1, cmp_k_w2, cmp_pos_v, cmp_v_w1, cmp_v_w2,
              w_branch_a, w_branch_b, w_mix_out, ln2_g, ln2_b, ffn2_w_in, ffn2_w_out, ln3_g, ln3_b):
    x = layer_norm(x, ln_in_g, ln_in_b)
    c_act = jax.nn.silu(c)
    for l in range(DEPTH):
        ada = (c_act @ w_ada[l] + b_ada[l]).reshape(c.shape[0], 3, 3, D_MODEL)
        mod = lambda i, j: ada[:, i, j][:, None, :]
        h = x * (1.0 + mod(0, 1)) + mod(0, 0)
        x = layer_norm(ALPHA * x + 0.5 * mod(0, 2) * swiglu(h, ffn1_w_in[l], ffn1_w_out[l]), ln1_g[l], ln1_b[l])
        h = x * (1.0 + mod(1, 1)) + mod(1, 0)
        y = token_mixer(h, w_mix_in[l], pool_w[l], pool_scale[l], cmp_pos_k[l], cmp_k_w1[l], cmp_k_w2[l],
                        cmp_pos_v[l], cmp_v_w1[l], cmp_v_w2[l], w_branch_a[l], w_branch_b[l], w_mix_out[l])
        x = layer_norm(ALPHA * x + mod(1, 2) * y, ln2_g[l], ln2_b[l])
        h = x * (1.0 + mod(2, 1)) + mod(2, 0)
        x = layer_norm(ALPHA * x + 0.5 * mod(2, 2) * swiglu(h, ffn2_w_in[l], ffn2_w_out[l]), ln3_g[l], ln3_b[l])
    return x
```

```python
import functools

import numpy as np
import jax
import jax.numpy as jnp
from jax import lax
from jax.experimental import pallas as pl
from jax.experimental.pallas import tpu as pltpu

F32 = jnp.float32
BF16 = jnp.bfloat16

D_MODEL = 1024
POOL_WIDTH = D_MODEL // 2
POOL_WINDOWS = (2, 4, 8, 16)
POOL_GROUP = POOL_WIDTH // len(POOL_WINDOWS)
POOL_HALO = 16
HEAD_DIM = 64
N_HEADS = (D_MODEL // 2) // HEAD_DIM
N_KV = 2
HEADS_PER_KV = N_HEADS // N_KV
Q_WIDTH = N_HEADS * HEAD_DIM
KV_WIDTH = N_KV * HEAD_DIM
CMP_STRIDE = 16
CMP_BLOCK = 2 * CMP_STRIDE
CMP_HIDDEN = 4 * HEAD_DIM
SLC_BLOCK = 64
N_SELECT = 16
WINDOW = 512
Q_BLOCK = SLC_BLOCK
ALIBI_MAX = 8.0
D_FF = 2816
DEPTH = 1
ALPHA = (2.0 * DEPTH) ** 0.25
LN_EPS = 1e-5
NEG = -1e30
FORCE = 1e9

LANES = 128
KV_PACK = 2 * HEAD_DIM
FF_CHUNK = 256
ROW_TILE = 512
SLC_CHUNK = 512
WIN_KEYS = WINDOW + 2 * Q_BLOCK
VMEM_LIMIT = 52 * 1024 * 1024

_C_U = 0
_C_Q = _C_U + POOL_WIDTH
_C_KC = _C_Q + Q_WIDTH
_C_VC = _C_KC + KV_WIDTH
_C_KVS = _C_VC + KV_WIDTH
_C_KVW = _C_KVS + N_KV * KV_PACK
_C_GN = _C_KVW + N_KV * KV_PACK
_C_GBR = _C_GN + N_KV * LANES
_C_END = _C_GBR + 2 * D_MODEL


def _dot(a, b):
    return jnp.dot(a, b, preferred_element_type=F32)


def _dot_nt(a, b):
    return lax.dot_general(a, b, (((1,), (1,)), ((), ())), preferred_element_type=F32)


def _layer_norm(x, g, b):
    mu = jnp.mean(x, axis=-1, keepdims=True)
    xc = x - mu
    var = jnp.mean(xc * xc, axis=-1, keepdims=True)
    return xc * lax.rsqrt(var + LN_EPS) * g + b


def _masked_softmax(s, mask):
    sm = jnp.where(mask, s, NEG)
    m = jnp.max(sm, axis=-1, keepdims=True)
    e = jnp.exp(sm - m)
    p = e * (1.0 / jnp.sum(e, axis=-1, keepdims=True))
    return jnp.where(mask, p, 0.0)


def _const_spec(shape):
    nd = len(shape)
    return pl.BlockSpec(shape, lambda *_: (0,) * nd, pipeline_mode=pl.Buffered(1))


def _params(n_grid):
    return pltpu.CompilerParams(dimension_semantics=("parallel",) * n_grid, vmem_limit_bytes=VMEM_LIMIT)


def _ada_kernel(c_ref, w_ref, b_ref, o_ref):
    c = c_ref[...]
    c_act = (c * jax.nn.sigmoid(c)).astype(BF16)
    o_ref[...] = _dot(c_act, w_ref[...].astype(BF16)) + b_ref[...]


def _ada(c, w, b):
    bsz, d = c.shape
    n = w.shape[1]
    tn = D_MODEL
    return pl.pallas_call(
        _ada_kernel,
        grid=(n // tn,),
        in_specs=[pl.BlockSpec((bsz, d), lambda j: (0, 0)),
                  pl.BlockSpec((d, tn), lambda j: (0, j)),
                  pl.BlockSpec((1, tn), lambda j: (0, j))],
        out_specs=pl.BlockSpec((bsz, tn), lambda j: (0, j)),
        out_shape=jax.ShapeDtypeStruct((bsz, n), F32),
        compiler_params=_params(1),
        name="ada_proj",
    )(c, w, b.reshape(1, n))


def _ffn_kernel(x_ref, shift_ref, scale_ref, gate_ref, g0_ref, b0_ref, g1_ref, b1_ref,
                w_in_ref, w_out_ref, o_ref, *, pre_ln):
    x = x_ref[0]
    if pre_ln:
        x = _layer_norm(x, g0_ref[...], b0_ref[...])
    h = (x * (1.0 + scale_ref[0]) + shift_ref[0]).astype(BF16)
    acc = jnp.zeros(x.shape, F32)
    for j in range(D_FF // FF_CHUNK):
        c0 = j * FF_CHUNK
        gt = _dot(h, w_in_ref[:, c0:c0 + FF_CHUNK])
        up = _dot(h, w_in_ref[:, D_FF + c0:D_FF + c0 + FF_CHUNK])
        act = (gt * jax.nn.sigmoid(gt) * up).astype(BF16)
        acc = acc + _dot(act, w_out_ref[c0:c0 + FF_CHUNK, :])
    y = ALPHA * x + 0.5 * gate_ref[0] * acc
    o_ref[0] = _layer_norm(y, g1_ref[...], b1_ref[...])


def _ffn(x, shift, scale, gate, g0, b0, g1, b1, w_in, w_out, *, pre_ln):
    bsz, s, d = x.shape
    tm = min(ROW_TILE, s)
    row = pl.BlockSpec((1, tm, d), lambda b, i: (b, i, 0))
    mod = pl.BlockSpec((1, 1, d), lambda b, i: (b, 0, 0))
    vec = _const_spec((1, d))
    return pl.pallas_call(
        functools.partial(_ffn_kernel, pre_ln=pre_ln),
        grid=(bsz, s // tm),
        in_specs=[row, mod, mod, mod, vec, vec, vec, vec,
                  _const_spec(w_in.shape), _const_spec(w_out.shape)],
        out_specs=row,
        out_shape=jax.ShapeDtypeStruct((bsz, s, d), F32),
        compiler_params=_params(2),
        name="ffn_block",
    )(x, shift, scale, gate, g0.reshape(1, d), b0.reshape(1, d), g1.reshape(1, d), b1.reshape(1, d),
      w_in, w_out)


def _mixin_kernel(x_ref, shift_ref, scale_ref, w_ref,
                  u_ref, q_ref, kc_ref, vc_ref, kvs_ref, kvw_ref, gn_ref, gbr_ref):
    h = (x_ref[0] * (1.0 + scale_ref[0]) + shift_ref[0]).astype(BF16)

    def proj(c0, c1):
        return _dot(h, w_ref[:, c0:c1])

    u_ref[0] = proj(_C_U, _C_Q)
    q_ref[0] = (proj(_C_Q, _C_KC) * HEAD_DIM ** -0.5).astype(BF16)
    kc_ref[0] = proj(_C_KC, _C_VC)
    vc_ref[0] = proj(_C_VC, _C_KVS)
    for ref, c0 in ((kvs_ref, _C_KVS), (kvw_ref, _C_KVW)):
        z = proj(c0, c0 + N_KV * KV_PACK).astype(BF16)
        for g in range(N_KV):
            ref[0, g] = z[:, g * KV_PACK:(g + 1) * KV_PACK]
    z = jax.nn.sigmoid(proj(_C_GN, _C_GBR))
    for g in range(N_KV):
        gn_ref[0, g] = z[:, g * LANES:(g + 1) * LANES]
    gbr_ref[0] = jax.nn.sigmoid(proj(_C_GBR, _C_END)).astype(BF16)


def _mixin(x, shift, scale, w):
    bsz, s, d = x.shape
    tm = min(ROW_TILE, s)

    def row(n, dt):
        return pl.BlockSpec((1, tm, n), lambda b, i: (b, i, 0)), jax.ShapeDtypeStruct((bsz, s, n), dt)

    def grouped(n, dt):
        return (pl.BlockSpec((1, N_KV, tm, n), lambda b, i: (b, 0, i, 0)),
                jax.ShapeDtypeStruct((bsz, N_KV, s, n), dt))

    outs = [row(POOL_WIDTH, F32), row(Q_WIDTH, BF16), row(KV_WIDTH, F32), row(KV_WIDTH, F32),
            grouped(KV_PACK, BF16), grouped(KV_PACK, BF16), grouped(LANES, F32), row(2 * D_MODEL, BF16)]
    mod = pl.BlockSpec((1, 1, d), lambda b, i: (b, 0, 0))
    return pl.pallas_call(
        _mixin_kernel,
        grid=(bsz, s // tm),
        in_specs=[pl.BlockSpec((1, tm, d), lambda b, i: (b, i, 0)), mod, mod, _const_spec(w.shape)],
        out_specs=[o[0] for o in outs],
        out_shape=[o[1] for o in outs],
        compiler_params=_params(2),
        name="mixer_in_proj",
    )(x, shift, scale, w)


def _compress_kernel(k_ref, v_ref, posk_ref, posv_ref, w1k_ref, w1v_ref, w2k_ref, w2v_ref, o_ref, nxt_ref):
    r = k_ref.shape[1]

    def hidden(x_ref, pos_ref, w1_ref):
        x = x_ref[0]
        first = _dot((x + pos_ref[0:1, :]).astype(BF16), w1_ref[0])
        nxt_ref[0:r, :] = _dot((x + pos_ref[1:2, :]).astype(BF16), w1_ref[1])
        nxt_ref[r:r + 8, :] = jnp.zeros((8, nxt_ref.shape[1]), F32)
        return jax.nn.gelu(first + nxt_ref[1:r + 1, :]).astype(BF16)

    hk = hidden(k_ref, posk_ref, w1k_ref)
    hv = hidden(v_ref, posv_ref, w1v_ref)
    o_ref[0] = (_dot(hk, w2k_ref[...]) + _dot(hv, w2v_ref[...])).astype(BF16)


def _compress(kc, vc, posk, posv, w1k, w1v, w2k, w2v):
    bsz, r, n = kc.shape
    hid = w1k.shape[2]
    row = pl.BlockSpec((1, r, n), lambda b: (b, 0, 0))
    return pl.pallas_call(
        _compress_kernel,
        grid=(bsz,),
        in_specs=[row, row, _const_spec(posk.shape), _const_spec(posv.shape),
                  _const_spec(w1k.shape), _const_spec(w1v.shape),
                  _const_spec(w2k.shape), _const_spec(w2v.shape)],
        out_specs=pl.BlockSpec((1, r, N_KV * KV_PACK), lambda b: (b, 0, 0)),
        out_shape=jax.ShapeDtypeStruct((bsz, r, N_KV * KV_PACK), BF16),
        scratch_shapes=[pltpu.VMEM((r + 8, hid), F32)],
        compiler_params=_params(1),
        name="compress_mlp",
    )(kc, vc, posk, posv, w1k, w1v, w2k, w2v)


def _attn_kernel(q_ref, gate_ref, kvc_ref, kvs_ref, kvw_ref, slope_ref, ovl_ref, expand_ref, o_ref,
                 *, n_cmp, n_sel):
    qb = pl.program_id(2)
    q0 = qb * Q_BLOCK
    rows = HEADS_PER_KV * Q_BLOCK
    n_slc = ovl_ref.shape[0]

    lane = lax.broadcasted_iota(jnp.int32, (Q_BLOCK, LANES), 1)
    qf = q_ref[0].astype(F32)
    parts = []
    for h in range(HEADS_PER_KV):
        slab = qf[:, (h // 2) * LANES:(h // 2 + 1) * LANES]
        if h % 2:
            slab = pltpu.roll(slab, HEAD_DIM, 1)
        parts.append(jnp.where(lane < HEAD_DIM, slab, 0.0))
    q_pad = jnp.concatenate(parts, axis=0).astype(BF16)

    slope = slope_ref[0]
    t = q0 + (lax.broadcasted_iota(jnp.int32, (rows, 1), 0) & (Q_BLOCK - 1))

    kvc = kvc_ref[0]
    r = kvc.shape[0]
    n_idx = lax.broadcasted_iota(jnp.int32, (rows, r), 1)
    dist = t - (n_idx * CMP_STRIDE + (CMP_BLOCK - 1))
    valid = (dist >= 0) & (n_idx < n_cmp)
    p_cmp = _masked_softmax(_dot_nt(q_pad, kvc) - slope * dist.astype(F32), valid)
    o_cmp = _dot(p_cmp.astype(BF16), kvc)

    p_sum = p_cmp[0:Q_BLOCK]
    for h in range(1, HEADS_PER_KV):
        p_sum = p_sum + p_cmp[h * Q_BLOCK:(h + 1) * Q_BLOCK]
    p_hi = p_sum.astype(BF16)
    p_lo = (p_sum - p_hi.astype(F32)).astype(BF16)
    ovl = ovl_ref[...]
    imp_t = _dot_nt(ovl, p_hi) + _dot_nt(ovl, p_lo)

    blk = lax.broadcasted_iota(jnp.int32, (n_slc, Q_BLOCK), 0)
    forced = (blk == 0) | (blk == qb) | (blk == qb - 1)
    score = jnp.where(blk > qb, NEG, jnp.where(forced, FORCE, imp_t))
    rank = jnp.zeros((n_slc, Q_BLOCK), F32)
    for i in range(n_slc):
        ri = score[i:i + 1, :]
        beats = (ri > score) | ((ri == score) & (blk > i))
        rank = rank + jnp.where(beats, 1.0, 0.0)
    sel_t = jnp.where(rank < n_sel, 1.0, 0.0).astype(BF16)
    eye = jnp.where(lax.broadcasted_iota(jnp.int32, (Q_BLOCK, Q_BLOCK), 0)
                    == lax.broadcasted_iota(jnp.int32, (Q_BLOCK, Q_BLOCK), 1), 1.0, 0.0).astype(BF16)
    sel = _dot_nt(eye, sel_t).astype(BF16)

    def slc_step(c, carry):
        m, l, acc = carry
        k0 = pl.multiple_of(c * SLC_CHUNK, SLC_CHUNK)
        kv = kvs_ref[0, 0, pl.ds(k0, SLC_CHUNK), :]
        chosen = _dot(sel, expand_ref[:, pl.ds(k0, SLC_CHUNK)])
        chosen = jnp.concatenate([chosen] * HEADS_PER_KV, axis=0)
        dist = t - (k0 + lax.broadcasted_iota(jnp.int32, (rows, SLC_CHUNK), 1))
        valid = (chosen > 0.5) & (dist >= 0)
        sm = jnp.where(valid, _dot_nt(q_pad, kv) - slope * dist.astype(F32), NEG)
        m_new = jnp.maximum(m, jnp.max(sm, axis=-1, keepdims=True))
        e = jnp.where(valid, jnp.exp(sm - m_new), 0.0)
        alpha = jnp.exp(m - m_new)
        l = alpha * l + jnp.sum(e, axis=-1, keepdims=True)
        acc = alpha * acc + _dot(e.astype(BF16), kv)
        return m_new, l, acc

    init = (jnp.full((rows, 1), NEG, F32), jnp.zeros((rows, 1), F32), jnp.zeros((rows, KV_PACK), F32))
    n_chunks = (q0 + Q_BLOCK + SLC_CHUNK - 1) // SLC_CHUNK
    _, l, acc = lax.fori_loop(0, n_chunks, slc_step, init)
    o_slc = acc * jnp.where(l > 0.0, 1.0 / l, 0.0)

    w0 = pl.multiple_of(jnp.maximum(q0 + Q_BLOCK - WIN_KEYS, 0), Q_BLOCK)
    kvw = kvw_ref[0, 0, pl.ds(w0, WIN_KEYS), :]
    dist = t - (w0 + lax.broadcasted_iota(jnp.int32, (rows, WIN_KEYS), 1))
    valid = (dist >= 0) & (dist < WINDOW)
    p_win = _masked_softmax(_dot_nt(q_pad, kvw) - slope * dist.astype(F32), valid)
    o_win = _dot(p_win.astype(BF16), kvw)

    gt = gate_ref[0, 0]
    heads = []
    for h in range(HEADS_PER_KV):
        rs = slice(h * Q_BLOCK, (h + 1) * Q_BLOCK)
        heads.append(gt[:, 3 * h:3 * h + 1] * o_cmp[rs] + gt[:, 3 * h + 1:3 * h + 2] * o_slc[rs]
                     + gt[:, 3 * h + 2:3 * h + 3] * o_win[rs])
    pairs = [jnp.where(lane < HEAD_DIM, pltpu.roll(heads[2 * j], HEAD_DIM, 1), heads[2 * j + 1])
             for j in range(HEADS_PER_KV // 2)]
    o_ref[0] = jnp.concatenate(pairs, axis=1).astype(BF16)


def _attention(q, gates, kvc, kvs, kvw):
    bsz, s, _ = q.shape
    r = kvc.shape[1]
    n_cmp = r - 1
    n_slc = s // SLC_BLOCK
    n_sel = min(N_SELECT, n_slc)
    rows = HEADS_PER_KV * Q_BLOCK
    gw = HEADS_PER_KV * HEAD_DIM

    slopes = 2.0 ** (-ALIBI_MAX * np.arange(1, N_HEADS + 1) / N_HEADS)
    slope_rows = np.repeat(slopes.reshape(N_KV, HEADS_PER_KV), Q_BLOCK, axis=1).reshape(N_KV, rows, 1)
    start = np.arange(r)[None, :] * CMP_STRIDE
    blk = np.arange(n_slc)[:, None] * SLC_BLOCK
    overlap_t = ((start < blk + SLC_BLOCK) & (start + CMP_BLOCK > blk) & (np.arange(r)[None, :] < n_cmp))
    expand = np.arange(s)[None, :] // SLC_BLOCK == np.arange(n_slc)[:, None]

    return pl.pallas_call(
        functools.partial(_attn_kernel, n_cmp=n_cmp, n_sel=n_sel),
        grid=(bsz, N_KV, s // Q_BLOCK),
        in_specs=[pl.BlockSpec((1, Q_BLOCK, gw), lambda b, g, i: (b, i, g)),
                  pl.BlockSpec((1, 1, Q_BLOCK, LANES), lambda b, g, i: (b, g, i, 0)),
                  pl.BlockSpec((1, r, KV_PACK), lambda b, g, i: (b, 0, g)),
                  pl.BlockSpec((1, 1, s, KV_PACK), lambda b, g, i: (b, g, 0, 0)),
                  pl.BlockSpec((1, 1, s, KV_PACK), lambda b, g, i: (b, g, 0, 0)),
                  pl.BlockSpec((1, rows, 1), lambda b, g, i: (g, 0, 0)),
                  _const_spec((n_slc, r)),
                  _const_spec((n_slc, s))],
        out_specs=pl.BlockSpec((1, Q_BLOCK, gw), lambda b, g, i: (b, i, g)),
        out_shape=jax.ShapeDtypeStruct((bsz, s, Q_WIDTH), BF16),
        compiler_params=_params(3),
        name="sparse_attention",
    )(q, gates, kvc, kvs, kvw, jnp.asarray(slope_rows, F32),
      jnp.asarray(overlap_t, BF16), jnp.asarray(expand, BF16))


def _mixout_kernel(x_ref, gate_ref, uprev_ref, u_ref, o_ref, gbr_ref, pw_ref, ps_ref,
                   wa_ref, wb_ref, wo_ref, g_ref, b_ref, out_ref, ubuf_ref):
    i = pl.program_id(1)
    tm = u_ref.shape[1]
    ubuf_ref[0:POOL_HALO, :] = jnp.where(i == 0, 0.0, uprev_ref[0])
    ubuf_ref[POOL_HALO:POOL_HALO + tm, :] = u_ref[0]
    t = i * tm + lax.broadcasted_iota(jnp.int32, (tm, 1), 0)

    mixed = []
    for gi, w in enumerate(POOL_WINDOWS):
        cs = slice(gi * POOL_GROUP, (gi + 1) * POOL_GROUP)
        cur = ubuf_ref[POOL_HALO:POOL_HALO + tm, cs]
        total = cur
        for k in range(1, w):
            total = total + ubuf_ref[POOL_HALO - k:POOL_HALO - k + tm, cs]
        inv_cnt = 1.0 / jnp.minimum(t + 1, w).astype(F32)
        delta = (total * inv_cnt - cur).astype(BF16)
        mixed.append((_dot(delta, pw_ref[gi]) * ps_ref[:, cs]).astype(BF16))
    y_a = _dot(jnp.concatenate(mixed, axis=1), wa_ref[...])
    y_b = _dot(o_ref[0], wb_ref[...])
    d = y_a.shape[1]
    y = (gbr_ref[0, :, 0:d].astype(F32) * y_a + gbr_ref[0, :, d:2 * d].astype(F32) * y_b).astype(BF16)
    y = _dot(y, wo_ref[...])
    out_ref[0] = _layer_norm(ALPHA * x_ref[0] + gate_ref[0] * y, g_ref[...], b_ref[...])


def _mixout(x, gate, u, o, gbr, pool_w, pool_scale, w_a, w_b, w_o, g, b):
    bsz, s, d = x.shape
    tm = min(ROW_TILE, s)
    halo_blocks = tm // POOL_HALO

    def row(n):
        return pl.BlockSpec((1, tm, n), lambda bi, i: (bi, i, 0))

    return pl.pallas_call(
        _mixout_kernel,
        grid=(bsz, s // tm),
        in_specs=[row(d), pl.BlockSpec((1, 1, d), lambda bi, i: (bi, 0, 0)),
                  pl.BlockSpec((1, POOL_HALO, POOL_WIDTH),
                               lambda bi, i: (bi, jnp.maximum(i * halo_blocks - 1, 0), 0)),
                  row(POOL_WIDTH), row(Q_WIDTH), row(2 * d),
                  _const_spec(pool_w.shape), _const_spec((1, POOL_WIDTH)),
                  _const_spec(w_a.shape), _const_spec(w_b.shape), _const_spec(w_o.shape),
                  _const_spec((1, d)), _const_spec((1, d))],
        out_specs=row(d),
        out_shape=jax.ShapeDtypeStruct((bsz, s, d), F32),
        scratch_shapes=[pltpu.VMEM((POOL_HALO + tm, POOL_WIDTH), F32)],
        compiler_params=_params(2),
        name="pool_merge_out",
    )(x, gate, u, u, o, gbr, pool_w, pool_scale.reshape(1, POOL_WIDTH), w_a, w_b, w_o,
      g.reshape(1, d), b.reshape(1, d))


def _mixer_in_weights(w):
    sizes = (POOL_WIDTH, Q_WIDTH) + (KV_WIDTH,) * 6 + (3 * N_HEADS, 2 * D_MODEL)
    offs = np.concatenate([[0], np.cumsum(sizes)])
    u, q, k_cmp, v_cmp, k_slc, v_slc, k_win, v_win, g_nsa, g_br = [w[:, offs[i]:offs[i + 1]] for i in range(10)]
    cols = [u, q, k_cmp, v_cmp]
    for k, v in ((k_slc, v_slc), (k_win, v_win)):
        for g in range(N_KV):
            cols += [k[:, g * HEAD_DIM:(g + 1) * HEAD_DIM], v[:, g * HEAD_DIM:(g + 1) * HEAD_DIM]]
    per_g = 3 * HEADS_PER_KV
    for g in range(N_KV):
        cols += [g_nsa[:, g * per_g:(g + 1) * per_g], jnp.zeros((w.shape[0], LANES - per_g), w.dtype)]
    cols.append(g_br)
    return jnp.concatenate(cols, axis=1).astype(BF16)


def _compress_weights(pos, w1, w2, value_slot):
    def expand_w1(half):
        wh = half.reshape(CMP_STRIDE, HEAD_DIM, CMP_HIDDEN)
        z = jnp.zeros((CMP_STRIDE, N_KV, HEAD_DIM, N_KV, CMP_HIDDEN), w1.dtype)
        for g in range(N_KV):
            z = z.at[:, g, :, g, :].set(wh)
        return z.reshape(CMP_STRIDE * KV_WIDTH, N_KV * CMP_HIDDEN)

    half_rows = CMP_STRIDE * HEAD_DIM
    w1_big = jnp.stack([expand_w1(w1[:half_rows]), expand_w1(w1[half_rows:])]).astype(BF16)
    w2_big = jnp.zeros((N_KV, CMP_HIDDEN, N_KV, 2, HEAD_DIM), w2.dtype)
    for g in range(N_KV):
        w2_big = w2_big.at[g, :, g, value_slot, :].set(w2)
    w2_big = w2_big.reshape(N_KV * CMP_HIDDEN, N_KV * KV_PACK).astype(BF16)
    pos_rows = jnp.broadcast_to(pos.reshape(2, CMP_STRIDE, 1, HEAD_DIM), (2, CMP_STRIDE, N_KV, HEAD_DIM))
    return pos_rows.reshape(2, CMP_STRIDE * KV_WIDTH), w1_big, w2_big


def kernel(x, c, ln_in_g, ln_in_b, w_ada, b_ada, ffn1_w_in, ffn1_w_out, ln1_g, ln1_b, w_mix_in, pool_w, pool_scale,
           cmp_pos_k, cmp_k_w1, cmp_k_w2, cmp_pos_v, cmp_v_w1, cmp_v_w2, w_branch_a, w_branch_b, w_mix_out,
           ln2_g, ln2_b, ffn2_w_in, ffn2_w_out, ln3_g, ln3_b):
    bsz, s, d = x.shape
    for l in range(DEPTH):
        ada = _ada(c, w_ada[l], b_ada[l]).reshape(bsz, 3, 3, 1, d)
        mod = lambda i, j: ada[:, i, j]

        pre_ln = l == 0
        x = _ffn(x, mod(0, 0), mod(0, 1), mod(0, 2), ln_in_g, ln_in_b, ln1_g[l], ln1_b[l],
                 ffn1_w_in[l].astype(BF16), ffn1_w_out[l].astype(BF16), pre_ln=pre_ln)

        u, q, kc, vc, kvs, kvw, gates, gbr = _mixin(x, mod(1, 0), mod(1, 1), _mixer_in_weights(w_mix_in[l]))
        posk, w1k, w2k = _compress_weights(cmp_pos_k[l], cmp_k_w1[l], cmp_k_w2[l], 0)
        posv, w1v, w2v = _compress_weights(cmp_pos_v[l], cmp_v_w1[l], cmp_v_w2[l], 1)
        rows = s // CMP_STRIDE
        kvc = _compress(kc.reshape(bsz, rows, CMP_STRIDE * KV_WIDTH), vc.reshape(bsz, rows, CMP_STRIDE * KV_WIDTH),
                        posk, posv, w1k, w1v, w2k, w2v)
        o = _attention(q, gates, kvc, kvs, kvw)
        x = _mixout(x, mod(1, 2), u, o, gbr, pool_w[l].astype(BF16), pool_scale[l],
                    w_branch_a[l].astype(BF16), w_branch_b[l].astype(BF16), w_mix_out[l].astype(BF16),
                    ln2_g[l], ln2_b[l])

        x = _ffn(x, mod(2, 0), mod(2, 1), mod(2, 2), ln_in_g, ln_in_b, ln3_g[l], ln3_b[l],
                 ffn2_w_in[l].astype(BF16), ffn2_w_out[l].astype(BF16), pre_ln=False)
    return x
```

```python
import functools

import numpy as np
import jax
import jax.numpy as jnp
from jax import lax
from jax.experimental import pallas as pl
from jax.experimental.pallas import tpu as pltpu

F32 = jnp.float32
BF16 = jnp.bfloat16

D_MODEL = 1024
POOL_WIDTH = D_MODEL // 2
POOL_WINDOWS = (2, 4, 8, 16)
POOL_GROUP = POOL_WIDTH // len(POOL_WINDOWS)
POOL_HALO = 16
HEAD_DIM = 64
N_HEADS = (D_MODEL // 2) // HEAD_DIM
N_KV = 2
HEADS_PER_KV = N_HEADS // N_KV
Q_WIDTH = N_HEADS * HEAD_DIM
KV_WIDTH = N_KV * HEAD_DIM
CMP_STRIDE = 16
CMP_BLOCK = 2 * CMP_STRIDE
CMP_HIDDEN = 4 * HEAD_DIM
SLC_BLOCK = 64
N_SELECT = 16
WINDOW = 512
Q_BLOCK = SLC_BLOCK
ALIBI_MAX = 8.0
D_FF = 2816
DEPTH = 1
ALPHA = (2.0 * DEPTH) ** 0.25
LN_EPS = 1e-5
NEG = -1e30
FORCE = 1e9

LANES = 128
KV_PACK = 2 * HEAD_DIM
KEY_WIDTH = 2 * KV_PACK
FEAT_OFFSET_LANE = SLC_BLOCK
BIG = 1e30
M_INIT = -3e38
FF_CHUNK = 256
ROW_TILE = 512
SLC_CHUNK = 512
WIN_KEYS = WINDOW + 2 * Q_BLOCK
VMEM_LIMIT = 52 * 1024 * 1024

_C_U = 0
_C_Q = _C_U + POOL_WIDTH
_C_KC = _C_Q + Q_WIDTH
_C_VC = _C_KC + KV_WIDTH
_C_KVS = _C_VC + KV_WIDTH
_C_KVW = _C_KVS + N_KV * KV_PACK
_C_GN = _C_KVW + N_KV * KV_PACK
_C_GBR = _C_GN + N_KV * LANES
_C_END = _C_GBR + 2 * D_MODEL


def _dot(a, b):
    return jnp.dot(a, b, preferred_element_type=F32)


def _dot_nt(a, b):
    return lax.dot_general(a, b, (((1,), (1,)), ((), ())), preferred_element_type=F32)


def _layer_norm(x, g, b):
    mu = jnp.mean(x, axis=-1, keepdims=True)
    xc = x - mu
    var = jnp.mean(xc * xc, axis=-1, keepdims=True)
    return xc * lax.rsqrt(var + LN_EPS) * g + b


def _masked_softmax(s, mask):
    sm = jnp.where(mask, s, NEG)
    m = jnp.max(sm, axis=-1, keepdims=True)
    e = jnp.exp(sm - m)
    p = e * (1.0 / jnp.sum(e, axis=-1, keepdims=True))
    return jnp.where(mask, p, 0.0)


def _const_spec(shape):
    nd = len(shape)
    return pl.BlockSpec(shape, lambda *_: (0,) * nd, pipeline_mode=pl.Buffered(1))


def _params(n_grid):
    return pltpu.CompilerParams(dimension_semantics=("parallel",) * n_grid, vmem_limit_bytes=VMEM_LIMIT)


def _ada_kernel(c_ref, w_ref, b_ref, o_ref):
    c = c_ref[...]
    c_act = (c * jax.nn.sigmoid(c)).astype(BF16)
    o_ref[...] = _dot(c_act, w_ref[...].astype(BF16)) + b_ref[...]


def _ada(c, w, b):
    bsz, d = c.shape
    n = w.shape[1]
    tn = D_MODEL
    return pl.pallas_call(
        _ada_kernel,
        grid=(n // tn,),
        in_specs=[pl.BlockSpec((bsz, d), lambda j: (0, 0)),
                  pl.BlockSpec((d, tn), lambda j: (0, j)),
                  pl.BlockSpec((1, tn), lambda j: (0, j))],
        out_specs=pl.BlockSpec((bsz, tn), lambda j: (0, j)),
        out_shape=jax.ShapeDtypeStruct((bsz, n), F32),
        compiler_params=_params(1),
        name="ada_proj",
    )(c, w, b.reshape(1, n))


def _ffn_kernel(x_ref, shift_ref, scale_ref, gate_ref, g0_ref, b0_ref, g1_ref, b1_ref,
                w_in_ref, w_out_ref, o_ref, *, pre_ln):
    x = x_ref[0]
    if pre_ln:
        x = _layer_norm(x, g0_ref[...], b0_ref[...])
    h = (x * (1.0 + scale_ref[0]) + shift_ref[0]).astype(BF16)
    acc = jnp.zeros(x.shape, F32)
    for j in range(D_FF // FF_CHUNK):
        c0 = j * FF_CHUNK
        gt = _dot(h, w_in_ref[:, c0:c0 + FF_CHUNK])
        up = _dot(h, w_in_ref[:, D_FF + c0:D_FF + c0 + FF_CHUNK])
        act = (gt * jax.nn.sigmoid(gt) * up).astype(BF16)
        acc = acc + _dot(act, w_out_ref[c0:c0 + FF_CHUNK, :])
    y = ALPHA * x + 0.5 * gate_ref[0] * acc
    o_ref[0] = _layer_norm(y, g1_ref[...], b1_ref[...])


def _ffn(x, shift, scale, gate, g0, b0, g1, b1, w_in, w_out, *, pre_ln):
    bsz, s, d = x.shape
    tm = min(ROW_TILE, s)
    row = pl.BlockSpec((1, tm, d), lambda b, i: (b, i, 0))
    mod = pl.BlockSpec((1, 1, d), lambda b, i: (b, 0, 0))
    vec = _const_spec((1, d))
    return pl.pallas_call(
        functools.partial(_ffn_kernel, pre_ln=pre_ln),
        grid=(bsz, s // tm),
        in_specs=[row, mod, mod, mod, vec, vec, vec, vec,
                  _const_spec(w_in.shape), _const_spec(w_out.shape)],
        out_specs=row,
        out_shape=jax.ShapeDtypeStruct((bsz, s, d), F32),
        compiler_params=_params(2),
        name="ffn_block",
    )(x, shift, scale, gate, g0.reshape(1, d), b0.reshape(1, d), g1.reshape(1, d), b1.reshape(1, d),
      w_in, w_out)


def _mixin_kernel(x_ref, shift_ref, scale_ref, w_ref,
                  u_ref, q_ref, kc_ref, vc_ref, kvs_ref, kvw_ref, gn_ref, gbr_ref):
    h = (x_ref[0] * (1.0 + scale_ref[0]) + shift_ref[0]).astype(BF16)

    def proj(c0, c1):
        return _dot(h, w_ref[:, c0:c1])

    u_ref[0] = proj(_C_U, _C_Q)
    q_ref[0] = (proj(_C_Q, _C_KC) * HEAD_DIM ** -0.5).astype(BF16)
    kc_ref[0] = proj(_C_KC, _C_VC)
    vc_ref[0] = proj(_C_VC, _C_KVS)
    tm = x_ref.shape[1]
    pos = pl.program_id(1) * tm + lax.broadcasted_iota(jnp.int32, (tm, LANES), 0)
    lane = lax.broadcasted_iota(jnp.int32, (tm, LANES), 1)
    feat = jnp.where(lane == pos // SLC_BLOCK, 1.0,
                     jnp.where(lane == FEAT_OFFSET_LANE, (pos % SLC_BLOCK).astype(F32), 0.0)).astype(BF16)
    for ref, c0 in ((kvs_ref, _C_KVS), (kvw_ref, _C_KVW)):
        z = proj(c0, c0 + N_KV * KV_PACK).astype(BF16)
        for g in range(N_KV):
            ref[0, g, :, 0:KV_PACK] = z[:, g * KV_PACK:(g + 1) * KV_PACK]
            ref[0, g, :, KV_PACK:KEY_WIDTH] = feat
    z = jax.nn.sigmoid(proj(_C_GN, _C_GBR))
    for g in range(N_KV):
        gn_ref[0, g] = z[:, g * LANES:(g + 1) * LANES]
    gbr_ref[0] = jax.nn.sigmoid(proj(_C_GBR, _C_END)).astype(BF16)


def _mixin(x, shift, scale, w):
    bsz, s, d = x.shape
    tm = min(ROW_TILE, s)

    def row(n, dt):
        return pl.BlockSpec((1, tm, n), lambda b, i: (b, i, 0)), jax.ShapeDtypeStruct((bsz, s, n), dt)

    def grouped(n, dt):
        return (pl.BlockSpec((1, N_KV, tm, n), lambda b, i: (b, 0, i, 0)),
                jax.ShapeDtypeStruct((bsz, N_KV, s, n), dt))

    outs = [row(POOL_WIDTH, F32), row(Q_WIDTH, BF16), row(KV_WIDTH, F32), row(KV_WIDTH, F32),
            grouped(KEY_WIDTH, BF16), grouped(KEY_WIDTH, BF16), grouped(LANES, F32), row(2 * D_MODEL, BF16)]
    mod = pl.BlockSpec((1, 1, d), lambda b, i: (b, 0, 0))
    return pl.pallas_call(
        _mixin_kernel,
        grid=(bsz, s // tm),
        in_specs=[pl.BlockSpec((1, tm, d), lambda b, i: (b, i, 0)), mod, mod, _const_spec(w.shape)],
        out_specs=[o[0] for o in outs],
        out_shape=[o[1] for o in outs],
        compiler_params=_params(2),
        name="mixer_in_proj",
    )(x, shift, scale, w)


def _compress_kernel(k_ref, v_ref, posk_ref, posv_ref, w1k_ref, w1v_ref, w2k_ref, w2v_ref, o_ref, nxt_ref):
    r = k_ref.shape[1]

    def hidden(x_ref, pos_ref, w1_ref):
        x = x_ref[0]
        first = _dot((x + pos_ref[0:1, :]).astype(BF16), w1_ref[0])
        nxt_ref[0:r, :] = _dot((x + pos_ref[1:2, :]).astype(BF16), w1_ref[1])
        nxt_ref[r:r + 8, :] = jnp.zeros((8, nxt_ref.shape[1]), F32)
        return jax.nn.gelu(first + nxt_ref[1:r + 1, :]).astype(BF16)

    hk = hidden(k_ref, posk_ref, w1k_ref)
    hv = hidden(v_ref, posv_ref, w1v_ref)
    o_ref[0] = (_dot(hk, w2k_ref[...]) + _dot(hv, w2v_ref[...])).astype(BF16)


def _compress(kc, vc, posk, posv, w1k, w1v, w2k, w2v):
    bsz, r, n = kc.shape
    hid = w1k.shape[2]
    row = pl.BlockSpec((1, r, n), lambda b: (b, 0, 0))
    return pl.pallas_call(
        _compress_kernel,
        grid=(bsz,),
        in_specs=[row, row, _const_spec(posk.shape), _const_spec(posv.shape),
                  _const_spec(w1k.shape), _const_spec(w1v.shape),
                  _const_spec(w2k.shape), _const_spec(w2v.shape)],
        out_specs=pl.BlockSpec((1, r, N_KV * KV_PACK), lambda b: (b, 0, 0)),
        out_shape=jax.ShapeDtypeStruct((bsz, r, N_KV * KV_PACK), BF16),
        scratch_shapes=[pltpu.VMEM((r + 8, hid), F32)],
        compiler_params=_params(1),
        name="compress_mlp",
    )(kc, vc, posk, posv, w1k, w1v, w2k, w2v)


def _alibi_slope(head):
    return 2.0 ** (-ALIBI_MAX * (head + 1) / N_HEADS)


def _attn_kernel(q_ref, gate_ref, kvc_ref, kps_ref, kpw_ref, cbias_ref, ovl_ref, tris_ref, triw_ref,
                 o_ref, score_ref, qs_ref, s_ref, mt_ref, m_ref, lt_ref, acc_ref, *, n_sel):
    qb = pl.program_id(1)
    q0 = qb * Q_BLOCK
    rows = HEADS_PER_KV * Q_BLOCK
    n_slc = ovl_ref.shape[0]
    r = kvc_ref.shape[1]

    lane = lax.broadcasted_iota(jnp.int32, (Q_BLOCK, LANES), 1)
    qf = q_ref[0].astype(F32)
    q_heads = []
    for hd in range(N_HEADS):
        slab = qf[:, (hd // 2) * LANES:(hd // 2 + 1) * LANES]
        if hd % 2:
            slab = pltpu.roll(slab, HEAD_DIM, 1)
        q_heads.append(jnp.where(lane < HEAD_DIM, slab, 0.0))

    t = q0 + (lax.broadcasted_iota(jnp.int32, (rows, 1), 0) & (Q_BLOCK - 1))
    last_cmp = (t - (CMP_BLOCK - 1)) >> 4
    visible = lax.broadcasted_iota(jnp.int32, (rows, r), 1) <= last_cmp
    o_cmp, p_sums = [], []
    for g in range(N_KV):
        q_pad = jnp.concatenate(q_heads[g * HEADS_PER_KV:(g + 1) * HEADS_PER_KV], axis=0).astype(BF16)
        kvc = kvc_ref[0, :, g * KV_PACK:(g + 1) * KV_PACK]
        p_cmp = _masked_softmax(_dot_nt(q_pad, kvc) + cbias_ref[g], visible)
        o_cmp.append(_dot(p_cmp.astype(BF16), kvc))
        p_sum = p_cmp[0:Q_BLOCK]
        for h in range(1, HEADS_PER_KV):
            p_sum = p_sum + p_cmp[h * Q_BLOCK:(h + 1) * Q_BLOCK]
        p_sums.append(p_sum)

    p_all = jnp.concatenate(p_sums, axis=0)
    p_hi = p_all.astype(BF16)
    p_lo = (p_all - p_hi.astype(F32)).astype(BF16)
    ovl = ovl_ref[...]
    imp_t = _dot_nt(ovl, p_hi) + _dot_nt(ovl, p_lo)

    blk = lax.broadcasted_iota(jnp.int32, (n_slc, LANES), 0)
    forced = (blk == 0) | (blk == qb) | (blk == qb - 1)
    score = jnp.where(blk > qb, NEG, jnp.where(forced, FORCE, imp_t))
    score_ref[...] = score

    def rank_step(i, rank):
        ri = score_ref[pl.ds(i, 1), :]
        beats = (ri > score) | ((ri == score) & (blk > i))
        return rank + jnp.where(beats, 1.0, 0.0)

    rank = lax.fori_loop(0, qb + 1, rank_step, jnp.zeros((n_slc, LANES), F32))
    sel_t = jnp.where(rank < n_sel, jnp.where(blk <= qb, 1.0, 0.0), 0.0).astype(BF16)
    sel_t = jnp.concatenate([sel_t, jnp.zeros((LANES - n_slc, LANES), BF16)], axis=0)
    eye = jnp.where(lax.broadcasted_iota(jnp.int32, (LANES, LANES), 0)
                    == lax.broadcasted_iota(jnp.int32, (LANES, LANES), 1), 1.0, 0.0).astype(BF16)
    sel = _dot_nt(eye, sel_t)

    lane_row = lax.broadcasted_iota(jnp.int32, (1, LANES), 1)
    jrel = (lane_row - qb).astype(F32)
    in_window = (lane_row >= qb - WINDOW // SLC_BLOCK) & (lane_row <= qb)

    def feature_rows(hd):
        slope = _alibi_slope(hd)
        tail = jnp.where(lane_row == FEAT_OFFSET_LANE, slope, 0.0)
        block_bias = slope * SLC_BLOCK * jrel
        slc_row = jnp.where(lane_row < SLC_BLOCK, block_bias, tail)
        win_row = jnp.where(lane_row < SLC_BLOCK, jnp.where(in_window, block_bias, -BIG), tail)
        return slc_row, win_row

    def fold(x, op):
        out = x[:, 0:LANES]
        for j in range(1, x.shape[1] // LANES):
            out = op(out, x[:, j * LANES:(j + 1) * LANES])
        return out

    chunk_blocks = SLC_CHUNK // SLC_BLOCK
    n_full = qb // chunk_blocks
    tri_s = jnp.concatenate([tris_ref[qb % chunk_blocks]] * HEADS_PER_KV, axis=0)
    w0 = pl.multiple_of(jnp.maximum(q0 + Q_BLOCK - WIN_KEYS, 0), Q_BLOCK)
    tri_w = jnp.concatenate([triw_ref[jnp.minimum(qb, WIN_KEYS // SLC_BLOCK - 1)]] * HEADS_PER_KV, axis=0)

    o_win = []
    for g in range(N_KV):
        masked_out = jnp.where(lane < SLC_BLOCK, (sel[g * Q_BLOCK:(g + 1) * Q_BLOCK] - 1.0) * BIG, 0.0)
        qs, qw = [], []
        for h in range(HEADS_PER_KV):
            hd = g * HEADS_PER_KV + h
            slc_row, win_row = feature_rows(hd)
            qs.append(jnp.concatenate([q_heads[hd], masked_out + slc_row], axis=1))
            qw.append(jnp.concatenate([q_heads[hd], jnp.broadcast_to(win_row, (Q_BLOCK, LANES))], axis=1))
        qs_ref[g] = jnp.concatenate(qs, axis=0).astype(BF16)

        kp = kpw_ref[0, g, pl.ds(w0, WIN_KEYS), :]
        s = _dot_nt(jnp.concatenate(qw, axis=0).astype(BF16), kp) + tri_w
        e = jnp.exp(s - jnp.max(s, axis=-1, keepdims=True))
        o_win.append(_dot(e.astype(BF16), kp[:, 0:KV_PACK]) * (1.0 / jnp.sum(e, axis=-1, keepdims=True)))

    def chunk_start(c):
        return pl.multiple_of(c * SLC_CHUNK, SLC_CHUNK)

    def score_chunk(g, c, tri):
        s = _dot_nt(qs_ref[g], kps_ref[0, g, pl.ds(chunk_start(c), SLC_CHUNK), :])
        if tri is not None:
            s = s + tri
        s_ref[g, :, pl.ds(chunk_start(c), SLC_CHUNK)] = s
        mt_ref[g] = jnp.maximum(mt_ref[g], fold(s, jnp.maximum))

    def value_chunk(g, c):
        m = m_ref[g]
        es = [jnp.exp(s_ref[g, :, pl.ds(chunk_start(c) + j * LANES, LANES)] - m)
              for j in range(SLC_CHUNK // LANES)]
        lt_ref[g] = lt_ref[g] + functools.reduce(jnp.add, es)
        values = kps_ref[0, g, pl.ds(chunk_start(c), SLC_CHUNK), 0:KV_PACK]
        acc_ref[g] = acc_ref[g] + _dot(jnp.concatenate(es, axis=1).astype(BF16), values)

    def sweep(step, last):
        def pair(i, carry):
            for u in range(2):
                for g in range(N_KV):
                    step(g, 2 * i + u)
            return carry

        lax.fori_loop(0, n_full // 2, pair, 0)

        @pl.when(n_full % 2 == 1)
        def _():
            for g in range(N_KV):
                step(g, n_full - 1)

        for g in range(N_KV):
            last(g, n_full)

    mt_ref[...] = jnp.full(mt_ref.shape, M_INIT, F32)
    sweep(lambda g, c: score_chunk(g, c, None), lambda g, c: score_chunk(g, c, tri_s))
    for g in range(N_KV):
        m_ref[g] = jnp.broadcast_to(jnp.max(mt_ref[g], axis=-1, keepdims=True), (rows, LANES))
    lt_ref[...] = jnp.zeros(lt_ref.shape, F32)
    acc_ref[...] = jnp.zeros(acc_ref.shape, F32)
    sweep(value_chunk, value_chunk)

    outs = []
    for g in range(N_KV):
        l = jnp.sum(lt_ref[g], axis=-1, keepdims=True)
        o_slc = acc_ref[g] * jnp.where(l > 0.0, 1.0 / l, 0.0)
        gt = gate_ref[0, g]
        heads = []
        for h in range(HEADS_PER_KV):
            rs = slice(h * Q_BLOCK, (h + 1) * Q_BLOCK)
            heads.append(gt[:, 3 * h:3 * h + 1] * o_cmp[g][rs] + gt[:, 3 * h + 1:3 * h + 2] * o_slc[rs]
                         + gt[:, 3 * h + 2:3 * h + 3] * o_win[g][rs])
        outs += [jnp.where(lane < HEAD_DIM, pltpu.roll(heads[2 * j], HEAD_DIM, 1), heads[2 * j + 1])
                 for j in range(HEADS_PER_KV // 2)]
    o_ref[0] = jnp.concatenate(outs, axis=1).astype(BF16)


def _attention(q, gates, kvc, kps, kpw):
    bsz, s, _ = q.shape
    r = kvc.shape[1]
    n_cmp = r - 1
    n_slc = s // SLC_BLOCK
    n_sel = min(N_SELECT, n_slc)
    rows = HEADS_PER_KV * Q_BLOCK
    assert n_slc <= SLC_BLOCK and s % SLC_CHUNK == 0 and s >= WIN_KEYS

    slopes = np.array([_alibi_slope(hd) for hd in range(N_HEADS)])
    slope_rows = np.repeat(slopes.reshape(N_KV, HEADS_PER_KV), Q_BLOCK, axis=1).reshape(N_KV, rows, 1)
    cbias = slope_rows * (CMP_STRIDE * np.arange(r))[None, None, :]
    start = np.arange(r)[None, :] * CMP_STRIDE
    blk = np.arange(n_slc)[:, None] * SLC_BLOCK
    overlap_t = ((start < blk + SLC_BLOCK) & (start + CMP_BLOCK > blk) & (np.arange(r)[None, :] < n_cmp))

    ql = np.arange(Q_BLOCK)[:, None]
    kl = np.arange(SLC_BLOCK)[None, :]
    lower = np.where(kl > ql, -BIG, 0.0)
    upper = np.where(kl <= ql, -BIG, 0.0)
    chunk_blocks = SLC_CHUNK // SLC_BLOCK
    tri_slc = np.zeros((chunk_blocks, Q_BLOCK, SLC_CHUNK))
    for j in range(chunk_blocks):
        tri_slc[j, :, j * SLC_BLOCK:(j + 1) * SLC_BLOCK] = lower
    win_blocks = WINDOW // SLC_BLOCK
    lead = WIN_KEYS // SLC_BLOCK - 1
    tri_win = np.zeros((lead + 1, Q_BLOCK, WIN_KEYS))
    for v in range(lead + 1):
        diag = v
        tri_win[v, :, diag * SLC_BLOCK:(diag + 1) * SLC_BLOCK] = lower
        if diag >= win_blocks:
            old = diag - win_blocks
            tri_win[v, :, old * SLC_BLOCK:(old + 1) * SLC_BLOCK] = upper

    return pl.pallas_call(
        functools.partial(_attn_kernel, n_sel=n_sel),
        grid=(bsz, s // Q_BLOCK),
        in_specs=[pl.BlockSpec((1, Q_BLOCK, Q_WIDTH), lambda b, i: (b, i, 0)),
                  pl.BlockSpec((1, N_KV, Q_BLOCK, LANES), lambda b, i: (b, 0, i, 0)),
                  pl.BlockSpec((1, r, N_KV * KV_PACK), lambda b, i: (b, 0, 0)),
                  pl.BlockSpec((1, N_KV, s, KEY_WIDTH), lambda b, i: (b, 0, 0, 0)),
                  pl.BlockSpec((1, N_KV, s, KEY_WIDTH), lambda b, i: (b, 0, 0, 0)),
                  _const_spec(cbias.shape), _const_spec(overlap_t.shape),
                  _const_spec(tri_slc.shape), _const_spec(tri_win.shape)],
        out_specs=pl.BlockSpec((1, Q_BLOCK, Q_WIDTH), lambda b, i: (b, i, 0)),
        out_shape=jax.ShapeDtypeStruct((bsz, s, Q_WIDTH), BF16),
        scratch_shapes=[pltpu.VMEM((n_slc, LANES), F32),
                        pltpu.VMEM((N_KV, rows, KEY_WIDTH), BF16),
                        pltpu.VMEM((N_KV, rows, s), F32)]
        + [pltpu.VMEM((N_KV, rows, LANES), F32)] * 4,
        compiler_params=_params(2),
        name="sparse_attention",
    )(q, gates, kvc, kps, kpw, jnp.asarray(cbias, F32), jnp.asarray(overlap_t, BF16),
      jnp.asarray(tri_slc, F32), jnp.asarray(tri_win, F32))


def _mixout_kernel(x_ref, gate_ref, uprev_ref, u_ref, o_ref, gbr_ref, pw_ref, ps_ref,
                   wa_ref, wb_ref, wo_ref, g_ref, b_ref, out_ref, ubuf_ref):
    i = pl.program_id(1)
    tm = u_ref.shape[1]
    ubuf_ref[0:POOL_HALO, :] = jnp.where(i == 0, 0.0, uprev_ref[0])
    ubuf_ref[POOL_HALO:POOL_HALO + tm, :] = u_ref[0]
    t = i * tm + lax.broadcasted_iota(jnp.int32, (tm, 1), 0)

    mixed = []
    for gi, w in enumerate(POOL_WINDOWS):
        cs = slice(gi * POOL_GROUP, (gi + 1) * POOL_GROUP)
        cur = ubuf_ref[POOL_HALO:POOL_HALO + tm, cs]
        total = cur
        for k in range(1, w):
            total = total + ubuf_ref[POOL_HALO - k:POOL_HALO - k + tm, cs]
        inv_cnt = 1.0 / jnp.minimum(t + 1, w).astype(F32)
        delta = (total * inv_cnt - cur).astype(BF16)
        mixed.append((_dot(delta, pw_ref[gi]) * ps_ref[:, cs]).astype(BF16))
    y_a = _dot(jnp.concatenate(mixed, axis=1), wa_ref[...])
    y_b = _dot(o_ref[0], wb_ref[...])
    d = y_a.shape[1]
    y = (gbr_ref[0, :, 0:d].astype(F32) * y_a + gbr_ref[0, :, d:2 * d].astype(F32) * y_b).astype(BF16)
    y = _dot(y, wo_ref[...])
    out_ref[0] = _layer_norm(ALPHA * x_ref[0] + gate_ref[0] * y, g_ref[...], b_ref[...])


def _mixout(x, gate, u, o, gbr, pool_w, pool_scale, w_a, w_b, w_o, g, b):
    bsz, s, d = x.shape
    tm = min(ROW_TILE, s)
    halo_blocks = tm // POOL_HALO

    def row(n):
        return pl.BlockSpec((1, tm, n), lambda bi, i: (bi, i, 0))

    return pl.pallas_call(
        _mixout_kernel,
        grid=(bsz, s // tm),
        in_specs=[row(d), pl.BlockSpec((1, 1, d), lambda bi, i: (bi, 0, 0)),
                  pl.BlockSpec((1, POOL_HALO, POOL_WIDTH),
                               lambda bi, i: (bi, jnp.maximum(i * halo_blocks - 1, 0), 0)),
                  row(POOL_WIDTH), row(Q_WIDTH), row(2 * d),
                  _const_spec(pool_w.shape), _const_spec((1, POOL_WIDTH)),
                  _const_spec(w_a.shape), _const_spec(w_b.shape), _const_spec(w_o.shape),
                  _const_spec((1, d)), _const_spec((1, d))],
        out_specs=row(d),
        out_shape=jax.ShapeDtypeStruct((bsz, s, d), F32),
        scratch_shapes=[pltpu.VMEM((POOL_HALO + tm, POOL_WIDTH), F32)],
        compiler_params=_params(2),
        name="pool_merge_out",
    )(x, gate, u, u, o, gbr, pool_w, pool_scale.reshape(1, POOL_WIDTH), w_a, w_b, w_o,
      g.reshape(1, d), b.reshape(1, d))


def _mixer_in_weights(w):
    sizes = (POOL_WIDTH, Q_WIDTH) + (KV_WIDTH,) * 6 + (3 * N_HEADS, 2 * D_MODEL)
    offs = np.concatenate([[0], np.cumsum(sizes)])
    u, q, k_cmp, v_cmp, k_slc, v_slc, k_win, v_win, g_nsa, g_br = [w[:, offs[i]:offs[i + 1]] for i in range(10)]
    cols = [u, q, k_cmp, v_cmp]
    for k, v in ((k_slc, v_slc), (k_win, v_win)):
        for g in range(N_KV):
            cols += [k[:, g * HEAD_DIM:(g + 1) * HEAD_DIM], v[:, g * HEAD_DIM:(g + 1) * HEAD_DIM]]
    per_g = 3 * HEADS_PER_KV
    for g in range(N_KV):
        cols += [g_nsa[:, g * per_g:(g + 1) * per_g], jnp.zeros((w.shape[0], LANES - per_g), w.dtype)]
    cols.append(g_br)
    return jnp.concatenate(cols, axis=1).astype(BF16)


def _compress_weights(pos, w1, w2, value_slot):
    def expand_w1(half):
        wh = half.reshape(CMP_STRIDE, HEAD_DIM, CMP_HIDDEN)
        z = jnp.zeros((CMP_STRIDE, N_KV, HEAD_DIM, N_KV, CMP_HIDDEN), w1.dtype)
        for g in range(N_KV):
            z = z.at[:, g, :, g, :].set(wh)
        return z.reshape(CMP_STRIDE * KV_WIDTH, N_KV * CMP_HIDDEN)

    half_rows = CMP_STRIDE * HEAD_DIM
    w1_big = jnp.stack([expand_w1(w1[:half_rows]), expand_w1(w1[half_rows:])]).astype(BF16)
    w2_big = jnp.zeros((N_KV, CMP_HIDDEN, N_KV, 2, HEAD_DIM), w2.dtype)
    for g in range(N_KV):
        w2_big = w2_big.at[g, :, g, value_slot, :].set(w2)
    w2_big = w2_big.reshape(N_KV * CMP_HIDDEN, N_KV * KV_PACK).astype(BF16)
    pos_rows = jnp.broadcast_to(pos.reshape(2, CMP_STRIDE, 1, HEAD_DIM), (2, CMP_STRIDE, N_KV, HEAD_DIM))
    return pos_rows.reshape(2, CMP_STRIDE * KV_WIDTH), w1_big, w2_big


def kernel(x, c, ln_in_g, ln_in_b, w_ada, b_ada, ffn1_w_in, ffn1_w_out, ln1_g, ln1_b, w_mix_in, pool_w, pool_scale,
           cmp_pos_k, cmp_k_w1, cmp_k_w2, cmp_pos_v, cmp_v_w1, cmp_v_w2, w_branch_a, w_branch_b, w_mix_out,
           ln2_g, ln2_b, ffn2_w_in, ffn2_w_out, ln3_g, ln3_b):
    bsz, s, d = x.shape
    for l in range(DEPTH):
        ada = _ada(c, w_ada[l], b_ada[l]).reshape(bsz, 3, 3, 1, d)
        mod = lambda i, j: ada[:, i, j]

        pre_ln = l == 0
        x = _ffn(x, mod(0, 0), mod(0, 1), mod(0, 2), ln_in_g, ln_in_b, ln1_g[l], ln1_b[l],
                 ffn1_w_in[l].astype(BF16), ffn1_w_out[l].astype(BF16), pre_ln=pre_ln)

        u, q, kc, vc, kvs, kvw, gates, gbr = _mixin(x, mod(1, 0), mod(1, 1), _mixer_in_weights(w_mix_in[l]))
        posk, w1k, w2k = _compress_weights(cmp_pos_k[l], cmp_k_w1[l], cmp_k_w2[l], 0)
        posv, w1v, w2v = _compress_weights(cmp_pos_v[l], cmp_v_w1[l], cmp_v_w2[l], 1)
        rows = s // CMP_STRIDE
        kvc = _compress(kc.reshape(bsz, rows, CMP_STRIDE * KV_WIDTH), vc.reshape(bsz, rows, CMP_STRIDE * KV_WIDTH),
                        posk, posv, w1k, w1v, w2k, w2v)
        o = _attention(q, gates, kvc, kvs, kvw)
        x = _mixout(x, mod(1, 2), u, o, gbr, pool_w[l].astype(BF16), pool_scale[l],
                    w_branch_a[l].astype(BF16), w_branch_b[l].astype(BF16), w_mix_out[l].astype(BF16),
                    ln2_g[l], ln2_b[l])

        x = _ffn(x, mod(2, 0), mod(2, 1), mod(2, 2), ln_in_g, ln_in_b, ln3_g[l], ln3_b[l],
                 ffn2_w_in[l].astype(BF16), ffn2_w_out[l].astype(BF16), pre_ln=False)
    return x
```

```python
import functools

import numpy as np
import jax
import jax.numpy as jnp
from jax import lax
from jax.experimental import pallas as pl
from jax.experimental.pallas import tpu as pltpu

F32 = jnp.float32
BF16 = jnp.bfloat16

D_MODEL = 1024
POOL_WIDTH = D_MODEL // 2
POOL_WINDOWS = (2, 4, 8, 16)
POOL_GROUP = POOL_WIDTH // len(POOL_WINDOWS)
POOL_HALO = 16
HEAD_DIM = 64
N_HEADS = (D_MODEL // 2) // HEAD_DIM
N_KV = 2
HEADS_PER_KV = N_HEADS // N_KV
Q_WIDTH = N_HEADS * HEAD_DIM
KV_WIDTH = N_KV * HEAD_DIM
CMP_STRIDE = 16
CMP_BLOCK = 2 * CMP_STRIDE
CMP_HIDDEN = 4 * HEAD_DIM
SLC_BLOCK = 64
N_SELECT = 16
WINDOW = 512
Q_BLOCK = SLC_BLOCK
ALIBI_MAX = 8.0
D_FF = 2816
DEPTH = 1
ALPHA = (2.0 * DEPTH) ** 0.25
LN_EPS = 1e-5
NEG = -1e30
FORCE = 1e9

LANES = 128
KV_PACK = 2 * HEAD_DIM
KEY_WIDTH = 2 * KV_PACK
FEAT_OFFSET_LANE = SLC_BLOCK
BIG = 1e30
M_INIT = -3e38
FF_CHUNK = 256
ROW_TILE = 512
SLC_UNIT = 256
RANK_UNROLL = 2
WIN_NQ = 4
SEL_NQ = 2
LOG2E = 1.4426950408889634
WIN_KEYS = WINDOW + 2 * Q_BLOCK
VMEM_LIMIT = 52 * 1024 * 1024

_C_U = 0
_C_Q = _C_U + POOL_WIDTH
_C_KC = _C_Q + Q_WIDTH
_C_VC = _C_KC + KV_WIDTH
_C_KVS = _C_VC + KV_WIDTH
_C_KVW = _C_KVS + N_KV * KV_PACK
_C_GN = _C_KVW + N_KV * KV_PACK
_C_GBR = _C_GN + N_KV * LANES
_C_END = _C_GBR + 2 * D_MODEL


def _dot(a, b):
    return jnp.dot(a, b, preferred_element_type=F32)


def _dot_nt(a, b):
    return lax.dot_general(a, b, (((1,), (1,)), ((), ())), preferred_element_type=F32)


def _layer_norm(x, g, b):
    mu = jnp.mean(x, axis=-1, keepdims=True)
    xc = x - mu
    var = jnp.mean(xc * xc, axis=-1, keepdims=True)
    return xc * lax.rsqrt(var + LN_EPS) * g + b


def _masked_softmax(s, mask):
    sm = jnp.where(mask, s, NEG)
    m = jnp.max(sm, axis=-1, keepdims=True)
    e = jnp.exp(sm - m)
    p = e * (1.0 / jnp.sum(e, axis=-1, keepdims=True))
    return jnp.where(mask, p, 0.0)


def _const_spec(shape):
    nd = len(shape)
    return pl.BlockSpec(shape, lambda *_: (0,) * nd, pipeline_mode=pl.Buffered(1))


def _params(n_grid):
    return pltpu.CompilerParams(dimension_semantics=("parallel",) * n_grid, vmem_limit_bytes=VMEM_LIMIT)


def _ada_kernel(c_ref, w_ref, b_ref, o_ref):
    c = c_ref[...]
    c_act = (c * jax.nn.sigmoid(c)).astype(BF16)
    o_ref[...] = _dot(c_act, w_ref[...].astype(BF16)) + b_ref[...]


def _ada(c, w, b):
    bsz, d = c.shape
    n = w.shape[1]
    tn = D_MODEL
    return pl.pallas_call(
        _ada_kernel,
        grid=(n // tn,),
        in_specs=[pl.BlockSpec((bsz, d), lambda j: (0, 0)),
                  pl.BlockSpec((d, tn), lambda j: (0, j)),
                  pl.BlockSpec((1, tn), lambda j: (0, j))],
        out_specs=pl.BlockSpec((bsz, tn), lambda j: (0, j)),
        out_shape=jax.ShapeDtypeStruct((bsz, n), F32),
        compiler_params=_params(1),
        name="ada_proj",
    )(c, w, b.reshape(1, n))


def _ffn_kernel(x_ref, shift_ref, scale_ref, gate_ref, g0_ref, b0_ref, g1_ref, b1_ref,
                w_in_ref, w_out_ref, o_ref, *, pre_ln):
    x = x_ref[0]
    if pre_ln:
        x = _layer_norm(x, g0_ref[...], b0_ref[...])
    h = (x * (1.0 + scale_ref[0]) + shift_ref[0]).astype(BF16)
    acc = jnp.zeros(x.shape, F32)
    for j in range(D_FF // FF_CHUNK):
        c0 = j * FF_CHUNK
        gt = _dot(h, w_in_ref[:, c0:c0 + FF_CHUNK])
        up = _dot(h, w_in_ref[:, D_FF + c0:D_FF + c0 + FF_CHUNK])
        act = (gt * jax.nn.sigmoid(gt) * up).astype(BF16)
        acc = acc + _dot(act, w_out_ref[c0:c0 + FF_CHUNK, :])
    y = ALPHA * x + 0.5 * gate_ref[0] * acc
    o_ref[0] = _layer_norm(y, g1_ref[...], b1_ref[...])


def _ffn(x, shift, scale, gate, g0, b0, g1, b1, w_in, w_out, *, pre_ln):
    bsz, s, d = x.shape
    tm = min(ROW_TILE, s)
    row = pl.BlockSpec((1, tm, d), lambda b, i: (b, i, 0))
    mod = pl.BlockSpec((1, 1, d), lambda b, i: (b, 0, 0))
    vec = _const_spec((1, d))
    return pl.pallas_call(
        functools.partial(_ffn_kernel, pre_ln=pre_ln),
        grid=(bsz, s // tm),
        in_specs=[row, mod, mod, mod, vec, vec, vec, vec,
                  _const_spec(w_in.shape), _const_spec(w_out.shape)],
        out_specs=row,
        out_shape=jax.ShapeDtypeStruct((bsz, s, d), F32),
        compiler_params=_params(2),
        name="ffn_block",
    )(x, shift, scale, gate, g0.reshape(1, d), b0.reshape(1, d), g1.reshape(1, d), b1.reshape(1, d),
      w_in, w_out)


def _mixin_kernel(x_ref, shift_ref, scale_ref, w_ref,
                  u_ref, q_ref, kc_ref, vc_ref, kvs_ref, kvw_ref, gn_ref, gbr_ref):
    h = (x_ref[0] * (1.0 + scale_ref[0]) + shift_ref[0]).astype(BF16)

    def proj(c0, c1):
        return _dot(h, w_ref[:, c0:c1])

    u_ref[0] = proj(_C_U, _C_Q)
    q_ref[0] = (proj(_C_Q, _C_KC) * HEAD_DIM ** -0.5).astype(BF16)
    kc_ref[0] = proj(_C_KC, _C_VC)
    vc_ref[0] = proj(_C_VC, _C_KVS)
    tm = x_ref.shape[1]
    pos = pl.program_id(1) * tm + lax.broadcasted_iota(jnp.int32, (tm, LANES), 0)
    lane = lax.broadcasted_iota(jnp.int32, (tm, LANES), 1)
    feat = jnp.where(lane == pos // SLC_BLOCK, 1.0,
                     jnp.where(lane == FEAT_OFFSET_LANE, (pos % SLC_BLOCK).astype(F32), 0.0)).astype(BF16)
    for ref, c0 in ((kvs_ref, _C_KVS), (kvw_ref, _C_KVW)):
        z = proj(c0, c0 + N_KV * KV_PACK).astype(BF16)
        for g in range(N_KV):
            ref[0, g, :, 0:KV_PACK] = z[:, g * KV_PACK:(g + 1) * KV_PACK]
            ref[0, g, :, KV_PACK:KEY_WIDTH] = feat
    z = jax.nn.sigmoid(proj(_C_GN, _C_GBR))
    for g in range(N_KV):
        gn_ref[0, g] = z[:, g * LANES:(g + 1) * LANES]
    gbr_ref[0] = jax.nn.sigmoid(proj(_C_GBR, _C_END)).astype(BF16)


def _mixin(x, shift, scale, w):
    bsz, s, d = x.shape
    tm = min(ROW_TILE, s)

    def row(n, dt):
        return pl.BlockSpec((1, tm, n), lambda b, i: (b, i, 0)), jax.ShapeDtypeStruct((bsz, s, n), dt)

    def grouped(n, dt):
        return (pl.BlockSpec((1, N_KV, tm, n), lambda b, i: (b, 0, i, 0)),
                jax.ShapeDtypeStruct((bsz, N_KV, s, n), dt))

    outs = [row(POOL_WIDTH, F32), row(Q_WIDTH, BF16), row(KV_WIDTH, F32), row(KV_WIDTH, F32),
            grouped(KEY_WIDTH, BF16), grouped(KEY_WIDTH, BF16), grouped(LANES, F32), row(2 * D_MODEL, BF16)]
    mod = pl.BlockSpec((1, 1, d), lambda b, i: (b, 0, 0))
    return pl.pallas_call(
        _mixin_kernel,
        grid=(bsz, s // tm),
        in_specs=[pl.BlockSpec((1, tm, d), lambda b, i: (b, i, 0)), mod, mod, _const_spec(w.shape)],
        out_specs=[o[0] for o in outs],
        out_shape=[o[1] for o in outs],
        compiler_params=_params(2),
        name="mixer_in_proj",
    )(x, shift, scale, w)


def _compress_kernel(k_ref, v_ref, posk_ref, posv_ref, w1k_ref, w1v_ref, w2k_ref, w2v_ref, o_ref, nxt_ref):
    r = k_ref.shape[1]

    def hidden(x_ref, pos_ref, w1_ref):
        x = x_ref[0]
        first = _dot((x + pos_ref[0:1, :]).astype(BF16), w1_ref[0])
        nxt_ref[0:r, :] = _dot((x + pos_ref[1:2, :]).astype(BF16), w1_ref[1])
        nxt_ref[r:r + 8, :] = jnp.zeros((8, nxt_ref.shape[1]), F32)
        return jax.nn.gelu(first + nxt_ref[1:r + 1, :]).astype(BF16)

    hk = hidden(k_ref, posk_ref, w1k_ref)
    hv = hidden(v_ref, posv_ref, w1v_ref)
    o_ref[0] = (_dot(hk, w2k_ref[...]) + _dot(hv, w2v_ref[...])).astype(BF16)


def _compress(kc, vc, posk, posv, w1k, w1v, w2k, w2v):
    bsz, r, n = kc.shape
    hid = w1k.shape[2]
    row = pl.BlockSpec((1, r, n), lambda b: (b, 0, 0))
    return pl.pallas_call(
        _compress_kernel,
        grid=(bsz,),
        in_specs=[row, row, _const_spec(posk.shape), _const_spec(posv.shape),
                  _const_spec(w1k.shape), _const_spec(w1v.shape),
                  _const_spec(w2k.shape), _const_spec(w2v.shape)],
        out_specs=pl.BlockSpec((1, r, N_KV * KV_PACK), lambda b: (b, 0, 0)),
        out_shape=jax.ShapeDtypeStruct((bsz, r, N_KV * KV_PACK), BF16),
        scratch_shapes=[pltpu.VMEM((r + 8, hid), F32)],
        compiler_params=_params(1),
        name="compress_mlp",
    )(kc, vc, posk, posv, w1k, w1v, w2k, w2v)


def _alibi_slope(head):
    return 2.0 ** (-ALIBI_MAX * (head + 1) / N_HEADS)


def _padded_heads(q_tile):
    lane = lax.broadcasted_iota(jnp.int32, (Q_BLOCK, LANES), 1)
    qf = q_tile.astype(F32)
    heads = []
    for hd in range(N_HEADS):
        slab = qf[:, (hd // 2) * LANES:(hd // 2 + 1) * LANES]
        if hd % 2:
            slab = pltpu.roll(slab, HEAD_DIM, 1)
        heads.append(jnp.where(lane < HEAD_DIM, slab, 0.0))
    return heads


def _pack_heads(o_rows):
    lane = lax.broadcasted_iota(jnp.int32, (Q_BLOCK, LANES), 1)
    pairs = []
    for j in range(HEADS_PER_KV // 2):
        even = o_rows[(2 * j) * Q_BLOCK:(2 * j + 1) * Q_BLOCK]
        odd = o_rows[(2 * j + 1) * Q_BLOCK:(2 * j + 2) * Q_BLOCK]
        pairs.append(jnp.where(lane < HEAD_DIM, pltpu.roll(even, HEAD_DIM, 1), odd))
    return jnp.concatenate(pairs, axis=1)


def _feature_rows(qb, hd):
    lane_row = lax.broadcasted_iota(jnp.int32, (1, LANES), 1)
    slope = _alibi_slope(hd)
    tail = jnp.where(lane_row == FEAT_OFFSET_LANE, slope, 0.0)
    block_bias = slope * SLC_BLOCK * (lane_row - qb).astype(F32)
    in_window = (lane_row >= qb - WINDOW // SLC_BLOCK) & (lane_row <= qb)
    slc_row = jnp.where(lane_row < SLC_BLOCK, block_bias, tail)
    win_row = jnp.where(lane_row < SLC_BLOCK, jnp.where(in_window, block_bias, -BIG), tail)
    return slc_row, win_row


def _window_kernel(q_ref, kpw_ref, triw_ref, o_ref):
    for k in range(WIN_NQ):
        qb = pl.program_id(1) * WIN_NQ + k
        heads = _padded_heads(q_ref[0, k * Q_BLOCK:(k + 1) * Q_BLOCK, :])
        w0 = pl.multiple_of(jnp.maximum((qb + 1) * Q_BLOCK - WIN_KEYS, 0), Q_BLOCK)
        tri = jnp.concatenate([triw_ref[jnp.minimum(qb, WIN_KEYS // SLC_BLOCK - 1)]] * HEADS_PER_KV, axis=0)
        for g in range(N_KV):
            qw = [jnp.concatenate([heads[hd], jnp.broadcast_to(_feature_rows(qb, hd)[1], (Q_BLOCK, LANES))], axis=1)
                  for hd in range(g * HEADS_PER_KV, (g + 1) * HEADS_PER_KV)]
            kp = kpw_ref[0, g, pl.ds(w0, WIN_KEYS), :]
            s = _dot_nt(jnp.concatenate(qw, axis=0).astype(BF16), kp) + tri
            e = jnp.exp(s - jnp.max(s, axis=-1, keepdims=True))
            o = _dot(e.astype(BF16), kp[:, 0:KV_PACK]) * (1.0 / jnp.sum(e, axis=-1, keepdims=True))
            gw = HEADS_PER_KV * HEAD_DIM
            o_ref[0, k * Q_BLOCK:(k + 1) * Q_BLOCK, g * gw:(g + 1) * gw] = _pack_heads(o).astype(BF16)


def _window(q, kpw, tri_win):
    bsz, s, _ = q.shape
    qt = WIN_NQ * Q_BLOCK
    return pl.pallas_call(
        _window_kernel,
        grid=(bsz, s // qt),
        in_specs=[pl.BlockSpec((1, qt, Q_WIDTH), lambda b, i: (b, i, 0)),
                  pl.BlockSpec((1, N_KV, s, KEY_WIDTH), lambda b, i: (b, 0, 0, 0)),
                  _const_spec(tri_win.shape)],
        out_specs=pl.BlockSpec((1, qt, Q_WIDTH), lambda b, i: (b, i, 0)),
        out_shape=jax.ShapeDtypeStruct((bsz, s, Q_WIDTH), BF16),
        compiler_params=_params(2),
        name="window_attention",
    )(q, kpw, tri_win)


def _select_kernel(q_ref, kvc_ref, cbias_ref, ovl_ref, ocmp_ref, selneg_ref, first_ref, score_ref, *, n_sel):
    step = pl.program_id(1)
    rows = SEL_NQ * HEADS_PER_KV * Q_BLOCK
    n_slc = ovl_ref.shape[0]
    r = kvc_ref.shape[1]
    lanes_q = SEL_NQ * N_KV * Q_BLOCK
    heads = [_padded_heads(q_ref[0, k * Q_BLOCK:(k + 1) * Q_BLOCK, :]) for k in range(SEL_NQ)]

    row = lax.broadcasted_iota(jnp.int32, (rows, 1), 0)
    t = (step * SEL_NQ + row // (HEADS_PER_KV * Q_BLOCK)) * Q_BLOCK + (row & (Q_BLOCK - 1))
    last_cmp = (t - (CMP_BLOCK - 1)) >> 4
    visible = lax.broadcasted_iota(jnp.int32, (rows, r), 1) <= last_cmp
    p_sums = [[None] * N_KV for _ in range(SEL_NQ)]
    gw = HEADS_PER_KV * HEAD_DIM
    for g in range(N_KV):
        q_pad = jnp.concatenate([heads[k][g * HEADS_PER_KV + h] for k in range(SEL_NQ)
                                 for h in range(HEADS_PER_KV)], axis=0).astype(BF16)
        kvc = kvc_ref[0, :, g * KV_PACK:(g + 1) * KV_PACK]
        p_cmp = _masked_softmax(_dot_nt(q_pad, kvc) + cbias_ref[g], visible)
        o_cmp = _dot(p_cmp.astype(BF16), kvc)
        for k in range(SEL_NQ):
            base = k * HEADS_PER_KV * Q_BLOCK
            ocmp_ref[0, k * Q_BLOCK:(k + 1) * Q_BLOCK, g * gw:(g + 1) * gw] = _pack_heads(
                o_cmp[base:base + HEADS_PER_KV * Q_BLOCK]).astype(BF16)
            p_sum = p_cmp[base:base + Q_BLOCK]
            for h in range(1, HEADS_PER_KV):
                p_sum = p_sum + p_cmp[base + h * Q_BLOCK:base + (h + 1) * Q_BLOCK]
            p_sums[k][g] = p_sum

    p_all = jnp.concatenate([p_sums[k][g] for k in range(SEL_NQ) for g in range(N_KV)], axis=0)
    p_hi = p_all.astype(BF16)
    p_lo = (p_all - p_hi.astype(F32)).astype(BF16)
    ovl = ovl_ref[...]
    imp_t = _dot_nt(ovl, p_hi) + _dot_nt(ovl, p_lo)

    blk = lax.broadcasted_iota(jnp.int32, (n_slc, lanes_q), 0)
    qb = step * SEL_NQ + lax.broadcasted_iota(jnp.int32, (n_slc, lanes_q), 1) // (N_KV * Q_BLOCK)
    forced = (blk == 0) | (blk == qb) | (blk == qb - 1)
    score = jnp.where(blk > qb, NEG, jnp.where(forced, FORCE, imp_t))
    score_ref[...] = score

    def rank_step(i, ranks):
        out = []
        for u in range(RANK_UNROLL):
            ri = score_ref[pl.ds(i * RANK_UNROLL + u, 1), :]
            beats = (ri > score) | ((ri == score) & (blk > i * RANK_UNROLL + u))
            out.append(ranks[u] + jnp.where(beats, 1.0, 0.0))
        return tuple(out)

    ranks = lax.fori_loop(0, (step * SEL_NQ + SEL_NQ - 1) // RANK_UNROLL + 1, rank_step,
                          (jnp.zeros((n_slc, lanes_q), F32),) * RANK_UNROLL)
    chosen = (functools.reduce(jnp.add, ranks) < n_sel) & (blk <= qb)

    oldest = jnp.where(chosen & (blk >= 1), blk, n_slc).astype(F32)
    sel_t = jnp.where(chosen, 1.0, 0.0).astype(BF16)
    sel_t = jnp.concatenate([sel_t, jnp.zeros((LANES - n_slc, lanes_q), BF16)], axis=0)
    eye = jnp.where(lax.broadcasted_iota(jnp.int32, (lanes_q, lanes_q), 0)
                    == lax.broadcasted_iota(jnp.int32, (lanes_q, lanes_q), 1), 1.0, 0.0).astype(BF16)
    sel = _dot_nt(eye, sel_t)
    lane = lax.broadcasted_iota(jnp.int32, (Q_BLOCK, LANES), 1)
    for k in range(SEL_NQ):
        lanes_k = slice(k * N_KV * Q_BLOCK, (k + 1) * N_KV * Q_BLOCK)
        first = jnp.min(jnp.min(oldest[:, lanes_k], axis=1, keepdims=True), axis=0, keepdims=True)
        first_ref[0, k] = jnp.broadcast_to(first, first_ref.shape[2:]).astype(jnp.int32)
        for g in range(N_KV):
            base = (k * N_KV + g) * Q_BLOCK
            selneg_ref[0, g, k * Q_BLOCK:(k + 1) * Q_BLOCK, :] = jnp.where(
                lane < SLC_BLOCK, (sel[base:base + Q_BLOCK] - 1.0) * BIG, 0.0).astype(BF16)


def _select(q, kvc, cbias, overlap_t, n_sel):
    bsz, s, _ = q.shape
    r = kvc.shape[1]
    n_slc = overlap_t.shape[0]
    qt = SEL_NQ * Q_BLOCK
    return pl.pallas_call(
        functools.partial(_select_kernel, n_sel=n_sel),
        grid=(bsz, s // qt),
        in_specs=[pl.BlockSpec((1, qt, Q_WIDTH), lambda b, i: (b, i, 0)),
                  pl.BlockSpec((1, r, N_KV * KV_PACK), lambda b, i: (b, 0, 0)),
                  _const_spec(cbias.shape), _const_spec(overlap_t.shape)],
        out_specs=[pl.BlockSpec((1, qt, Q_WIDTH), lambda b, i: (b, i, 0)),
                   pl.BlockSpec((1, N_KV, qt, LANES), lambda b, i: (b, 0, i, 0)),
                   pl.BlockSpec((1, SEL_NQ, 8, LANES), lambda b, i: (b, i, 0, 0))],
        out_shape=[jax.ShapeDtypeStruct((bsz, s, Q_WIDTH), BF16),
                   jax.ShapeDtypeStruct((bsz, N_KV, s, LANES), BF16),
                   jax.ShapeDtypeStruct((bsz, s // Q_BLOCK, 8, LANES), jnp.int32)],
        scratch_shapes=[pltpu.VMEM((n_slc, SEL_NQ * N_KV * Q_BLOCK), F32)],
        compiler_params=_params(2),
        name="compressed_attention_select",
    )(q, kvc, cbias, overlap_t)


def _selected_kernel(first_ref, q_ref, selneg_ref, gate_ref, ocmp_ref, owin_ref, kps_ref, tris_ref,
                     o_ref, qs_ref, s_ref, mt_ref, m_ref, lt_ref, acc_ref):
    qb = pl.program_id(1)
    rows = HEADS_PER_KV * Q_BLOCK
    heads = _padded_heads(q_ref[0])
    for g in range(N_KV):
        masked_out = selneg_ref[0, g].astype(F32)
        qs = [jnp.concatenate([heads[hd], masked_out + _feature_rows(qb, hd)[0]], axis=1)
              for hd in range(g * HEADS_PER_KV, (g + 1) * HEADS_PER_KV)]
        qs_ref[g] = jnp.concatenate(qs, axis=0).astype(BF16)

    unit_blocks = SLC_UNIT // SLC_BLOCK
    diag_unit = qb // unit_blocks
    first_unit = jnp.minimum(first_ref[pl.program_id(0), qb], qb) // unit_blocks
    lead = jnp.minimum(first_unit, 1)
    n_units = diag_unit - first_unit + 1 + lead

    def unit_of(i):
        return jnp.where((i < lead) | (i >= n_units), 0, first_unit + i - lead)

    def unit_start(i):
        return pl.multiple_of(unit_of(i) * SLC_UNIT, SLC_UNIT)

    def slot_start(i):
        return pl.multiple_of(i * SLC_UNIT, SLC_UNIT)

    def score_unit(g, i):
        variant = jnp.where(i >= n_units, unit_blocks + 1,
                            jnp.where(unit_of(i) == diag_unit, qb % unit_blocks, unit_blocks))
        tri = jnp.concatenate([tris_ref[variant]] * HEADS_PER_KV, axis=0)
        s = (_dot_nt(qs_ref[g], kps_ref[0, g, pl.ds(unit_start(i), SLC_UNIT), :]) + tri) * LOG2E
        s_ref[g, :, pl.ds(slot_start(i), SLC_UNIT)] = s
        mt_ref[g] = jnp.maximum(jnp.maximum(mt_ref[g], s[:, 0:LANES]), s[:, LANES:SLC_UNIT])

    def value_unit(g, i):
        m = m_ref[g]
        es = [jnp.exp2(s_ref[g, :, pl.ds(slot_start(i) + j * LANES, LANES)] - m) for j in range(SLC_UNIT // LANES)]
        lt_ref[g] = lt_ref[g] + functools.reduce(jnp.add, es)
        values = kps_ref[0, g, pl.ds(unit_start(i), SLC_UNIT), 0:KV_PACK]
        acc_ref[g] = acc_ref[g] + _dot(jnp.concatenate(es, axis=1).astype(BF16), values)

    def sweep(step):
        def pair(p, carry):
            for u in range(2):
                for g in range(N_KV):
                    step(g, 2 * p + u)
            return carry

        lax.fori_loop(0, (n_units + 1) // 2, pair, 0)

    mt_ref[...] = jnp.full(mt_ref.shape, M_INIT, F32)
    sweep(score_unit)
    for g in range(N_KV):
        m_ref[g] = jnp.broadcast_to(jnp.max(mt_ref[g], axis=-1, keepdims=True), (rows, LANES))
    lt_ref[...] = jnp.zeros(lt_ref.shape, F32)
    acc_ref[...] = jnp.zeros(acc_ref.shape, F32)
    sweep(value_unit)

    lane = lax.broadcasted_iota(jnp.int32, (Q_BLOCK, LANES), 1)
    outs = []
    for g in range(N_KV):
        l = jnp.sum(lt_ref[g], axis=-1, keepdims=True)
        o_slc = _pack_heads(acc_ref[g] * jnp.where(l > 0.0, 1.0 / l, 0.0))
        gt = gate_ref[0, g]
        for j in range(HEADS_PER_KV // 2):
            cols = slice((g * HEADS_PER_KV // 2 + j) * LANES, (g * HEADS_PER_KV // 2 + j + 1) * LANES)
            branches = (ocmp_ref[0, :, cols].astype(F32), o_slc[:, j * LANES:(j + 1) * LANES],
                        owin_ref[0, :, cols].astype(F32))
            total = None
            for c, branch in enumerate(branches):
                even, odd = 3 * (2 * j) + c, 3 * (2 * j + 1) + c
                gate = jnp.where(lane < HEAD_DIM, gt[:, even:even + 1], gt[:, odd:odd + 1])
                total = gate * branch if total is None else total + gate * branch
            outs.append(total)
    o_ref[0] = jnp.concatenate(outs, axis=1).astype(BF16)


def _selected(first, q, selneg, gates, o_cmp, o_win, kps, tri_slc):
    bsz, s, _ = q.shape
    rows = HEADS_PER_KV * Q_BLOCK

    def row(n):
        return pl.BlockSpec((1, Q_BLOCK, n), lambda b, i, first_ref: (b, i, 0))

    def grouped(n):
        return pl.BlockSpec((1, N_KV, Q_BLOCK, n), lambda b, i, first_ref: (b, 0, i, 0))

    grid_spec = pltpu.PrefetchScalarGridSpec(
        num_scalar_prefetch=1,
        grid=(bsz, s // Q_BLOCK),
        in_specs=[row(Q_WIDTH), grouped(LANES), grouped(LANES), row(Q_WIDTH), row(Q_WIDTH),
                  pl.BlockSpec((1, N_KV, s, KEY_WIDTH), lambda b, i, first_ref: (b, 0, 0, 0)),
                  _const_spec(tri_slc.shape)],
        out_specs=row(Q_WIDTH),
        scratch_shapes=[pltpu.VMEM((N_KV, rows, KEY_WIDTH), BF16),
                        pltpu.VMEM((N_KV, rows, s + SLC_UNIT), F32)]
        + [pltpu.VMEM((N_KV, rows, LANES), F32)] * 4)
    return pl.pallas_call(
        _selected_kernel,
        grid_spec=grid_spec,
        out_shape=jax.ShapeDtypeStruct((bsz, s, Q_WIDTH), BF16),
        compiler_params=_params(2),
        name="selected_attention",
    )(first, q, selneg, gates, o_cmp, o_win, kps, tri_slc)


def _attention_v2(q, gates, kvc, kps, kpw):
    bsz, s, _ = q.shape
    r = kvc.shape[1]
    n_cmp = r - 1
    n_slc = s // SLC_BLOCK
    n_sel = min(N_SELECT, n_slc)
    rows = HEADS_PER_KV * Q_BLOCK
    assert n_slc <= SLC_BLOCK and n_slc % RANK_UNROLL == 0 and s % SLC_UNIT == 0 and s >= WIN_KEYS
    assert s % (WIN_NQ * Q_BLOCK) == 0 and s % (SEL_NQ * Q_BLOCK) == 0

    slopes = np.array([_alibi_slope(hd) for hd in range(N_HEADS)])
    slope_rows = np.repeat(slopes.reshape(N_KV, HEADS_PER_KV), Q_BLOCK, axis=1).reshape(N_KV, rows, 1)
    cbias = np.tile(slope_rows * (CMP_STRIDE * np.arange(r))[None, None, :], (1, SEL_NQ, 1))
    start = np.arange(r)[None, :] * CMP_STRIDE
    blk = np.arange(n_slc)[:, None] * SLC_BLOCK
    overlap_t = ((start < blk + SLC_BLOCK) & (start + CMP_BLOCK > blk) & (np.arange(r)[None, :] < n_cmp))

    ql = np.arange(Q_BLOCK)[:, None]
    kl = np.arange(SLC_BLOCK)[None, :]
    lower = np.where(kl > ql, -BIG, 0.0)
    upper = np.where(kl <= ql, -BIG, 0.0)
    unit_blocks = SLC_UNIT // SLC_BLOCK
    tri_slc = np.zeros((unit_blocks + 2, Q_BLOCK, SLC_UNIT))
    for j in range(unit_blocks):
        tri_slc[j, :, j * SLC_BLOCK:(j + 1) * SLC_BLOCK] = lower
    tri_slc[unit_blocks + 1] = -BIG
    win_blocks = WINDOW // SLC_BLOCK
    lead = WIN_KEYS // SLC_BLOCK - 1
    tri_win = np.zeros((lead + 1, Q_BLOCK, WIN_KEYS))
    for v in range(lead + 1):
        diag = v
        tri_win[v, :, diag * SLC_BLOCK:(diag + 1) * SLC_BLOCK] = lower
        if diag >= win_blocks:
            old = diag - win_blocks
            tri_win[v, :, old * SLC_BLOCK:(old + 1) * SLC_BLOCK] = upper

    o_win = _window(q, kpw, jnp.asarray(tri_win, F32))
    o_cmp, selneg, first = _select(q, kvc, jnp.asarray(cbias, F32), jnp.asarray(overlap_t, BF16), n_sel)
    return _selected(first[:, :, 0, 0], q, selneg, gates, o_cmp, o_win, kps, jnp.asarray(tri_slc, F32))


def _attn_kernel(q_ref, gate_ref, kvc_ref, kps_ref, kpw_ref, cbias_ref, ovl_ref, tris_ref, triw_ref,
                 o_ref, score_ref, qs_ref, s_ref, mt_ref, m_ref, lt_ref, acc_ref, *, n_sel):
    qb = pl.program_id(1)
    q0 = qb * Q_BLOCK
    rows = HEADS_PER_KV * Q_BLOCK
    n_slc = ovl_ref.shape[0]
    r = kvc_ref.shape[1]

    lane = lax.broadcasted_iota(jnp.int32, (Q_BLOCK, LANES), 1)
    qf = q_ref[0].astype(F32)
    q_heads = []
    for hd in range(N_HEADS):
        slab = qf[:, (hd // 2) * LANES:(hd // 2 + 1) * LANES]
        if hd % 2:
            slab = pltpu.roll(slab, HEAD_DIM, 1)
        q_heads.append(jnp.where(lane < HEAD_DIM, slab, 0.0))

    t = q0 + (lax.broadcasted_iota(jnp.int32, (rows, 1), 0) & (Q_BLOCK - 1))
    last_cmp = (t - (CMP_BLOCK - 1)) >> 4
    visible = lax.broadcasted_iota(jnp.int32, (rows, r), 1) <= last_cmp
    o_cmp, p_sums = [], []
    for g in range(N_KV):
        q_pad = jnp.concatenate(q_heads[g * HEADS_PER_KV:(g + 1) * HEADS_PER_KV], axis=0).astype(BF16)
        kvc = kvc_ref[0, :, g * KV_PACK:(g + 1) * KV_PACK]
        p_cmp = _masked_softmax(_dot_nt(q_pad, kvc) + cbias_ref[g], visible)
        o_cmp.append(_dot(p_cmp.astype(BF16), kvc))
        p_sum = p_cmp[0:Q_BLOCK]
        for h in range(1, HEADS_PER_KV):
            p_sum = p_sum + p_cmp[h * Q_BLOCK:(h + 1) * Q_BLOCK]
        p_sums.append(p_sum)

    p_all = jnp.concatenate(p_sums, axis=0)
    p_hi = p_all.astype(BF16)
    p_lo = (p_all - p_hi.astype(F32)).astype(BF16)
    ovl = ovl_ref[...]
    imp_t = _dot_nt(ovl, p_hi) + _dot_nt(ovl, p_lo)

    lane_row = lax.broadcasted_iota(jnp.int32, (1, LANES), 1)
    jrel = (lane_row - qb).astype(F32)
    in_window = (lane_row >= qb - WINDOW // SLC_BLOCK) & (lane_row <= qb)

    def feature_rows(hd):
        slope = _alibi_slope(hd)
        tail = jnp.where(lane_row == FEAT_OFFSET_LANE, slope, 0.0)
        block_bias = slope * SLC_BLOCK * jrel
        slc_row = jnp.where(lane_row < SLC_BLOCK, block_bias, tail)
        win_row = jnp.where(lane_row < SLC_BLOCK, jnp.where(in_window, block_bias, -BIG), tail)
        return slc_row, win_row

    w0 = pl.multiple_of(jnp.maximum(q0 + Q_BLOCK - WIN_KEYS, 0), Q_BLOCK)
    tri_w = jnp.concatenate([triw_ref[jnp.minimum(qb, WIN_KEYS // SLC_BLOCK - 1)]] * HEADS_PER_KV, axis=0)
    o_win = []
    for g in range(N_KV):
        qw = [jnp.concatenate([q_heads[hd], jnp.broadcast_to(feature_rows(hd)[1], (Q_BLOCK, LANES))], axis=1)
              for hd in range(g * HEADS_PER_KV, (g + 1) * HEADS_PER_KV)]
        kp = kpw_ref[0, g, pl.ds(w0, WIN_KEYS), :]
        s = _dot_nt(jnp.concatenate(qw, axis=0).astype(BF16), kp) + tri_w
        e = jnp.exp(s - jnp.max(s, axis=-1, keepdims=True))
        o_win.append(_dot(e.astype(BF16), kp[:, 0:KV_PACK]) * (1.0 / jnp.sum(e, axis=-1, keepdims=True)))

    blk = lax.broadcasted_iota(jnp.int32, (n_slc, LANES), 0)
    forced = (blk == 0) | (blk == qb) | (blk == qb - 1)
    score = jnp.where(blk > qb, NEG, jnp.where(forced, FORCE, imp_t))
    score_ref[...] = score

    def rank_step(i, ranks):
        out = []
        for u in range(RANK_UNROLL):
            row = i * RANK_UNROLL + u
            ri = score_ref[pl.ds(row, 1), :]
            beats = (ri > score) | ((ri == score) & (blk > row))
            out.append(ranks[u] + jnp.where(beats, 1.0, 0.0))
        return tuple(out)

    ranks = lax.fori_loop(0, qb // RANK_UNROLL + 1, rank_step,
                          (jnp.zeros((n_slc, LANES), F32),) * RANK_UNROLL)
    rank = functools.reduce(jnp.add, ranks)
    chosen = (rank < n_sel) & (blk <= qb)
    first = jnp.min(jnp.where(chosen & (blk >= 1), blk, n_slc).astype(F32)).astype(jnp.int32)
    first = jnp.minimum(first, qb)
    sel_t = jnp.where(chosen, 1.0, 0.0).astype(BF16)
    sel_t = jnp.concatenate([sel_t, jnp.zeros((LANES - n_slc, LANES), BF16)], axis=0)
    eye = jnp.where(lax.broadcasted_iota(jnp.int32, (LANES, LANES), 0)
                    == lax.broadcasted_iota(jnp.int32, (LANES, LANES), 1), 1.0, 0.0).astype(BF16)
    sel = _dot_nt(eye, sel_t)

    for g in range(N_KV):
        masked_out = jnp.where(lane < SLC_BLOCK, (sel[g * Q_BLOCK:(g + 1) * Q_BLOCK] - 1.0) * BIG, 0.0)
        qs = [jnp.concatenate([q_heads[hd], masked_out + feature_rows(hd)[0]], axis=1)
              for hd in range(g * HEADS_PER_KV, (g + 1) * HEADS_PER_KV)]
        qs_ref[g] = jnp.concatenate(qs, axis=0).astype(BF16)

    unit_blocks = SLC_UNIT // SLC_BLOCK
    diag_unit = qb // unit_blocks
    first_unit = first // unit_blocks
    lead = jnp.minimum(first_unit, 1)
    n_units = diag_unit - first_unit + 1 + lead

    def unit_of(i):
        return jnp.where((i < lead) | (i >= n_units), 0, first_unit + i - lead)

    def unit_start(i):
        return pl.multiple_of(unit_of(i) * SLC_UNIT, SLC_UNIT)

    def slot_start(i):
        return pl.multiple_of(i * SLC_UNIT, SLC_UNIT)

    def score_unit(g, i):
        variant = jnp.where(i >= n_units, unit_blocks + 1,
                            jnp.where(unit_of(i) == diag_unit, qb % unit_blocks, unit_blocks))
        tri = jnp.concatenate([tris_ref[variant]] * HEADS_PER_KV, axis=0)
        s = (_dot_nt(qs_ref[g], kps_ref[0, g, pl.ds(unit_start(i), SLC_UNIT), :]) + tri) * LOG2E
        s_ref[g, :, pl.ds(slot_start(i), SLC_UNIT)] = s
        mt_ref[g] = jnp.maximum(jnp.maximum(mt_ref[g], s[:, 0:LANES]), s[:, LANES:SLC_UNIT])

    def value_unit(g, i):
        m = m_ref[g]
        es = [jnp.exp2(s_ref[g, :, pl.ds(slot_start(i) + j * LANES, LANES)] - m) for j in range(SLC_UNIT // LANES)]
        lt_ref[g] = lt_ref[g] + functools.reduce(jnp.add, es)
        values = kps_ref[0, g, pl.ds(unit_start(i), SLC_UNIT), 0:KV_PACK]
        acc_ref[g] = acc_ref[g] + _dot(jnp.concatenate(es, axis=1).astype(BF16), values)

    def sweep(step):
        def pair(p, carry):
            for u in range(2):
                for g in range(N_KV):
                    step(g, 2 * p + u)
            return carry

        lax.fori_loop(0, (n_units + 1) // 2, pair, 0)

    mt_ref[...] = jnp.full(mt_ref.shape, M_INIT, F32)
    sweep(score_unit)
    for g in range(N_KV):
        m_ref[g] = jnp.broadcast_to(jnp.max(mt_ref[g], axis=-1, keepdims=True), (rows, LANES))
    lt_ref[...] = jnp.zeros(lt_ref.shape, F32)
    acc_ref[...] = jnp.zeros(acc_ref.shape, F32)
    sweep(value_unit)

    outs = []
    for g in range(N_KV):
        l = jnp.sum(lt_ref[g], axis=-1, keepdims=True)
        o_slc = acc_ref[g] * jnp.where(l > 0.0, 1.0 / l, 0.0)
        gt = gate_ref[0, g]
        heads = []
        for h in range(HEADS_PER_KV):
            rs = slice(h * Q_BLOCK, (h + 1) * Q_BLOCK)
            heads.append(gt[:, 3 * h:3 * h + 1] * o_cmp[g][rs] + gt[:, 3 * h + 1:3 * h + 2] * o_slc[rs]
                         + gt[:, 3 * h + 2:3 * h + 3] * o_win[g][rs])
        outs += [jnp.where(lane < HEAD_DIM, pltpu.roll(heads[2 * j], HEAD_DIM, 1), heads[2 * j + 1])
                 for j in range(HEADS_PER_KV // 2)]
    o_ref[0] = jnp.concatenate(outs, axis=1).astype(BF16)


def _attention(q, gates, kvc, kps, kpw):
    bsz, s, _ = q.shape
    r = kvc.shape[1]
    n_cmp = r - 1
    n_slc = s // SLC_BLOCK
    n_sel = min(N_SELECT, n_slc)
    rows = HEADS_PER_KV * Q_BLOCK
    assert n_slc <= SLC_BLOCK and n_slc % RANK_UNROLL == 0 and s % SLC_UNIT == 0 and s >= WIN_KEYS

    slopes = np.array([_alibi_slope(hd) for hd in range(N_HEADS)])
    slope_rows = np.repeat(slopes.reshape(N_KV, HEADS_PER_KV), Q_BLOCK, axis=1).reshape(N_KV, rows, 1)
    cbias = slope_rows * (CMP_STRIDE * np.arange(r))[None, None, :]
    start = np.arange(r)[None, :] * CMP_STRIDE
    blk = np.arange(n_slc)[:, None] * SLC_BLOCK
    overlap_t = ((start < blk + SLC_BLOCK) & (start + CMP_BLOCK > blk) & (np.arange(r)[None, :] < n_cmp))

    ql = np.arange(Q_BLOCK)[:, None]
    kl = np.arange(SLC_BLOCK)[None, :]
    lower = np.where(kl > ql, -BIG, 0.0)
    upper = np.where(kl <= ql, -BIG, 0.0)
    unit_blocks = SLC_UNIT // SLC_BLOCK
    tri_slc = np.zeros((unit_blocks + 2, Q_BLOCK, SLC_UNIT))
    for j in range(unit_blocks):
        tri_slc[j, :, j * SLC_BLOCK:(j + 1) * SLC_BLOCK] = lower
    tri_slc[unit_blocks + 1] = -BIG
    win_blocks = WINDOW // SLC_BLOCK
    lead = WIN_KEYS // SLC_BLOCK - 1
    tri_win = np.zeros((lead + 1, Q_BLOCK, WIN_KEYS))
    for v in range(lead + 1):
        diag = v
        tri_win[v, :, diag * SLC_BLOCK:(diag + 1) * SLC_BLOCK] = lower
        if diag >= win_blocks:
            old = diag - win_blocks
            tri_win[v, :, old * SLC_BLOCK:(old + 1) * SLC_BLOCK] = upper

    return pl.pallas_call(
        functools.partial(_attn_kernel, n_sel=n_sel),
        grid=(bsz, s // Q_BLOCK),
        in_specs=[pl.BlockSpec((1, Q_BLOCK, Q_WIDTH), lambda b, i: (b, i, 0)),
                  pl.BlockSpec((1, N_KV, Q_BLOCK, LANES), lambda b, i: (b, 0, i, 0)),
                  pl.BlockSpec((1, r, N_KV * KV_PACK), lambda b, i: (b, 0, 0)),
                  pl.BlockSpec((1, N_KV, s, KEY_WIDTH), lambda b, i: (b, 0, 0, 0)),
                  pl.BlockSpec((1, N_KV, s, KEY_WIDTH), lambda b, i: (b, 0, 0, 0)),
                  _const_spec(cbias.shape), _const_spec(overlap_t.shape),
                  _const_spec(tri_slc.shape), _const_spec(tri_win.shape)],
        out_specs=pl.BlockSpec((1, Q_BLOCK, Q_WIDTH), lambda b, i: (b, i, 0)),
        out_shape=jax.ShapeDtypeStruct((bsz, s, Q_WIDTH), BF16),
        scratch_shapes=[pltpu.VMEM((n_slc, LANES), F32),
                        pltpu.VMEM((N_KV, rows, KEY_WIDTH), BF16),
                        pltpu.VMEM((N_KV, rows, s + SLC_UNIT), F32)]
        + [pltpu.VMEM((N_KV, rows, LANES), F32)] * 4,
        compiler_params=_params(2),
        name="sparse_attention",
    )(q, gates, kvc, kps, kpw, jnp.asarray(cbias, F32), jnp.asarray(overlap_t, BF16),
      jnp.asarray(tri_slc, F32), jnp.asarray(tri_win, F32))


def _mixout_kernel(x_ref, gate_ref, uprev_ref, u_ref, o_ref, gbr_ref, pw_ref, ps_ref,
                   wa_ref, wb_ref, wo_ref, g_ref, b_ref, out_ref, ubuf_ref):
    i = pl.program_id(1)
    tm = u_ref.shape[1]
    ubuf_ref[0:POOL_HALO, :] = jnp.where(i == 0, 0.0, uprev_ref[0])
    ubuf_ref[POOL_HALO:POOL_HALO + tm, :] = u_ref[0]
    t = i * tm + lax.broadcasted_iota(jnp.int32, (tm, 1), 0)

    mixed = []
    for gi, w in enumerate(POOL_WINDOWS):
        cs = slice(gi * POOL_GROUP, (gi + 1) * POOL_GROUP)
        cur = ubuf_ref[POOL_HALO:POOL_HALO + tm, cs]
        total = cur
        for k in range(1, w):
            total = total + ubuf_ref[POOL_HALO - k:POOL_HALO - k + tm, cs]
        inv_cnt = 1.0 / jnp.minimum(t + 1, w).astype(F32)
        delta = (total * inv_cnt - cur).astype(BF16)
        mixed.append((_dot(delta, pw_ref[gi]) * ps_ref[:, cs]).astype(BF16))
    y_a = _dot(jnp.concatenate(mixed, axis=1), wa_ref[...])
    y_b = _dot(o_ref[0], wb_ref[...])
    d = y_a.shape[1]
    y = (gbr_ref[0, :, 0:d].astype(F32) * y_a + gbr_ref[0, :, d:2 * d].astype(F32) * y_b).astype(BF16)
    y = _dot(y, wo_ref[...])
    out_ref[0] = _layer_norm(ALPHA * x_ref[0] + gate_ref[0] * y, g_ref[...], b_ref[...])


def _mixout(x, gate, u, o, gbr, pool_w, pool_scale, w_a, w_b, w_o, g, b):
    bsz, s, d = x.shape
    tm = min(ROW_TILE, s)
    halo_blocks = tm // POOL_HALO

    def row(n):
        return pl.BlockSpec((1, tm, n), lambda bi, i: (bi, i, 0))

    return pl.pallas_call(
        _mixout_kernel,
        grid=(bsz, s // tm),
        in_specs=[row(d), pl.BlockSpec((1, 1, d), lambda bi, i: (bi, 0, 0)),
                  pl.BlockSpec((1, POOL_HALO, POOL_WIDTH),
                               lambda bi, i: (bi, jnp.maximum(i * halo_blocks - 1, 0), 0)),
                  row(POOL_WIDTH), row(Q_WIDTH), row(2 * d),
                  _const_spec(pool_w.shape), _const_spec((1, POOL_WIDTH)),
                  _const_spec(w_a.shape), _const_spec(w_b.shape), _const_spec(w_o.shape),
                  _const_spec((1, d)), _const_spec((1, d))],
        out_specs=row(d),
        out_shape=jax.ShapeDtypeStruct((bsz, s, d), F32),
        scratch_shapes=[pltpu.VMEM((POOL_HALO + tm, POOL_WIDTH), F32)],
        compiler_params=_params(2),
        name="pool_merge_out",
    )(x, gate, u, u, o, gbr, pool_w, pool_scale.reshape(1, POOL_WIDTH), w_a, w_b, w_o,
      g.reshape(1, d), b.reshape(1, d))


def _mixer_in_weights(w):
    sizes = (POOL_WIDTH, Q_WIDTH) + (KV_WIDTH,) * 6 + (3 * N_HEADS, 2 * D_MODEL)
    offs = np.concatenate([[0], np.cumsum(sizes)])
    u, q, k_cmp, v_cmp, k_slc, v_slc, k_win, v_win, g_nsa, g_br = [w[:, offs[i]:offs[i + 1]] for i in range(10)]
    cols = [u, q, k_cmp, v_cmp]
    for k, v in ((k_slc, v_slc), (k_win, v_win)):
        for g in range(N_KV):
            cols += [k[:, g * HEAD_DIM:(g + 1) * HEAD_DIM], v[:, g * HEAD_DIM:(g + 1) * HEAD_DIM]]
    per_g = 3 * HEADS_PER_KV
    for g in range(N_KV):
        cols += [g_nsa[:, g * per_g:(g + 1) * per_g], jnp.zeros((w.shape[0], LANES - per_g), w.dtype)]
    cols.append(g_br)
    return jnp.concatenate(cols, axis=1).astype(BF16)


def _compress_weights(pos, w1, w2, value_slot):
    def expand_w1(half):
        wh = half.reshape(CMP_STRIDE, HEAD_DIM, CMP_HIDDEN)
        z = jnp.zeros((CMP_STRIDE, N_KV, HEAD_DIM, N_KV, CMP_HIDDEN), w1.dtype)
        for g in range(N_KV):
            z = z.at[:, g, :, g, :].set(wh)
        return z.reshape(CMP_STRIDE * KV_WIDTH, N_KV * CMP_HIDDEN)

    half_rows = CMP_STRIDE * HEAD_DIM
    w1_big = jnp.stack([expand_w1(w1[:half_rows]), expand_w1(w1[half_rows:])]).astype(BF16)
    w2_big = jnp.zeros((N_KV, CMP_HIDDEN, N_KV, 2, HEAD_DIM), w2.dtype)
    for g in range(N_KV):
        w2_big = w2_big.at[g, :, g, value_slot, :].set(w2)
    w2_big = w2_big.reshape(N_KV * CMP_HIDDEN, N_KV * KV_PACK).astype(BF16)
    pos_rows = jnp.broadcast_to(pos.reshape(2, CMP_STRIDE, 1, HEAD_DIM), (2, CMP_STRIDE, N_KV, HEAD_DIM))
    return pos_rows.reshape(2, CMP_STRIDE * KV_WIDTH), w1_big, w2_big


def kernel(x, c, ln_in_g, ln_in_b, w_ada, b_ada, ffn1_w_in, ffn1_w_out, ln1_g, ln1_b, w_mix_in, pool_w, pool_scale,
           cmp_pos_k, cmp_k_w1, cmp_k_w2, cmp_pos_v, cmp_v_w1, cmp_v_w2, w_branch_a, w_branch_b, w_mix_out,
           ln2_g, ln2_b, ffn2_w_in, ffn2_w_out, ln3_g, ln3_b):
    bsz, s, d = x.shape
    for l in range(DEPTH):
        ada = _ada(c, w_ada[l], b_ada[l]).reshape(bsz, 3, 3, 1, d)
        mod = lambda i, j: ada[:, i, j]

        pre_ln = l == 0
        x = _ffn(x, mod(0, 0), mod(0, 1), mod(0, 2), ln_in_g, ln_in_b, ln1_g[l], ln1_b[l],
                 ffn1_w_in[l].astype(BF16), ffn1_w_out[l].astype(BF16), pre_ln=pre_ln)

        u, q, kc, vc, kvs, kvw, gates, gbr = _mixin(x, mod(1, 0), mod(1, 1), _mixer_in_weights(w_mix_in[l]))
        posk, w1k, w2k = _compress_weights(cmp_pos_k[l], cmp_k_w1[l], cmp_k_w2[l], 0)
        posv, w1v, w2v = _compress_weights(cmp_pos_v[l], cmp_v_w1[l], cmp_v_w2[l], 1)
        rows = s // CMP_STRIDE
        kvc = _compress(kc.reshape(bsz, rows, CMP_STRIDE * KV_WIDTH), vc.reshape(bsz, rows, CMP_STRIDE * KV_WIDTH),
                        posk, posv, w1k, w1v, w2k, w2v)
        o = _attention_v2(q, gates, kvc, kvs, kvw)
        x = _mixout(x, mod(1, 2), u, o, gbr, pool_w[l].astype(BF16), pool_scale[l],
                    w_branch_a[l].astype(BF16), w_branch_b[l].astype(BF16), w_mix_out[l].astype(BF16),
                    ln2_g[l], ln2_b[l])

        x = _ffn(x, mod(2, 0), mod(2, 1), mod(2, 2), ln_in_g, ln_in_b, ln3_g[l], ln3_b[l],
                 ffn2_w_in[l].astype(BF16), ffn2_w_out[l].astype(BF16), pre_ln=False)
    return x
```

```python
import functools

import numpy as np
import jax
import jax.numpy as jnp
from jax import lax
from jax.experimental import pallas as pl
from jax.experimental.pallas import tpu as pltpu

F32 = jnp.float32
BF16 = jnp.bfloat16

D_MODEL = 1024
POOL_WIDTH = D_MODEL // 2
POOL_WINDOWS = (2, 4, 8, 16)
POOL_GROUP = POOL_WIDTH // len(POOL_WINDOWS)
POOL_HALO = 16
HEAD_DIM = 64
N_HEADS = (D_MODEL // 2) // HEAD_DIM
N_KV = 2
HEADS_PER_KV = N_HEADS // N_KV
Q_WIDTH = N_HEADS * HEAD_DIM
KV_WIDTH = N_KV * HEAD_DIM
CMP_STRIDE = 16
CMP_BLOCK = 2 * CMP_STRIDE
CMP_HIDDEN = 4 * HEAD_DIM
SLC_BLOCK = 64
N_SELECT = 16
WINDOW = 512
Q_BLOCK = SLC_BLOCK
ALIBI_MAX = 8.0
D_FF = 2816
DEPTH = 1
ALPHA = (2.0 * DEPTH) ** 0.25
LN_EPS = 1e-5
NEG = -1e30
FORCE = 1e9

LANES = 128
KV_PACK = 2 * HEAD_DIM
KEY_WIDTH = 2 * KV_PACK
FEAT_OFFSET_LANE = SLC_BLOCK
BIG = 1e30
M_INIT = -3e38
FF_CHUNK = 256
ROW_TILE = 512
SLC_UNIT = 256
RANK_UNROLL = 2
WIN_NQ = 4
SEL_NQ = 2
SLC_NQ = 2
LOG2E = 1.4426950408889634
WIN_KEYS = WINDOW + 2 * Q_BLOCK
VMEM_LIMIT = 52 * 1024 * 1024

_C_U = 0
_C_Q = _C_U + POOL_WIDTH
_C_KC = _C_Q + Q_WIDTH
_C_VC = _C_KC + KV_WIDTH
_C_KVS = _C_VC + KV_WIDTH
_C_KVW = _C_KVS + N_KV * KV_PACK
_C_GN = _C_KVW + N_KV * KV_PACK
_C_GBR = _C_GN + N_KV * LANES
_C_END = _C_GBR + 2 * D_MODEL


def _dot(a, b):
    return jnp.dot(a, b, preferred_element_type=F32)


def _dot_nt(a, b):
    return lax.dot_general(a, b, (((1,), (1,)), ((), ())), preferred_element_type=F32)


def _layer_norm(x, g, b):
    mu = jnp.mean(x, axis=-1, keepdims=True)
    xc = x - mu
    var = jnp.mean(xc * xc, axis=-1, keepdims=True)
    return xc * lax.rsqrt(var + LN_EPS) * g + b


def _masked_softmax(s, mask):
    sm = jnp.where(mask, s, NEG)
    m = jnp.max(sm, axis=-1, keepdims=True)
    e = jnp.exp(sm - m)
    p = e * (1.0 / jnp.sum(e, axis=-1, keepdims=True))
    return jnp.where(mask, p, 0.0)


def _const_spec(shape):
    nd = len(shape)
    return pl.BlockSpec(shape, lambda *_: (0,) * nd, pipeline_mode=pl.Buffered(1))


def _params(n_grid):
    return pltpu.CompilerParams(dimension_semantics=("parallel",) * n_grid, vmem_limit_bytes=VMEM_LIMIT)


def _ada_kernel(c_ref, w_ref, b_ref, o_ref):
    c = c_ref[...]
    c_act = (c * jax.nn.sigmoid(c)).astype(BF16)
    o_ref[...] = _dot(c_act, w_ref[...].astype(BF16)) + b_ref[...]


def _ada(c, w, b):
    bsz, d = c.shape
    n = w.shape[1]
    tn = D_MODEL
    return pl.pallas_call(
        _ada_kernel,
        grid=(n // tn,),
        in_specs=[pl.BlockSpec((bsz, d), lambda j: (0, 0)),
                  pl.BlockSpec((d, tn), lambda j: (0, j)),
                  pl.BlockSpec((1, tn), lambda j: (0, j))],
        out_specs=pl.BlockSpec((bsz, tn), lambda j: (0, j)),
        out_shape=jax.ShapeDtypeStruct((bsz, n), F32),
        compiler_params=_params(1),
        name="ada_proj",
    )(c, w, b.reshape(1, n))


def _ffn_kernel(x_ref, shift_ref, scale_ref, gate_ref, g0_ref, b0_ref, g1_ref, b1_ref,
                w_in_ref, w_out_ref, o_ref, *, pre_ln):
    x = x_ref[0]
    if pre_ln:
        x = _layer_norm(x, g0_ref[...], b0_ref[...])
    h = (x * (1.0 + scale_ref[0]) + shift_ref[0]).astype(BF16)
    acc = jnp.zeros(x.shape, F32)
    for j in range(D_FF // FF_CHUNK):
        c0 = j * FF_CHUNK
        gt = _dot(h, w_in_ref[:, c0:c0 + FF_CHUNK])
        up = _dot(h, w_in_ref[:, D_FF + c0:D_FF + c0 + FF_CHUNK])
        act = (gt * jax.nn.sigmoid(gt) * up).astype(BF16)
        acc = acc + _dot(act, w_out_ref[c0:c0 + FF_CHUNK, :])
    y = ALPHA * x + 0.5 * gate_ref[0] * acc
    o_ref[0] = _layer_norm(y, g1_ref[...], b1_ref[...])


def _ffn(x, shift, scale, gate, g0, b0, g1, b1, w_in, w_out, *, pre_ln):
    bsz, s, d = x.shape
    tm = min(ROW_TILE, s)
    row = pl.BlockSpec((1, tm, d), lambda b, i: (b, i, 0))
    mod = pl.BlockSpec((1, 1, d), lambda b, i: (b, 0, 0))
    vec = _const_spec((1, d))
    return pl.pallas_call(
        functools.partial(_ffn_kernel, pre_ln=pre_ln),
        grid=(bsz, s // tm),
        in_specs=[row, mod, mod, mod, vec, vec, vec, vec,
                  _const_spec(w_in.shape), _const_spec(w_out.shape)],
        out_specs=row,
        out_shape=jax.ShapeDtypeStruct((bsz, s, d), F32),
        compiler_params=_params(2),
        name="ffn_block",
    )(x, shift, scale, gate, g0.reshape(1, d), b0.reshape(1, d), g1.reshape(1, d), b1.reshape(1, d),
      w_in, w_out)


def _mixin_kernel(x_ref, shift_ref, scale_ref, w_ref,
                  u_ref, q_ref, kc_ref, vc_ref, kvs_ref, kvw_ref, gn_ref, gbr_ref):
    h = (x_ref[0] * (1.0 + scale_ref[0]) + shift_ref[0]).astype(BF16)

    def proj(c0, c1):
        return _dot(h, w_ref[:, c0:c1])

    u_ref[0] = proj(_C_U, _C_Q)
    q_ref[0] = (proj(_C_Q, _C_KC) * HEAD_DIM ** -0.5).astype(BF16)
    kc_ref[0] = proj(_C_KC, _C_VC)
    vc_ref[0] = proj(_C_VC, _C_KVS)
    tm = x_ref.shape[1]
    pos = pl.program_id(1) * tm + lax.broadcasted_iota(jnp.int32, (tm, LANES), 0)
    lane = lax.broadcasted_iota(jnp.int32, (tm, LANES), 1)
    feat = jnp.where(lane == pos // SLC_BLOCK, 1.0,
                     jnp.where(lane == FEAT_OFFSET_LANE, (pos % SLC_BLOCK).astype(F32), 0.0)).astype(BF16)
    for ref, c0 in ((kvs_ref, _C_KVS), (kvw_ref, _C_KVW)):
        z = proj(c0, c0 + N_KV * KV_PACK).astype(BF16)
        for g in range(N_KV):
            ref[0, g, :, 0:KV_PACK] = z[:, g * KV_PACK:(g + 1) * KV_PACK]
            ref[0, g, :, KV_PACK:KEY_WIDTH] = feat
    z = jax.nn.sigmoid(proj(_C_GN, _C_GBR))
    for g in range(N_KV):
        gn_ref[0, g] = z[:, g * LANES:(g + 1) * LANES]
    gbr_ref[0] = jax.nn.sigmoid(proj(_C_GBR, _C_END)).astype(BF16)


def _mixin(x, shift, scale, w):
    bsz, s, d = x.shape
    tm = min(ROW_TILE, s)

    def row(n, dt):
        return pl.BlockSpec((1, tm, n), lambda b, i: (b, i, 0)), jax.ShapeDtypeStruct((bsz, s, n), dt)

    def grouped(n, dt):
        return (pl.BlockSpec((1, N_KV, tm, n), lambda b, i: (b, 0, i, 0)),
                jax.ShapeDtypeStruct((bsz, N_KV, s, n), dt))

    outs = [row(POOL_WIDTH, F32), row(Q_WIDTH, BF16), row(KV_WIDTH, F32), row(KV_WIDTH, F32),
            grouped(KEY_WIDTH, BF16), grouped(KEY_WIDTH, BF16), grouped(LANES, F32), row(2 * D_MODEL, BF16)]
    mod = pl.BlockSpec((1, 1, d), lambda b, i: (b, 0, 0))
    return pl.pallas_call(
        _mixin_kernel,
        grid=(bsz, s // tm),
        in_specs=[pl.BlockSpec((1, tm, d), lambda b, i: (b, i, 0)), mod, mod, _const_spec(w.shape)],
        out_specs=[o[0] for o in outs],
        out_shape=[o[1] for o in outs],
        compiler_params=_params(2),
        name="mixer_in_proj",
    )(x, shift, scale, w)


def _compress_kernel(k_ref, v_ref, posk_ref, posv_ref, w1k_ref, w1v_ref, w2k_ref, w2v_ref, o_ref, nxt_ref):
    r = k_ref.shape[1]

    def hidden(x_ref, pos_ref, w1_ref):
        x = x_ref[0]
        first = _dot((x + pos_ref[0:1, :]).astype(BF16), w1_ref[0])
        nxt_ref[0:r, :] = _dot((x + pos_ref[1:2, :]).astype(BF16), w1_ref[1])
        nxt_ref[r:r + 8, :] = jnp.zeros((8, nxt_ref.shape[1]), F32)
        return jax.nn.gelu(first + nxt_ref[1:r + 1, :]).astype(BF16)

    hk = hidden(k_ref, posk_ref, w1k_ref)
    hv = hidden(v_ref, posv_ref, w1v_ref)
    o_ref[0] = (_dot(hk, w2k_ref[...]) + _dot(hv, w2v_ref[...])).astype(BF16)


def _compress(kc, vc, posk, posv, w1k, w1v, w2k, w2v):
    bsz, r, n = kc.shape
    hid = w1k.shape[2]
    row = pl.BlockSpec((1, r, n), lambda b: (b, 0, 0))
    return pl.pallas_call(
        _compress_kernel,
        grid=(bsz,),
        in_specs=[row, row, _const_spec(posk.shape), _const_spec(posv.shape),
                  _const_spec(w1k.shape), _const_spec(w1v.shape),
                  _const_spec(w2k.shape), _const_spec(w2v.shape)],
        out_specs=pl.BlockSpec((1, r, N_KV * KV_PACK), lambda b: (b, 0, 0)),
        out_shape=jax.ShapeDtypeStruct((bsz, r, N_KV * KV_PACK), BF16),
        scratch_shapes=[pltpu.VMEM((r + 8, hid), F32)],
        compiler_params=_params(1),
        name="compress_mlp",
    )(kc, vc, posk, posv, w1k, w1v, w2k, w2v)


def _alibi_slope(head):
    return 2.0 ** (-ALIBI_MAX * (head + 1) / N_HEADS)


def _padded_heads(q_tile):
    lane = lax.broadcasted_iota(jnp.int32, (Q_BLOCK, LANES), 1)
    qf = q_tile.astype(F32)
    heads = []
    for hd in range(N_HEADS):
        slab = qf[:, (hd // 2) * LANES:(hd // 2 + 1) * LANES]
        if hd % 2:
            slab = pltpu.roll(slab, HEAD_DIM, 1)
        heads.append(jnp.where(lane < HEAD_DIM, slab, 0.0))
    return heads


def _pack_heads(o_rows):
    lane = lax.broadcasted_iota(jnp.int32, (Q_BLOCK, LANES), 1)
    pairs = []
    for j in range(HEADS_PER_KV // 2):
        even = o_rows[(2 * j) * Q_BLOCK:(2 * j + 1) * Q_BLOCK]
        odd = o_rows[(2 * j + 1) * Q_BLOCK:(2 * j + 2) * Q_BLOCK]
        pairs.append(jnp.where(lane < HEAD_DIM, pltpu.roll(even, HEAD_DIM, 1), odd))
    return jnp.concatenate(pairs, axis=1)


def _feature_rows(qb, hd):
    lane_row = lax.broadcasted_iota(jnp.int32, (1, LANES), 1)
    slope = _alibi_slope(hd)
    tail = jnp.where(lane_row == FEAT_OFFSET_LANE, slope, 0.0)
    block_bias = slope * SLC_BLOCK * (lane_row - qb).astype(F32)
    in_window = (lane_row >= qb - WINDOW // SLC_BLOCK) & (lane_row <= qb)
    slc_row = jnp.where(lane_row < SLC_BLOCK, block_bias, tail)
    win_row = jnp.where(lane_row < SLC_BLOCK, jnp.where(in_window, block_bias, -BIG), tail)
    return slc_row, win_row


def _window_kernel(q_ref, kpw_ref, triw_ref, o_ref):
    for k in range(WIN_NQ):
        qb = pl.program_id(1) * WIN_NQ + k
        heads = _padded_heads(q_ref[0, k * Q_BLOCK:(k + 1) * Q_BLOCK, :])
        w0 = pl.multiple_of(jnp.maximum((qb + 1) * Q_BLOCK - WIN_KEYS, 0), Q_BLOCK)
        tri = jnp.concatenate([triw_ref[jnp.minimum(qb, WIN_KEYS // SLC_BLOCK - 1)]] * HEADS_PER_KV, axis=0)
        for g in range(N_KV):
            qw = [jnp.concatenate([heads[hd], jnp.broadcast_to(_feature_rows(qb, hd)[1], (Q_BLOCK, LANES))], axis=1)
                  for hd in range(g * HEADS_PER_KV, (g + 1) * HEADS_PER_KV)]
            kp = kpw_ref[0, g, pl.ds(w0, WIN_KEYS), :]
            s = _dot_nt(jnp.concatenate(qw, axis=0).astype(BF16), kp) + tri
            e = jnp.exp(s - jnp.max(s, axis=-1, keepdims=True))
            o = _dot(e.astype(BF16), kp[:, 0:KV_PACK]) * (1.0 / jnp.sum(e, axis=-1, keepdims=True))
            gw = HEADS_PER_KV * HEAD_DIM
            o_ref[0, k * Q_BLOCK:(k + 1) * Q_BLOCK, g * gw:(g + 1) * gw] = _pack_heads(o).astype(BF16)


def _window(q, kpw, tri_win):
    bsz, s, _ = q.shape
    qt = WIN_NQ * Q_BLOCK
    return pl.pallas_call(
        _window_kernel,
        grid=(bsz, s // qt),
        in_specs=[pl.BlockSpec((1, qt, Q_WIDTH), lambda b, i: (b, i, 0)),
                  pl.BlockSpec((1, N_KV, s, KEY_WIDTH), lambda b, i: (b, 0, 0, 0)),
                  _const_spec(tri_win.shape)],
        out_specs=pl.BlockSpec((1, qt, Q_WIDTH), lambda b, i: (b, i, 0)),
        out_shape=jax.ShapeDtypeStruct((bsz, s, Q_WIDTH), BF16),
        compiler_params=_params(2),
        name="window_attention",
    )(q, kpw, tri_win)


def _select_kernel(q_ref, kvc_ref, cbias_ref, ovl_ref, ocmp_ref, selneg_ref, first_ref, score_ref, *, n_sel):
    step = pl.program_id(1)
    rows = SEL_NQ * HEADS_PER_KV * Q_BLOCK
    n_slc = ovl_ref.shape[0]
    r = kvc_ref.shape[1]
    lanes_q = SEL_NQ * N_KV * Q_BLOCK
    heads = [_padded_heads(q_ref[0, k * Q_BLOCK:(k + 1) * Q_BLOCK, :]) for k in range(SEL_NQ)]

    row = lax.broadcasted_iota(jnp.int32, (rows, 1), 0)
    t = (step * SEL_NQ + row // (HEADS_PER_KV * Q_BLOCK)) * Q_BLOCK + (row & (Q_BLOCK - 1))
    last_cmp = (t - (CMP_BLOCK - 1)) >> 4
    visible = lax.broadcasted_iota(jnp.int32, (rows, r), 1) <= last_cmp
    p_sums = [[None] * N_KV for _ in range(SEL_NQ)]
    gw = HEADS_PER_KV * HEAD_DIM
    for g in range(N_KV):
        q_pad = jnp.concatenate([heads[k][g * HEADS_PER_KV + h] for k in range(SEL_NQ)
                                 for h in range(HEADS_PER_KV)], axis=0).astype(BF16)
        kvc = kvc_ref[0, :, g * KV_PACK:(g + 1) * KV_PACK]
        p_cmp = _masked_softmax(_dot_nt(q_pad, kvc) + cbias_ref[g], visible)
        o_cmp = _dot(p_cmp.astype(BF16), kvc)
        for k in range(SEL_NQ):
            base = k * HEADS_PER_KV * Q_BLOCK
            ocmp_ref[0, k * Q_BLOCK:(k + 1) * Q_BLOCK, g * gw:(g + 1) * gw] = _pack_heads(
                o_cmp[base:base + HEADS_PER_KV * Q_BLOCK]).astype(BF16)
            p_sum = p_cmp[base:base + Q_BLOCK]
            for h in range(1, HEADS_PER_KV):
                p_sum = p_sum + p_cmp[base + h * Q_BLOCK:base + (h + 1) * Q_BLOCK]
            p_sums[k][g] = p_sum

    p_all = jnp.concatenate([p_sums[k][g] for k in range(SEL_NQ) for g in range(N_KV)], axis=0)
    p_hi = p_all.astype(BF16)
    p_lo = (p_all - p_hi.astype(F32)).astype(BF16)
    ovl = ovl_ref[...]
    imp_t = _dot_nt(ovl, p_hi) + _dot_nt(ovl, p_lo)

    blk = lax.broadcasted_iota(jnp.int32, (n_slc, lanes_q), 0)
    qb = step * SEL_NQ + lax.broadcasted_iota(jnp.int32, (n_slc, lanes_q), 1) // (N_KV * Q_BLOCK)
    forced = (blk == 0) | (blk == qb) | (blk == qb - 1)
    score = jnp.where(blk > qb, NEG, jnp.where(forced, FORCE, imp_t))
    score_ref[...] = score

    def rank_step(i, ranks):
        out = []
        for u in range(RANK_UNROLL):
            ri = score_ref[pl.ds(i * RANK_UNROLL + u, 1), :]
            beats = (ri > score) | ((ri == score) & (blk > i * RANK_UNROLL + u))
            out.append(ranks[u] + jnp.where(beats, 1.0, 0.0))
        return tuple(out)

    ranks = lax.fori_loop(0, (step * SEL_NQ + SEL_NQ - 1) // RANK_UNROLL + 1, rank_step,
                          (jnp.zeros((n_slc, lanes_q), F32),) * RANK_UNROLL)
    chosen = (functools.reduce(jnp.add, ranks) < n_sel) & (blk <= qb)

    oldest = jnp.where(chosen & (blk >= 1), blk, n_slc).astype(F32)
    sel_t = jnp.where(chosen, 1.0, 0.0).astype(BF16)
    sel_t = jnp.concatenate([sel_t, jnp.zeros((LANES - n_slc, lanes_q), BF16)], axis=0)
    eye = jnp.where(lax.broadcasted_iota(jnp.int32, (lanes_q, lanes_q), 0)
                    == lax.broadcasted_iota(jnp.int32, (lanes_q, lanes_q), 1), 1.0, 0.0).astype(BF16)
    sel = _dot_nt(eye, sel_t)
    lane = lax.broadcasted_iota(jnp.int32, (Q_BLOCK, LANES), 1)
    for k in range(SEL_NQ):
        lanes_k = slice(k * N_KV * Q_BLOCK, (k + 1) * N_KV * Q_BLOCK)
        first = jnp.min(jnp.min(oldest[:, lanes_k], axis=1, keepdims=True), axis=0, keepdims=True)
        first_ref[0, k] = jnp.broadcast_to(first, first_ref.shape[2:]).astype(jnp.int32)
        for g in range(N_KV):
            base = (k * N_KV + g) * Q_BLOCK
            selneg_ref[0, g, k * Q_BLOCK:(k + 1) * Q_BLOCK, :] = jnp.where(
                lane < SLC_BLOCK, (sel[base:base + Q_BLOCK] - 1.0) * BIG, 0.0).astype(BF16)


def _select(q, kvc, cbias, overlap_t, n_sel):
    bsz, s, _ = q.shape
    r = kvc.shape[1]
    n_slc = overlap_t.shape[0]
    qt = SEL_NQ * Q_BLOCK
    return pl.pallas_call(
        functools.partial(_select_kernel, n_sel=n_sel),
        grid=(bsz, s // qt),
        in_specs=[pl.BlockSpec((1, qt, Q_WIDTH), lambda b, i: (b, i, 0)),
                  pl.BlockSpec((1, r, N_KV * KV_PACK), lambda b, i: (b, 0, 0)),
                  _const_spec(cbias.shape), _const_spec(overlap_t.shape)],
        out_specs=[pl.BlockSpec((1, qt, Q_WIDTH), lambda b, i: (b, i, 0)),
                   pl.BlockSpec((1, N_KV, qt, LANES), lambda b, i: (b, 0, i, 0)),
                   pl.BlockSpec((1, SEL_NQ, 8, LANES), lambda b, i: (b, i, 0, 0))],
        out_shape=[jax.ShapeDtypeStruct((bsz, s, Q_WIDTH), BF16),
                   jax.ShapeDtypeStruct((bsz, N_KV, s, LANES), BF16),
                   jax.ShapeDtypeStruct((bsz, s // Q_BLOCK, 8, LANES), jnp.int32)],
        scratch_shapes=[pltpu.VMEM((n_slc, SEL_NQ * N_KV * Q_BLOCK), F32)],
        compiler_params=_params(2),
        name="compressed_attention_select",
    )(q, kvc, cbias, overlap_t)


def _selected_kernel(first_ref, q_ref, selneg_ref, gate_ref, ocmp_ref, owin_ref, kps_ref, tris_ref,
                     o_ref, qs_ref, s_ref, mt_ref, m_ref, lt_ref, acc_ref):
    qbs = [pl.program_id(1) * SLC_NQ + k for k in range(SLC_NQ)]
    rows = SLC_NQ * HEADS_PER_KV * Q_BLOCK
    heads = [_padded_heads(q_ref[0, k * Q_BLOCK:(k + 1) * Q_BLOCK, :]) for k in range(SLC_NQ)]
    for g in range(N_KV):
        qs = []
        for k in range(SLC_NQ):
            masked_out = selneg_ref[0, g, k * Q_BLOCK:(k + 1) * Q_BLOCK, :].astype(F32)
            qs += [jnp.concatenate([heads[k][hd], masked_out + _feature_rows(qbs[k], hd)[0]], axis=1)
                   for hd in range(g * HEADS_PER_KV, (g + 1) * HEADS_PER_KV)]
        qs_ref[g] = jnp.concatenate(qs, axis=0).astype(BF16)

    unit_blocks = SLC_UNIT // SLC_BLOCK
    diag_unit = qbs[0] // unit_blocks
    first = functools.reduce(jnp.minimum, [first_ref[pl.program_id(0), qb] for qb in qbs])
    first_unit = jnp.minimum(first, qbs[0]) // unit_blocks
    lead = jnp.minimum(first_unit, 1)
    n_units = diag_unit - first_unit + 1 + lead

    def unit_of(i):
        return jnp.where((i < lead) | (i >= n_units), 0, first_unit + i - lead)

    def unit_start(i):
        return pl.multiple_of(unit_of(i) * SLC_UNIT, SLC_UNIT)

    def slot_start(i):
        return pl.multiple_of(i * SLC_UNIT, SLC_UNIT)

    def score_unit(g, i):
        tri = []
        for qb in qbs:
            variant = jnp.where(i >= n_units, unit_blocks + 1,
                                jnp.where(unit_of(i) == diag_unit, qb % unit_blocks, unit_blocks))
            tri += [tris_ref[variant]] * HEADS_PER_KV
        tri = jnp.concatenate(tri, axis=0)
        s = (_dot_nt(qs_ref[g], kps_ref[0, g, pl.ds(unit_start(i), SLC_UNIT), :]) + tri) * LOG2E
        s_ref[g, :, pl.ds(slot_start(i), SLC_UNIT)] = s
        mt_ref[g] = jnp.maximum(jnp.maximum(mt_ref[g], s[:, 0:LANES]), s[:, LANES:SLC_UNIT])

    def value_unit(g, i):
        m = m_ref[g]
        es = [jnp.exp2(s_ref[g, :, pl.ds(slot_start(i) + j * LANES, LANES)] - m) for j in range(SLC_UNIT // LANES)]
        lt_ref[g] = lt_ref[g] + functools.reduce(jnp.add, es)
        values = kps_ref[0, g, pl.ds(unit_start(i), SLC_UNIT), 0:KV_PACK]
        acc_ref[g] = acc_ref[g] + _dot(jnp.concatenate(es, axis=1).astype(BF16), values)

    def sweep(step):
        def pair(p, carry):
            for u in range(2):
                for g in range(N_KV):
                    step(g, 2 * p + u)
            return carry

        lax.fori_loop(0, (n_units + 1) // 2, pair, 0)

    mt_ref[...] = jnp.full(mt_ref.shape, M_INIT, F32)
    sweep(score_unit)
    for g in range(N_KV):
        m_ref[g] = jnp.broadcast_to(jnp.max(mt_ref[g], axis=-1, keepdims=True), (rows, LANES))
    lt_ref[...] = jnp.zeros(lt_ref.shape, F32)
    acc_ref[...] = jnp.zeros(acc_ref.shape, F32)
    sweep(value_unit)

    lane = lax.broadcasted_iota(jnp.int32, (Q_BLOCK, LANES), 1)
    for g in range(N_KV):
        l = jnp.sum(lt_ref[g], axis=-1, keepdims=True)
        o_rows = acc_ref[g] * jnp.where(l > 0.0, 1.0 / l, 0.0)
        for k in range(SLC_NQ):
            qr = slice(k * Q_BLOCK, (k + 1) * Q_BLOCK)
            o_slc = _pack_heads(o_rows[k * HEADS_PER_KV * Q_BLOCK:(k + 1) * HEADS_PER_KV * Q_BLOCK])
            gt = gate_ref[0, g, qr, :]
            for j in range(HEADS_PER_KV // 2):
                cols = slice((g * HEADS_PER_KV // 2 + j) * LANES, (g * HEADS_PER_KV // 2 + j + 1) * LANES)
                branches = (ocmp_ref[0, qr, cols].astype(F32), o_slc[:, j * LANES:(j + 1) * LANES],
                            owin_ref[0, qr, cols].astype(F32))
                total = None
                for c, branch in enumerate(branches):
                    even, odd = 3 * (2 * j) + c, 3 * (2 * j + 1) + c
                    gate = jnp.take_along_axis(gt, jnp.where(lane < HEAD_DIM, even, odd), axis=1)
                    total = gate * branch if total is None else total + gate * branch
                o_ref[0, qr, cols] = total.astype(BF16)


def _selected(first, q, selneg, gates, o_cmp, o_win, kps, tri_slc):
    bsz, s, _ = q.shape
    qt = SLC_NQ * Q_BLOCK
    rows = HEADS_PER_KV * qt
    assert (SLC_UNIT // SLC_BLOCK) % SLC_NQ == 0 and s % qt == 0

    def row(n):
        return pl.BlockSpec((1, qt, n), lambda b, i, first_ref: (b, i, 0))

    def grouped(n):
        return pl.BlockSpec((1, N_KV, qt, n), lambda b, i, first_ref: (b, 0, i, 0))

    grid_spec = pltpu.PrefetchScalarGridSpec(
        num_scalar_prefetch=1,
        grid=(bsz, s // qt),
        in_specs=[row(Q_WIDTH), grouped(LANES), grouped(LANES), row(Q_WIDTH), row(Q_WIDTH),
                  pl.BlockSpec((1, N_KV, s, KEY_WIDTH), lambda b, i, first_ref: (b, 0, 0, 0)),
                  _const_spec(tri_slc.shape)],
        out_specs=row(Q_WIDTH),
        scratch_shapes=[pltpu.VMEM((N_KV, rows, KEY_WIDTH), BF16),
                        pltpu.VMEM((N_KV, rows, s + SLC_UNIT), F32)]
        + [pltpu.VMEM((N_KV, rows, LANES), F32)] * 4)
    return pl.pallas_call(
        _selected_kernel,
        grid_spec=grid_spec,
        out_shape=jax.ShapeDtypeStruct((bsz, s, Q_WIDTH), BF16),
        compiler_params=_params(2),
        name="selected_attention",
    )(first, q, selneg, gates, o_cmp, o_win, kps, tri_slc)


def _attention_v2(q, gates, kvc, kps, kpw):
    bsz, s, _ = q.shape
    r = kvc.shape[1]
    n_cmp = r - 1
    n_slc = s // SLC_BLOCK
    n_sel = min(N_SELECT, n_slc)
    rows = HEADS_PER_KV * Q_BLOCK
    assert n_slc <= SLC_BLOCK and n_slc % RANK_UNROLL == 0 and s % SLC_UNIT == 0 and s >= WIN_KEYS
    assert s % (WIN_NQ * Q_BLOCK) == 0 and s % (SEL_NQ * Q_BLOCK) == 0

    slopes = np.array([_alibi_slope(hd) for hd in range(N_HEADS)])
    slope_rows = np.repeat(slopes.reshape(N_KV, HEADS_PER_KV), Q_BLOCK, axis=1).reshape(N_KV, rows, 1)
    cbias = np.tile(slope_rows * (CMP_STRIDE * np.arange(r))[None, None, :], (1, SEL_NQ, 1))
    start = np.arange(r)[None, :] * CMP_STRIDE
    blk = np.arange(n_slc)[:, None] * SLC_BLOCK
    overlap_t = ((start < blk + SLC_BLOCK) & (start + CMP_BLOCK > blk) & (np.arange(r)[None, :] < n_cmp))

    ql = np.arange(Q_BLOCK)[:, None]
    kl = np.arange(SLC_BLOCK)[None, :]
    lower = np.where(kl > ql, -BIG, 0.0)
    upper = np.where(kl <= ql, -BIG, 0.0)
    unit_blocks = SLC_UNIT // SLC_BLOCK
    tri_slc = np.zeros((unit_blocks + 2, Q_BLOCK, SLC_UNIT))
    for j in range(unit_blocks):
        tri_slc[j, :, j * SLC_BLOCK:(j + 1) * SLC_BLOCK] = lower
    tri_slc[unit_blocks + 1] = -BIG
    win_blocks = WINDOW // SLC_BLOCK
    lead = WIN_KEYS // SLC_BLOCK - 1
    tri_win = np.zeros((lead + 1, Q_BLOCK, WIN_KEYS))
    for v in range(lead + 1):
        diag = v
        tri_win[v, :, diag * SLC_BLOCK:(diag + 1) * SLC_BLOCK] = lower
        if diag >= win_blocks:
            old = diag - win_blocks
            tri_win[v, :, old * SLC_BLOCK:(old + 1) * SLC_BLOCK] = upper

    o_win = _window(q, kpw, jnp.asarray(tri_win, F32))
    o_cmp, selneg, first = _select(q, kvc, jnp.asarray(cbias, F32), jnp.asarray(overlap_t, BF16), n_sel)
    return _selected(first[:, :, 0, 0], q, selneg, gates, o_cmp, o_win, kps, jnp.asarray(tri_slc, F32))


def _attn_kernel(q_ref, gate_ref, kvc_ref, kps_ref, kpw_ref, cbias_ref, ovl_ref, tris_ref, triw_ref,
                 o_ref, score_ref, qs_ref, s_ref, mt_ref, m_ref, lt_ref, acc_ref, *, n_sel):
    qb = pl.program_id(1)
    q0 = qb * Q_BLOCK
    rows = HEADS_PER_KV * Q_BLOCK
    n_slc = ovl_ref.shape[0]
    r = kvc_ref.shape[1]

    lane = lax.broadcasted_iota(jnp.int32, (Q_BLOCK, LANES), 1)
    qf = q_ref[0].astype(F32)
    q_heads = []
    for hd in range(N_HEADS):
        slab = qf[:, (hd // 2) * LANES:(hd // 2 + 1) * LANES]
        if hd % 2:
            slab = pltpu.roll(slab, HEAD_DIM, 1)
        q_heads.append(jnp.where(lane < HEAD_DIM, slab, 0.0))

    t = q0 + (lax.broadcasted_iota(jnp.int32, (rows, 1), 0) & (Q_BLOCK - 1))
    last_cmp = (t - (CMP_BLOCK - 1)) >> 4
    visible = lax.broadcasted_iota(jnp.int32, (rows, r), 1) <= last_cmp
    o_cmp, p_sums = [], []
    for g in range(N_KV):
        q_pad = jnp.concatenate(q_heads[g * HEADS_PER_KV:(g + 1) * HEADS_PER_KV], axis=0).astype(BF16)
        kvc = kvc_ref[0, :, g * KV_PACK:(g + 1) * KV_PACK]
        p_cmp = _masked_softmax(_dot_nt(q_pad, kvc) + cbias_ref[g], visible)
        o_cmp.append(_dot(p_cmp.astype(BF16), kvc))
        p_sum = p_cmp[0:Q_BLOCK]
        for h in range(1, HEADS_PER_KV):
            p_sum = p_sum + p_cmp[h * Q_BLOCK:(h + 1) * Q_BLOCK]
        p_sums.append(p_sum)

    p_all = jnp.concatenate(p_sums, axis=0)
    p_hi = p_all.astype(BF16)
    p_lo = (p_all - p_hi.astype(F32)).astype(BF16)
    ovl = ovl_ref[...]
    imp_t = _dot_nt(ovl, p_hi) + _dot_nt(ovl, p_lo)

    lane_row = lax.broadcasted_iota(jnp.int32, (1, LANES), 1)
    jrel = (lane_row - qb).astype(F32)
    in_window = (lane_row >= qb - WINDOW // SLC_BLOCK) & (lane_row <= qb)

    def feature_rows(hd):
        slope = _alibi_slope(hd)
        tail = jnp.where(lane_row == FEAT_OFFSET_LANE, slope, 0.0)
        block_bias = slope * SLC_BLOCK * jrel
        slc_row = jnp.where(lane_row < SLC_BLOCK, block_bias, tail)
        win_row = jnp.where(lane_row < SLC_BLOCK, jnp.where(in_window, block_bias, -BIG), tail)
        return slc_row, win_row

    w0 = pl.multiple_of(jnp.maximum(q0 + Q_BLOCK - WIN_KEYS, 0), Q_BLOCK)
    tri_w = jnp.concatenate([triw_ref[jnp.minimum(qb, WIN_KEYS // SLC_BLOCK - 1)]] * HEADS_PER_KV, axis=0)
    o_win = []
    for g in range(N_KV):
        qw = [jnp.concatenate([q_heads[hd], jnp.broadcast_to(feature_rows(hd)[1], (Q_BLOCK, LANES))], axis=1)
              for hd in range(g * HEADS_PER_KV, (g + 1) * HEADS_PER_KV)]
        kp = kpw_ref[0, g, pl.ds(w0, WIN_KEYS), :]
        s = _dot_nt(jnp.concatenate(qw, axis=0).astype(BF16), kp) + tri_w
        e = jnp.exp(s - jnp.max(s, axis=-1, keepdims=True))
        o_win.append(_dot(e.astype(BF16), kp[:, 0:KV_PACK]) * (1.0 / jnp.sum(e, axis=-1, keepdims=True)))

    blk = lax.broadcasted_iota(jnp.int32, (n_slc, LANES), 0)
    forced = (blk == 0) | (blk == qb) | (blk == qb - 1)
    score = jnp.where(blk > qb, NEG, jnp.where(forced, FORCE, imp_t))
    score_ref[...] = score

    def rank_step(i, ranks):
        out = []
        for u in range(RANK_UNROLL):
            row = i * RANK_UNROLL + u
            ri = score_ref[pl.ds(row, 1), :]
            beats = (ri > score) | ((ri == score) & (blk > row))
            out.append(ranks[u] + jnp.where(beats, 1.0, 0.0))
        return tuple(out)

    ranks = lax.fori_loop(0, qb // RANK_UNROLL + 1, rank_step,
                          (jnp.zeros((n_slc, LANES), F32),) * RANK_UNROLL)
    rank = functools.reduce(jnp.add, ranks)
    chosen = (rank < n_sel) & (blk <= qb)
    first = jnp.min(jnp.where(chosen & (blk >= 1), blk, n_slc).astype(F32)).astype(jnp.int32)
    first = jnp.minimum(first, qb)
    sel_t = jnp.where(chosen, 1.0, 0.0).astype(BF16)
    sel_t = jnp.concatenate([sel_t, jnp.zeros((LANES - n_slc, LANES), BF16)], axis=0)
    eye = jnp.where(lax.broadcasted_iota(jnp.int32, (LANES, LANES), 0)
                    == lax.broadcasted_iota(jnp.int32, (LANES, LANES), 1), 1.0, 0.0).astype(BF16)
    sel = _dot_nt(eye, sel_t)

    for g in range(N_KV):
        masked_out = jnp.where(lane < SLC_BLOCK, (sel[g * Q_BLOCK:(g + 1) * Q_BLOCK] - 1.0) * BIG, 0.0)
        qs = [jnp.concatenate([q_heads[hd], masked_out + feature_rows(hd)[0]], axis=1)
              for hd in range(g * HEADS_PER_KV, (g + 1) * HEADS_PER_KV)]
        qs_ref[g] = jnp.concatenate(qs, axis=0).astype(BF16)

    unit_blocks = SLC_UNIT // SLC_BLOCK
    diag_unit = qb // unit_blocks
    first_unit = first // unit_blocks
    lead = jnp.minimum(first_unit, 1)
    n_units = diag_unit - first_unit + 1 + lead

    def unit_of(i):
        return jnp.where((i < lead) | (i >= n_units), 0, first_unit + i - lead)

    def unit_start(i):
        return pl.multiple_of(unit_of(i) * SLC_UNIT, SLC_UNIT)

    def slot_start(i):
        return pl.multiple_of(i * SLC_UNIT, SLC_UNIT)

    def score_unit(g, i):
        tri = []
        for qb in qbs:
            variant = jnp.where(i >= n_units, unit_blocks + 1,
                                jnp.where(unit_of(i) == diag_unit, qb % unit_blocks, unit_blocks))
            tri += [tris_ref[variant]] * HEADS_PER_KV
        tri = jnp.concatenate(tri, axis=0)
        s = (_dot_nt(qs_ref[g], kps_ref[0, g, pl.ds(unit_start(i), SLC_UNIT), :]) + tri) * LOG2E
        s_ref[g, :, pl.ds(slot_start(i), SLC_UNIT)] = s
        mt_ref[g] = jnp.maximum(jnp.maximum(mt_ref[g], s[:, 0:LANES]), s[:, LANES:SLC_UNIT])

    def value_unit(g, i):
        m = m_ref[g]
        es = [jnp.exp2(s_ref[g, :, pl.ds(slot_start(i) + j * LANES, LANES)] - m) for j in range(SLC_UNIT // LANES)]
        lt_ref[g] = lt_ref[g] + functools.reduce(jnp.add, es)
        values = kps_ref[0, g, pl.ds(unit_start(i), SLC_UNIT), 0:KV_PACK]
        acc_ref[g] = acc_ref[g] + _dot(jnp.concatenate(es, axis=1).astype(BF16), values)

    def sweep(step):
        def pair(p, carry):
            for u in range(2):
                for g in range(N_KV):
                    step(g, 2 * p + u)
            return carry

        lax.fori_loop(0, (n_units + 1) // 2, pair, 0)

    mt_ref[...] = jnp.full(mt_ref.shape, M_INIT, F32)
    sweep(score_unit)
    for g in range(N_KV):
        m_ref[g] = jnp.broadcast_to(jnp.max(mt_ref[g], axis=-1, keepdims=True), (rows, LANES))
    lt_ref[...] = jnp.zeros(lt_ref.shape, F32)
    acc_ref[...] = jnp.zeros(acc_ref.shape, F32)
    sweep(value_unit)

    outs = []
    for g in range(N_KV):
        l = jnp.sum(lt_ref[g], axis=-1, keepdims=True)
        o_slc = acc_ref[g] * jnp.where(l > 0.0, 1.0 / l, 0.0)
        gt = gate_ref[0, g]
        heads = []
        for h in range(HEADS_PER_KV):
            rs = slice(h * Q_BLOCK, (h + 1) * Q_BLOCK)
            heads.append(gt[:, 3 * h:3 * h + 1] * o_cmp[g][rs] + gt[:, 3 * h + 1:3 * h + 2] * o_slc[rs]
                         + gt[:, 3 * h + 2:3 * h + 3] * o_win[g][rs])
        outs += [jnp.where(lane < HEAD_DIM, pltpu.roll(heads[2 * j], HEAD_DIM, 1), heads[2 * j + 1])
                 for j in range(HEADS_PER_KV // 2)]
    o_ref[0] = jnp.concatenate(outs, axis=1).astype(BF16)


def _attention(q, gates, kvc, kps, kpw):
    bsz, s, _ = q.shape
    r = kvc.shape[1]
    n_cmp = r - 1
    n_slc = s // SLC_BLOCK
    n_sel = min(N_SELECT, n_slc)
    rows = HEADS_PER_KV * Q_BLOCK
    assert n_slc <= SLC_BLOCK and n_slc % RANK_UNROLL == 0 and s % SLC_UNIT == 0 and s >= WIN_KEYS

    slopes = np.array([_alibi_slope(hd) for hd in range(N_HEADS)])
    slope_rows = np.repeat(slopes.reshape(N_KV, HEADS_PER_KV), Q_BLOCK, axis=1).reshape(N_KV, rows, 1)
    cbias = slope_rows * (CMP_STRIDE * np.arange(r))[None, None, :]
    start = np.arange(r)[None, :] * CMP_STRIDE
    blk = np.arange(n_slc)[:, None] * SLC_BLOCK
    overlap_t = ((start < blk + SLC_BLOCK) & (start + CMP_BLOCK > blk) & (np.arange(r)[None, :] < n_cmp))

    ql = np.arange(Q_BLOCK)[:, None]
    kl = np.arange(SLC_BLOCK)[None, :]
    lower = np.where(kl > ql, -BIG, 0.0)
    upper = np.where(kl <= ql, -BIG, 0.0)
    unit_blocks = SLC_UNIT // SLC_BLOCK
    tri_slc = np.zeros((unit_blocks + 2, Q_BLOCK, SLC_UNIT))
    for j in range(unit_blocks):
        tri_slc[j, :, j * SLC_BLOCK:(j + 1) * SLC_BLOCK] = lower
    tri_slc[unit_blocks + 1] = -BIG
    win_blocks = WINDOW // SLC_BLOCK
    lead = WIN_KEYS // SLC_BLOCK - 1
    tri_win = np.zeros((lead + 1, Q_BLOCK, WIN_KEYS))
    for v in range(lead + 1):
        diag = v
        tri_win[v, :, diag * SLC_BLOCK:(diag + 1) * SLC_BLOCK] = lower
        if diag >= win_blocks:
            old = diag - win_blocks
            tri_win[v, :, old * SLC_BLOCK:(old + 1) * SLC_BLOCK] = upper

    return pl.pallas_call(
        functools.partial(_attn_kernel, n_sel=n_sel),
        grid=(bsz, s // Q_BLOCK),
        in_specs=[pl.BlockSpec((1, Q_BLOCK, Q_WIDTH), lambda b, i: (b, i, 0)),
                  pl.BlockSpec((1, N_KV, Q_BLOCK, LANES), lambda b, i: (b, 0, i, 0)),
                  pl.BlockSpec((1, r, N_KV * KV_PACK), lambda b, i: (b, 0, 0)),
                  pl.BlockSpec((1, N_KV, s, KEY_WIDTH), lambda b, i: (b, 0, 0, 0)),
                  pl.BlockSpec((1, N_KV, s, KEY_WIDTH), lambda b, i: (b, 0, 0, 0)),
                  _const_spec(cbias.shape), _const_spec(overlap_t.shape),
                  _const_spec(tri_slc.shape), _const_spec(tri_win.shape)],
        out_specs=pl.BlockSpec((1, Q_BLOCK, Q_WIDTH), lambda b, i: (b, i, 0)),
        out_shape=jax.ShapeDtypeStruct((bsz, s, Q_WIDTH), BF16),
        scratch_shapes=[pltpu.VMEM((n_slc, LANES), F32),
                        pltpu.VMEM((N_KV, rows, KEY_WIDTH), BF16),
                        pltpu.VMEM((N_KV, rows, s + SLC_UNIT), F32)]
        + [pltpu.VMEM((N_KV, rows, LANES), F32)] * 4,
        compiler_params=_params(2),
        name="sparse_attention",
    )(q, gates, kvc, kps, kpw, jnp.asarray(cbias, F32), jnp.asarray(overlap_t, BF16),
      jnp.asarray(tri_slc, F32), jnp.asarray(tri_win, F32))


def _mixout_kernel(x_ref, gate_ref, uprev_ref, u_ref, o_ref, gbr_ref, pw_ref, ps_ref,
                   wa_ref, wb_ref, wo_ref, g_ref, b_ref, out_ref, ubuf_ref):
    i = pl.program_id(1)
    tm = u_ref.shape[1]
    ubuf_ref[0:POOL_HALO, :] = jnp.where(i == 0, 0.0, uprev_ref[0])
    ubuf_ref[POOL_HALO:POOL_HALO + tm, :] = u_ref[0]
    t = i * tm + lax.broadcasted_iota(jnp.int32, (tm, 1), 0)

    mixed = []
    for gi, w in enumerate(POOL_WINDOWS):
        cs = slice(gi * POOL_GROUP, (gi + 1) * POOL_GROUP)
        cur = ubuf_ref[POOL_HALO:POOL_HALO + tm, cs]
        total = cur
        for k in range(1, w):
            total = total + ubuf_ref[POOL_HALO - k:POOL_HALO - k + tm, cs]
        inv_cnt = 1.0 / jnp.minimum(t + 1, w).astype(F32)
        delta = (total * inv_cnt - cur).astype(BF16)
        mixed.append((_dot(delta, pw_ref[gi]) * ps_ref[:, cs]).astype(BF16))
    y_a = _dot(jnp.concatenate(mixed, axis=1), wa_ref[...])
    y_b = _dot(o_ref[0], wb_ref[...])
    d = y_a.shape[1]
    y = (gbr_ref[0, :, 0:d].astype(F32) * y_a + gbr_ref[0, :, d:2 * d].astype(F32) * y_b).astype(BF16)
    y = _dot(y, wo_ref[...])
    out_ref[0] = _layer_norm(ALPHA * x_ref[0] + gate_ref[0] * y, g_ref[...], b_ref[...])


def _mixout(x, gate, u, o, gbr, pool_w, pool_scale, w_a, w_b, w_o, g, b):
    bsz, s, d = x.shape
    tm = min(ROW_TILE, s)
    halo_blocks = tm // POOL_HALO

    def row(n):
        return pl.BlockSpec((1, tm, n), lambda bi, i: (bi, i, 0))

    return pl.pallas_call(
        _mixout_kernel,
        grid=(bsz, s // tm),
        in_specs=[row(d), pl.BlockSpec((1, 1, d), lambda bi, i: (bi, 0, 0)),
                  pl.BlockSpec((1, POOL_HALO, POOL_WIDTH),
                               lambda bi, i: (bi, jnp.maximum(i * halo_blocks - 1, 0), 0)),
                  row(POOL_WIDTH), row(Q_WIDTH), row(2 * d),
                  _const_spec(pool_w.shape), _const_spec((1, POOL_WIDTH)),
                  _const_spec(w_a.shape), _const_spec(w_b.shape), _const_spec(w_o.shape),
                  _const_spec((1, d)), _const_spec((1, d))],
        out_specs=row(d),
        out_shape=jax.ShapeDtypeStruct((bsz, s, d), F32),
        scratch_shapes=[pltpu.VMEM((POOL_HALO + tm, POOL_WIDTH), F32)],
        compiler_params=_params(2),
        name="pool_merge_out",
    )(x, gate, u, u, o, gbr, pool_w, pool_scale.reshape(1, POOL_WIDTH), w_a, w_b, w_o,
      g.reshape(1, d), b.reshape(1, d))


def _mixer_in_weights(w):
    sizes = (POOL_WIDTH, Q_WIDTH) + (KV_WIDTH,) * 6 + (3 * N_HEADS, 2 * D_MODEL)
    offs = np.concatenate([[0], np.cumsum(sizes)])
    u, q, k_cmp, v_cmp, k_slc, v_slc, k_win, v_win, g_nsa, g_br = [w[:, offs[i]:offs[i + 1]] for i in range(10)]
    cols = [u, q, k_cmp, v_cmp]
    for k, v in ((k_slc, v_slc), (k_win, v_win)):
        for g in range(N_KV):
            cols += [k[:, g * HEAD_DIM:(g + 1) * HEAD_DIM], v[:, g * HEAD_DIM:(g + 1) * HEAD_DIM]]
    per_g = 3 * HEADS_PER_KV
    for g in range(N_KV):
        cols += [g_nsa[:, g * per_g:(g + 1) * per_g], jnp.zeros((w.shape[0], LANES - per_g), w.dtype)]
    cols.append(g_br)
    return jnp.concatenate(cols, axis=1).astype(BF16)


def _compress_weights(pos, w1, w2, value_slot):
    def expand_w1(half):
        wh = half.reshape(CMP_STRIDE, HEAD_DIM, CMP_HIDDEN)
        z = jnp.zeros((CMP_STRIDE, N_KV, HEAD_DIM, N_KV, CMP_HIDDEN), w1.dtype)
        for g in range(N_KV):
            z = z.at[:, g, :, g, :].set(wh)
        return z.reshape(CMP_STRIDE * KV_WIDTH, N_KV * CMP_HIDDEN)

    half_rows = CMP_STRIDE * HEAD_DIM
    w1_big = jnp.stack([expand_w1(w1[:half_rows]), expand_w1(w1[half_rows:])]).astype(BF16)
    w2_big = jnp.zeros((N_KV, CMP_HIDDEN, N_KV, 2, HEAD_DIM), w2.dtype)
    for g in range(N_KV):
        w2_big = w2_big.at[g, :, g, value_slot, :].set(w2)
    w2_big = w2_big.reshape(N_KV * CMP_HIDDEN, N_KV * KV_PACK).astype(BF16)
    pos_rows = jnp.broadcast_to(pos.reshape(2, CMP_STRIDE, 1, HEAD_DIM), (2, CMP_STRIDE, N_KV, HEAD_DIM))
    return pos_rows.reshape(2, CMP_STRIDE * KV_WIDTH), w1_big, w2_big


def kernel(x, c, ln_in_g, ln_in_b, w_ada, b_ada, ffn1_w_in, ffn1_w_out, ln1_g, ln1_b, w_mix_in, pool_w, pool_scale,
           cmp_pos_k, cmp_k_w1, cmp_k_w2, cmp_pos_v, cmp_v_w1, cmp_v_w2, w_branch_a, w_branch_b, w_mix_out,
           ln2_g, ln2_b, ffn2_w_in, ffn2_w_out, ln3_g, ln3_b):
    bsz, s, d = x.shape
    for l in range(DEPTH):
        ada = _ada(c, w_ada[l], b_ada[l]).reshape(bsz, 3, 3, 1, d)
        mod = lambda i, j: ada[:, i, j]

        pre_ln = l == 0
        x = _ffn(x, mod(0, 0), mod(0, 1), mod(0, 2), ln_in_g, ln_in_b, ln1_g[l], ln1_b[l],
                 ffn1_w_in[l].astype(BF16), ffn1_w_out[l].astype(BF16), pre_ln=pre_ln)

        u, q, kc, vc, kvs, kvw, gates, gbr = _mixin(x, mod(1, 0), mod(1, 1), _mixer_in_weights(w_mix_in[l]))
        posk, w1k, w2k = _compress_weights(cmp_pos_k[l], cmp_k_w1[l], cmp_k_w2[l], 0)
        posv, w1v, w2v = _compress_weights(cmp_pos_v[l], cmp_v_w1[l], cmp_v_w2[l], 1)
        rows = s // CMP_STRIDE
        kvc = _compress(kc.reshape(bsz, rows, CMP_STRIDE * KV_WIDTH), vc.reshape(bsz, rows, CMP_STRIDE * KV_WIDTH),
                        posk, posv, w1k, w1v, w2k, w2v)
        o = _attention_v2(q, gates, kvc, kvs, kvw)
        x = _mixout(x, mod(1, 2), u, o, gbr, pool_w[l].astype(BF16), pool_scale[l],
                    w_branch_a[l].astype(BF16), w_branch_b[l].astype(BF16), w_mix_out[l].astype(BF16),
                    ln2_g[l], ln2_b[l])

        x = _ffn(x, mod(2, 0), mod(2, 1), mod(2, 2), ln_in_g, ln_in_b, ln3_g[l], ln3_b[l],
                 ffn2_w_in[l].astype(BF16), ffn2_w_out[l].astype(BF16), pre_ln=False)
    return x
```

```python
import functools

import numpy as np
import jax
import jax.numpy as jnp
from jax import lax
from jax.experimental import pallas as pl
from jax.experimental.pallas import tpu as pltpu

F32 = jnp.float32
BF16 = jnp.bfloat16

D_MODEL = 1024
POOL_WIDTH = D_MODEL // 2
POOL_WINDOWS = (2, 4, 8, 16)
POOL_GROUP = POOL_WIDTH // len(POOL_WINDOWS)
POOL_HALO = 16
HEAD_DIM = 64
N_HEADS = (D_MODEL // 2) // HEAD_DIM
N_KV = 2
HEADS_PER_KV = N_HEADS // N_KV
Q_WIDTH = N_HEADS * HEAD_DIM
KV_WIDTH = N_KV * HEAD_DIM
CMP_STRIDE = 16
CMP_BLOCK = 2 * CMP_STRIDE
CMP_HIDDEN = 4 * HEAD_DIM
SLC_BLOCK = 64
N_SELECT = 16
WINDOW = 512
Q_BLOCK = SLC_BLOCK
ALIBI_MAX = 8.0
D_FF = 2816
DEPTH = 1
ALPHA = (2.0 * DEPTH) ** 0.25
LN_EPS = 1e-5
NEG = -1e30
FORCE = 1e9

LANES = 128
KV_PACK = 2 * HEAD_DIM
KEY_WIDTH = 2 * KV_PACK
FEAT_OFFSET_LANE = SLC_BLOCK
BIG = 1e30
M_INIT = -3e38
FF_CHUNK = 256
ROW_TILE = 512
SLC_UNIT = 256
RANK_UNROLL = 2
WIN_NQ = 4
SEL_NQ = 2
SLC_NQ = 2
LOG2E = 1.4426950408889634
WIN_KEYS = WINDOW + 2 * Q_BLOCK
VMEM_LIMIT = 56 * 1024 * 1024

_C_U = 0
_C_Q = _C_U + POOL_WIDTH
_C_KC = _C_Q + Q_WIDTH
_C_VC = _C_KC + KV_WIDTH
_C_KVS = _C_VC + KV_WIDTH
_C_KVW = _C_KVS + N_KV * KV_PACK
_C_GN = _C_KVW + N_KV * KV_PACK
_C_GBR = _C_GN + N_KV * LANES
_C_END = _C_GBR + 2 * D_MODEL


def _dot(a, b):
    return jnp.dot(a, b, preferred_element_type=F32)


def _dot_nt(a, b):
    return lax.dot_general(a, b, (((1,), (1,)), ((), ())), preferred_element_type=F32)


def _layer_norm(x, g, b):
    mu = jnp.mean(x, axis=-1, keepdims=True)
    xc = x - mu
    var = jnp.mean(xc * xc, axis=-1, keepdims=True)
    return xc * lax.rsqrt(var + LN_EPS) * g + b


def _masked_softmax(s, mask):
    sm = jnp.where(mask, s, NEG)
    m = jnp.max(sm, axis=-1, keepdims=True)
    e = jnp.exp(sm - m)
    p = e * (1.0 / jnp.sum(e, axis=-1, keepdims=True))
    return jnp.where(mask, p, 0.0)


def _const_spec(shape):
    nd = len(shape)
    return pl.BlockSpec(shape, lambda *_: (0,) * nd, pipeline_mode=pl.Buffered(1))


def _params(n_grid):
    return pltpu.CompilerParams(dimension_semantics=("parallel",) * n_grid, vmem_limit_bytes=VMEM_LIMIT)


def _ada_kernel(c_ref, w_ref, b_ref, o_ref):
    c = c_ref[...]
    c_act = (c * jax.nn.sigmoid(c)).astype(BF16)
    o_ref[...] = _dot(c_act, w_ref[...].astype(BF16)) + b_ref[...]


def _ada(c, w, b):
    bsz, d = c.shape
    n = w.shape[1]
    tn = D_MODEL
    return pl.pallas_call(
        _ada_kernel,
        grid=(n // tn,),
        in_specs=[pl.BlockSpec((bsz, d), lambda j: (0, 0)),
                  pl.BlockSpec((d, tn), lambda j: (0, j)),
                  pl.BlockSpec((1, tn), lambda j: (0, j))],
        out_specs=pl.BlockSpec((bsz, tn), lambda j: (0, j)),
        out_shape=jax.ShapeDtypeStruct((bsz, n), F32),
        compiler_params=_params(1),
        name="ada_proj",
    )(c, w, b.reshape(1, n))


def _swiglu_block(x, shift, scale, gate, w_in_ref, w_out_ref, g, b):
    h = (x * (1.0 + scale) + shift).astype(BF16)
    acc = jnp.zeros(x.shape, F32)
    for j in range(D_FF // FF_CHUNK):
        c0 = j * FF_CHUNK
        gt = _dot(h, w_in_ref[:, c0:c0 + FF_CHUNK])
        up = _dot(h, w_in_ref[:, D_FF + c0:D_FF + c0 + FF_CHUNK])
        act = (gt * jax.nn.sigmoid(gt) * up).astype(BF16)
        acc = acc + _dot(act, w_out_ref[c0:c0 + FF_CHUNK, :])
    return _layer_norm(ALPHA * x + 0.5 * gate * acc, g, b)


def _ffn_mixin_kernel(x_ref, shift0_ref, scale0_ref, gate0_ref, g0_ref, b0_ref, g1_ref, b1_ref, w_in_ref, w_out_ref,
                      shift1_ref, scale1_ref, w_ref,
                      x1_ref, u_ref, q_ref, kc_ref, vc_ref, kvs_ref, kvw_ref, gn_ref, gbr_ref, *, pre_ln):
    x = x_ref[0]
    if pre_ln:
        x = _layer_norm(x, g0_ref[...], b0_ref[...])
    x1 = _swiglu_block(x, shift0_ref[0], scale0_ref[0], gate0_ref[0], w_in_ref, w_out_ref, g1_ref[...], b1_ref[...])
    x1_ref[0] = x1

    h = (x1 * (1.0 + scale1_ref[0]) + shift1_ref[0]).astype(BF16)

    def proj(c0, c1):
        return _dot(h, w_ref[:, c0:c1])

    u_ref[0] = proj(_C_U, _C_Q)
    q_ref[0] = (proj(_C_Q, _C_KC) * HEAD_DIM ** -0.5).astype(BF16)
    kc_ref[0] = proj(_C_KC, _C_VC)
    vc_ref[0] = proj(_C_VC, _C_KVS)
    tm = x_ref.shape[1]
    pos = pl.program_id(1) * tm + lax.broadcasted_iota(jnp.int32, (tm, LANES), 0)
    lane = lax.broadcasted_iota(jnp.int32, (tm, LANES), 1)
    feat = jnp.where(lane == pos // SLC_BLOCK, 1.0,
                     jnp.where(lane == FEAT_OFFSET_LANE, (pos % SLC_BLOCK).astype(F32), 0.0)).astype(BF16)
    for ref, c0 in ((kvs_ref, _C_KVS), (kvw_ref, _C_KVW)):
        z = proj(c0, c0 + N_KV * KV_PACK).astype(BF16)
        for g in range(N_KV):
            ref[0, g, :, 0:KV_PACK] = z[:, g * KV_PACK:(g + 1) * KV_PACK]
            ref[0, g, :, KV_PACK:KEY_WIDTH] = feat
    z = jax.nn.sigmoid(proj(_C_GN, _C_GBR))
    for g in range(N_KV):
        gn_ref[0, g] = z[:, g * LANES:(g + 1) * LANES]
    gbr_ref[0] = jax.nn.sigmoid(proj(_C_GBR, _C_END)).astype(BF16)


def _ffn_mixin(x, shift0, scale0, gate0, g0, b0, g1, b1, w_in, w_out, shift1, scale1, w_mix, *, pre_ln):
    bsz, s, d = x.shape
    tm = min(ROW_TILE, s)

    def row(n, dt):
        return pl.BlockSpec((1, tm, n), lambda b, i: (b, i, 0)), jax.ShapeDtypeStruct((bsz, s, n), dt)

    def grouped(n, dt):
        return (pl.BlockSpec((1, N_KV, tm, n), lambda b, i: (b, 0, i, 0)),
                jax.ShapeDtypeStruct((bsz, N_KV, s, n), dt))

    outs = [row(d, F32), row(POOL_WIDTH, F32), row(Q_WIDTH, BF16), row(KV_WIDTH, F32), row(KV_WIDTH, F32),
            grouped(KEY_WIDTH, BF16), grouped(KEY_WIDTH, BF16), grouped(LANES, F32), row(2 * D_MODEL, BF16)]
    mod = pl.BlockSpec((1, 1, d), lambda b, i: (b, 0, 0))
    vec = _const_spec((1, d))
    return pl.pallas_call(
        functools.partial(_ffn_mixin_kernel, pre_ln=pre_ln),
        grid=(bsz, s // tm),
        in_specs=[row(d, F32)[0], mod, mod, mod, vec, vec, vec, vec,
                  _const_spec(w_in.shape), _const_spec(w_out.shape), mod, mod, _const_spec(w_mix.shape)],
        out_specs=[o[0] for o in outs],
        out_shape=[o[1] for o in outs],
        compiler_params=_params(2),
        name="ffn_mixer_in",
    )(x, shift0, scale0, gate0, g0.reshape(1, d), b0.reshape(1, d), g1.reshape(1, d), b1.reshape(1, d),
      w_in, w_out, shift1, scale1, w_mix)


def _compress_kernel(k_ref, v_ref, posk_ref, posv_ref, w1k_ref, w1v_ref, w2k_ref, w2v_ref, o_ref, nxt_ref):
    r = k_ref.shape[1] // CMP_STRIDE

    def hidden(x_ref, pos_ref, w1_ref):
        x = jnp.concatenate([x_ref[0, pl.ds(j, r, stride=CMP_STRIDE), :] for j in range(CMP_STRIDE)], axis=1)
        first = _dot((x + pos_ref[0:1, :]).astype(BF16), w1_ref[0])
        nxt_ref[0:r, :] = _dot((x + pos_ref[1:2, :]).astype(BF16), w1_ref[1])
        nxt_ref[r:r + 8, :] = jnp.zeros((8, nxt_ref.shape[1]), F32)
        return jax.nn.gelu(first + nxt_ref[1:r + 1, :]).astype(BF16)

    hk = hidden(k_ref, posk_ref, w1k_ref)
    hv = hidden(v_ref, posv_ref, w1v_ref)
    o_ref[0] = (_dot(hk, w2k_ref[...]) + _dot(hv, w2v_ref[...])).astype(BF16)


def _compress(kc, vc, posk, posv, w1k, w1v, w2k, w2v):
    bsz, s, n = kc.shape
    r = s // CMP_STRIDE
    hid = w1k.shape[2]
    row = pl.BlockSpec((1, s, n), lambda b: (b, 0, 0))
    return pl.pallas_call(
        _compress_kernel,
        grid=(bsz,),
        in_specs=[row, row, _const_spec(posk.shape), _const_spec(posv.shape),
                  _const_spec(w1k.shape), _const_spec(w1v.shape),
                  _const_spec(w2k.shape), _const_spec(w2v.shape)],
        out_specs=pl.BlockSpec((1, r, N_KV * KV_PACK), lambda b: (b, 0, 0)),
        out_shape=jax.ShapeDtypeStruct((bsz, r, N_KV * KV_PACK), BF16),
        scratch_shapes=[pltpu.VMEM((r + 8, hid), F32)],
        compiler_params=_params(1),
        name="compress_mlp",
    )(kc, vc, posk, posv, w1k, w1v, w2k, w2v)


def _alibi_slope(head):
    return 2.0 ** (-ALIBI_MAX * (head + 1) / N_HEADS)


def _padded_heads(q_tile):
    lane = lax.broadcasted_iota(jnp.int32, (Q_BLOCK, LANES), 1)
    qf = q_tile.astype(F32)
    heads = []
    for hd in range(N_HEADS):
        slab = qf[:, (hd // 2) * LANES:(hd // 2 + 1) * LANES]
        if hd % 2:
            slab = pltpu.roll(slab, HEAD_DIM, 1)
        heads.append(jnp.where(lane < HEAD_DIM, slab, 0.0))
    return heads


def _pack_heads(o_rows):
    lane = lax.broadcasted_iota(jnp.int32, (Q_BLOCK, LANES), 1)
    pairs = []
    for j in range(HEADS_PER_KV // 2):
        even = o_rows[(2 * j) * Q_BLOCK:(2 * j + 1) * Q_BLOCK]
        odd = o_rows[(2 * j + 1) * Q_BLOCK:(2 * j + 2) * Q_BLOCK]
        pairs.append(jnp.where(lane < HEAD_DIM, pltpu.roll(even, HEAD_DIM, 1), odd))
    return jnp.concatenate(pairs, axis=1)


def _feature_rows(qb, hd):
    lane_row = lax.broadcasted_iota(jnp.int32, (1, LANES), 1)
    slope = _alibi_slope(hd)
    tail = jnp.where(lane_row == FEAT_OFFSET_LANE, slope, 0.0)
    block_bias = slope * SLC_BLOCK * (lane_row - qb).astype(F32)
    in_window = (lane_row >= qb - WINDOW // SLC_BLOCK) & (lane_row <= qb)
    slc_row = jnp.where(lane_row < SLC_BLOCK, block_bias, tail)
    win_row = jnp.where(lane_row < SLC_BLOCK, jnp.where(in_window, block_bias, -BIG), tail)
    return slc_row, win_row


def _window_kernel(q_ref, kpw_ref, triw_ref, o_ref):
    for k in range(WIN_NQ):
        qb = pl.program_id(1) * WIN_NQ + k
        heads = _padded_heads(q_ref[0, k * Q_BLOCK:(k + 1) * Q_BLOCK, :])
        w0 = pl.multiple_of(jnp.maximum((qb + 1) * Q_BLOCK - WIN_KEYS, 0), Q_BLOCK)
        tri = jnp.concatenate([triw_ref[jnp.minimum(qb, WIN_KEYS // SLC_BLOCK - 1)]] * HEADS_PER_KV, axis=0)
        for g in range(N_KV):
            qw = [jnp.concatenate([heads[hd], jnp.broadcast_to(_feature_rows(qb, hd)[1], (Q_BLOCK, LANES))], axis=1)
                  for hd in range(g * HEADS_PER_KV, (g + 1) * HEADS_PER_KV)]
            kp = kpw_ref[0, g, pl.ds(w0, WIN_KEYS), :]
            s = _dot_nt(jnp.concatenate(qw, axis=0).astype(BF16), kp) + tri
            e = jnp.exp(s - jnp.max(s, axis=-1, keepdims=True))
            o = _dot(e.astype(BF16), kp[:, 0:KV_PACK]) * (1.0 / jnp.sum(e, axis=-1, keepdims=True))
            gw = HEADS_PER_KV * HEAD_DIM
            o_ref[0, k * Q_BLOCK:(k + 1) * Q_BLOCK, g * gw:(g + 1) * gw] = _pack_heads(o).astype(BF16)


def _window(q, kpw, tri_win):
    bsz, s, _ = q.shape
    qt = WIN_NQ * Q_BLOCK
    return pl.pallas_call(
        _window_kernel,
        grid=(bsz, s // qt),
        in_specs=[pl.BlockSpec((1, qt, Q_WIDTH), lambda b, i: (b, i, 0)),
                  pl.BlockSpec((1, N_KV, s, KEY_WIDTH), lambda b, i: (b, 0, 0, 0)),
                  _const_spec(tri_win.shape)],
        out_specs=pl.BlockSpec((1, qt, Q_WIDTH), lambda b, i: (b, i, 0)),
        out_shape=jax.ShapeDtypeStruct((bsz, s, Q_WIDTH), BF16),
        compiler_params=_params(2),
        name="window_attention",
    )(q, kpw, tri_win)


def _select_kernel(q_ref, kvc_ref, cbias_ref, ovl_ref, ocmp_ref, selneg_ref, first_ref, score_ref, *, n_sel):
    step = pl.program_id(1)
    rows = SEL_NQ * HEADS_PER_KV * Q_BLOCK
    n_slc = ovl_ref.shape[0]
    r = kvc_ref.shape[1]
    lanes_q = SEL_NQ * N_KV * Q_BLOCK
    heads = [_padded_heads(q_ref[0, k * Q_BLOCK:(k + 1) * Q_BLOCK, :]) for k in range(SEL_NQ)]

    row = lax.broadcasted_iota(jnp.int32, (rows, 1), 0)
    t = (step * SEL_NQ + row // (HEADS_PER_KV * Q_BLOCK)) * Q_BLOCK + (row & (Q_BLOCK - 1))
    last_cmp = (t - (CMP_BLOCK - 1)) >> 4
    visible = lax.broadcasted_iota(jnp.int32, (rows, r), 1) <= last_cmp
    p_sums = [[None] * N_KV for _ in range(SEL_NQ)]
    gw = HEADS_PER_KV * HEAD_DIM
    for g in range(N_KV):
        q_pad = jnp.concatenate([heads[k][g * HEADS_PER_KV + h] for k in range(SEL_NQ)
                                 for h in range(HEADS_PER_KV)], axis=0).astype(BF16)
        kvc = kvc_ref[0, :, g * KV_PACK:(g + 1) * KV_PACK]
        p_cmp = _masked_softmax(_dot_nt(q_pad, kvc) + cbias_ref[g], visible)
        o_cmp = _dot(p_cmp.astype(BF16), kvc)
        for k in range(SEL_NQ):
            base = k * HEADS_PER_KV * Q_BLOCK
            ocmp_ref[0, k * Q_BLOCK:(k + 1) * Q_BLOCK, g * gw:(g + 1) * gw] = _pack_heads(
                o_cmp[base:base + HEADS_PER_KV * Q_BLOCK]).astype(BF16)
            p_sum = p_cmp[base:base + Q_BLOCK]
            for h in range(1, HEADS_PER_KV):
                p_sum = p_sum + p_cmp[base + h * Q_BLOCK:base + (h + 1) * Q_BLOCK]
            p_sums[k][g] = p_sum

    p_all = jnp.concatenate([p_sums[k][g] for k in range(SEL_NQ) for g in range(N_KV)], axis=0)
    p_hi = p_all.astype(BF16)
    p_lo = (p_all - p_hi.astype(F32)).astype(BF16)
    ovl = ovl_ref[...]
    imp_t = _dot_nt(ovl, p_hi) + _dot_nt(ovl, p_lo)

    blk = lax.broadcasted_iota(jnp.int32, (n_slc, lanes_q), 0)
    qb = step * SEL_NQ + lax.broadcasted_iota(jnp.int32, (n_slc, lanes_q), 1) // (N_KV * Q_BLOCK)
    forced = (blk == 0) | (blk == qb) | (blk == qb - 1)
    score = jnp.where(blk > qb, NEG, jnp.where(forced, FORCE, imp_t))
    score_ref[...] = score

    def rank_step(i, ranks):
        out = []
        for u in range(RANK_UNROLL):
            ri = score_ref[pl.ds(i * RANK_UNROLL + u, 1), :]
            beats = (ri > score) | ((ri == score) & (blk > i * RANK_UNROLL + u))
            out.append(ranks[u] + jnp.where(beats, 1.0, 0.0))
        return tuple(out)

    ranks = lax.fori_loop(0, (step * SEL_NQ + SEL_NQ - 1) // RANK_UNROLL + 1, rank_step,
                          (jnp.zeros((n_slc, lanes_q), F32),) * RANK_UNROLL)
    chosen = (functools.reduce(jnp.add, ranks) < n_sel) & (blk <= qb)

    oldest = jnp.where(chosen & (blk >= 1), blk, n_slc).astype(F32)
    sel_t = jnp.where(chosen, 1.0, 0.0).astype(BF16)
    sel_t = jnp.concatenate([sel_t, jnp.zeros((LANES - n_slc, lanes_q), BF16)], axis=0)
    eye = jnp.where(lax.broadcasted_iota(jnp.int32, (lanes_q, lanes_q), 0)
                    == lax.broadcasted_iota(jnp.int32, (lanes_q, lanes_q), 1), 1.0, 0.0).astype(BF16)
    sel = _dot_nt(eye, sel_t)
    lane = lax.broadcasted_iota(jnp.int32, (Q_BLOCK, LANES), 1)
    for k in range(SEL_NQ):
        lanes_k = slice(k * N_KV * Q_BLOCK, (k + 1) * N_KV * Q_BLOCK)
        first = jnp.min(jnp.min(oldest[:, lanes_k], axis=1, keepdims=True), axis=0, keepdims=True)
        first_ref[0, k] = jnp.broadcast_to(first, first_ref.shape[2:]).astype(jnp.int32)
        for g in range(N_KV):
            base = (k * N_KV + g) * Q_BLOCK
            selneg_ref[0, g, k * Q_BLOCK:(k + 1) * Q_BLOCK, :] = jnp.where(
                lane < SLC_BLOCK, (sel[base:base + Q_BLOCK] - 1.0) * BIG, 0.0).astype(BF16)


def _select(q, kvc, cbias, overlap_t, n_sel):
    bsz, s, _ = q.shape
    r = kvc.shape[1]
    n_slc = overlap_t.shape[0]
    qt = SEL_NQ * Q_BLOCK
    return pl.pallas_call(
        functools.partial(_select_kernel, n_sel=n_sel),
        grid=(bsz, s // qt),
        in_specs=[pl.BlockSpec((1, qt, Q_WIDTH), lambda b, i: (b, i, 0)),
                  pl.BlockSpec((1, r, N_KV * KV_PACK), lambda b, i: (b, 0, 0)),
                  _const_spec(cbias.shape), _const_spec(overlap_t.shape)],
        out_specs=[pl.BlockSpec((1, qt, Q_WIDTH), lambda b, i: (b, i, 0)),
                   pl.BlockSpec((1, N_KV, qt, LANES), lambda b, i: (b, 0, i, 0)),
                   pl.BlockSpec((1, SEL_NQ, 8, LANES), lambda b, i: (b, i, 0, 0))],
        out_shape=[jax.ShapeDtypeStruct((bsz, s, Q_WIDTH), BF16),
                   jax.ShapeDtypeStruct((bsz, N_KV, s, LANES), BF16),
                   jax.ShapeDtypeStruct((bsz, s // Q_BLOCK, 8, LANES), jnp.int32)],
        scratch_shapes=[pltpu.VMEM((n_slc, SEL_NQ * N_KV * Q_BLOCK), F32)],
        compiler_params=_params(2),
        name="compressed_attention_select",
    )(q, kvc, cbias, overlap_t)


def _selected_kernel(first_ref, q_ref, selneg_ref, gate_ref, ocmp_ref, owin_ref, kps_ref, tris_ref,
                     o_ref, qs_ref, s_ref, mt_ref, m_ref, lt_ref, acc_ref):
    qbs = [pl.program_id(1) * SLC_NQ + k for k in range(SLC_NQ)]
    rows = SLC_NQ * HEADS_PER_KV * Q_BLOCK
    heads = [_padded_heads(q_ref[0, k * Q_BLOCK:(k + 1) * Q_BLOCK, :]) for k in range(SLC_NQ)]
    for g in range(N_KV):
        qs = []
        for k in range(SLC_NQ):
            masked_out = selneg_ref[0, g, k * Q_BLOCK:(k + 1) * Q_BLOCK, :].astype(F32)
            qs += [jnp.concatenate([heads[k][hd], masked_out + _feature_rows(qbs[k], hd)[0]], axis=1)
                   for hd in range(g * HEADS_PER_KV, (g + 1) * HEADS_PER_KV)]
        qs_ref[g] = jnp.concatenate(qs, axis=0).astype(BF16)

    unit_blocks = SLC_UNIT // SLC_BLOCK
    diag_unit = qbs[0] // unit_blocks
    first = functools.reduce(jnp.minimum, [first_ref[pl.program_id(0), qb] for qb in qbs])
    first_unit = jnp.minimum(first, qbs[0]) // unit_blocks
    lead = jnp.minimum(first_unit, 1)
    n_units = diag_unit - first_unit + 1 + lead

    def unit_of(i):
        return jnp.where((i < lead) | (i >= n_units), 0, first_unit + i - lead)

    def unit_start(i):
        return pl.multiple_of(unit_of(i) * SLC_UNIT, SLC_UNIT)

    def slot_start(i):
        return pl.multiple_of(i * SLC_UNIT, SLC_UNIT)

    def score_unit(g, i):
        tri = []
        for qb in qbs:
            variant = jnp.where(i >= n_units, unit_blocks + 1,
                                jnp.where(unit_of(i) == diag_unit, qb % unit_blocks, unit_blocks))
            tri += [tris_ref[variant]] * HEADS_PER_KV
        tri = jnp.concatenate(tri, axis=0)
        s = (_dot_nt(qs_ref[g], kps_ref[0, g, pl.ds(unit_start(i), SLC_UNIT), :]) + tri) * LOG2E
        s_ref[g, :, pl.ds(slot_start(i), SLC_UNIT)] = s
        mt_ref[g] = jnp.maximum(jnp.maximum(mt_ref[g], s[:, 0:LANES]), s[:, LANES:SLC_UNIT])

    def value_unit(g, i):
        m = m_ref[g]
        es = [jnp.exp2(s_ref[g, :, pl.ds(slot_start(i) + j * LANES, LANES)] - m) for j in range(SLC_UNIT // LANES)]
        lt_ref[g] = lt_ref[g] + functools.reduce(jnp.add, es)
        values = kps_ref[0, g, pl.ds(unit_start(i), SLC_UNIT), 0:KV_PACK]
        acc_ref[g] = acc_ref[g] + _dot(jnp.concatenate(es, axis=1).astype(BF16), values)

    def sweep(step):
        def pair(p, carry):
            for u in range(2):
                for g in range(N_KV):
                    step(g, 2 * p + u)
            return carry

        lax.fori_loop(0, (n_units + 1) // 2, pair, 0)

    mt_ref[...] = jnp.full(mt_ref.shape, M_INIT, F32)
    sweep(score_unit)
    for g in range(N_KV):
        m_ref[g] = jnp.broadcast_to(jnp.max(mt_ref[g], axis=-1, keepdims=True), (rows, LANES))
    lt_ref[...] = jnp.zeros(lt_ref.shape, F32)
    acc_ref[...] = jnp.zeros(acc_ref.shape, F32)
    sweep(value_unit)

    lane = lax.broadcasted_iota(jnp.int32, (Q_BLOCK, LANES), 1)
    for g in range(N_KV):
        l = jnp.sum(lt_ref[g], axis=-1, keepdims=True)
        o_rows = acc_ref[g] * jnp.where(l > 0.0, 1.0 / l, 0.0)
        for k in range(SLC_NQ):
            qr = slice(k * Q_BLOCK, (k + 1) * Q_BLOCK)
            o_slc = _pack_heads(o_rows[k * HEADS_PER_KV * Q_BLOCK:(k + 1) * HEADS_PER_KV * Q_BLOCK])
            gt = gate_ref[0, g, qr, :]
            for j in range(HEADS_PER_KV // 2):
                cols = slice((g * HEADS_PER_KV // 2 + j) * LANES, (g * HEADS_PER_KV // 2 + j + 1) * LANES)
                branches = (ocmp_ref[0, qr, cols].astype(F32), o_slc[:, j * LANES:(j + 1) * LANES],
                            owin_ref[0, qr, cols].astype(F32))
                total = None
                for c, branch in enumerate(branches):
                    even, odd = 3 * (2 * j) + c, 3 * (2 * j + 1) + c
                    gate = jnp.take_along_axis(gt, jnp.where(lane < HEAD_DIM, even, odd), axis=1)
                    total = gate * branch if total is None else total + gate * branch
                o_ref[0, qr, cols] = total.astype(BF16)


def _selected(first, q, selneg, gates, o_cmp, o_win, kps, tri_slc):
    bsz, s, _ = q.shape
    qt = SLC_NQ * Q_BLOCK
    rows = HEADS_PER_KV * qt
    assert (SLC_UNIT // SLC_BLOCK) % SLC_NQ == 0 and s % qt == 0

    def row(n):
        return pl.BlockSpec((1, qt, n), lambda b, i, first_ref: (b, i, 0))

    def grouped(n):
        return pl.BlockSpec((1, N_KV, qt, n), lambda b, i, first_ref: (b, 0, i, 0))

    grid_spec = pltpu.PrefetchScalarGridSpec(
        num_scalar_prefetch=1,
        grid=(bsz, s // qt),
        in_specs=[row(Q_WIDTH), grouped(LANES), grouped(LANES), row(Q_WIDTH), row(Q_WIDTH),
                  pl.BlockSpec((1, N_KV, s, KEY_WIDTH), lambda b, i, first_ref: (b, 0, 0, 0)),
                  _const_spec(tri_slc.shape)],
        out_specs=row(Q_WIDTH),
        scratch_shapes=[pltpu.VMEM((N_KV, rows, KEY_WIDTH), BF16),
                        pltpu.VMEM((N_KV, rows, s + SLC_UNIT), F32)]
        + [pltpu.VMEM((N_KV, rows, LANES), F32)] * 4)
    return pl.pallas_call(
        _selected_kernel,
        grid_spec=grid_spec,
        out_shape=jax.ShapeDtypeStruct((bsz, s, Q_WIDTH), BF16),
        compiler_params=_params(2),
        name="selected_attention",
    )(first, q, selneg, gates, o_cmp, o_win, kps, tri_slc)


def _attention(q, gates, kvc, kps, kpw):
    bsz, s, _ = q.shape
    r = kvc.shape[1]
    n_cmp = r - 1
    n_slc = s // SLC_BLOCK
    n_sel = min(N_SELECT, n_slc)
    rows = HEADS_PER_KV * Q_BLOCK
    assert n_slc <= SLC_BLOCK and n_slc % RANK_UNROLL == 0 and s % SLC_UNIT == 0 and s >= WIN_KEYS
    assert s % (WIN_NQ * Q_BLOCK) == 0 and s % (SEL_NQ * Q_BLOCK) == 0

    slopes = np.array([_alibi_slope(hd) for hd in range(N_HEADS)])
    slope_rows = np.repeat(slopes.reshape(N_KV, HEADS_PER_KV), Q_BLOCK, axis=1).reshape(N_KV, rows, 1)
    cbias = np.tile(slope_rows * (CMP_STRIDE * np.arange(r))[None, None, :], (1, SEL_NQ, 1))
    start = np.arange(r)[None, :] * CMP_STRIDE
    blk = np.arange(n_slc)[:, None] * SLC_BLOCK
    overlap_t = ((start < blk + SLC_BLOCK) & (start + CMP_BLOCK > blk) & (np.arange(r)[None, :] < n_cmp))

    ql = np.arange(Q_BLOCK)[:, None]
    kl = np.arange(SLC_BLOCK)[None, :]
    lower = np.where(kl > ql, -BIG, 0.0)
    upper = np.where(kl <= ql, -BIG, 0.0)
    unit_blocks = SLC_UNIT // SLC_BLOCK
    tri_slc = np.zeros((unit_blocks + 2, Q_BLOCK, SLC_UNIT))
    for j in range(unit_blocks):
        tri_slc[j, :, j * SLC_BLOCK:(j + 1) * SLC_BLOCK] = lower
    tri_slc[unit_blocks + 1] = -BIG
    win_blocks = WINDOW // SLC_BLOCK
    lead = WIN_KEYS // SLC_BLOCK - 1
    tri_win = np.zeros((lead + 1, Q_BLOCK, WIN_KEYS))
    for v in range(lead + 1):
        diag = v
        tri_win[v, :, diag * SLC_BLOCK:(diag + 1) * SLC_BLOCK] = lower
        if diag >= win_blocks:
            old = diag - win_blocks
            tri_win[v, :, old * SLC_BLOCK:(old + 1) * SLC_BLOCK] = upper

    o_win = _window(q, kpw, jnp.asarray(tri_win, F32))
    o_cmp, selneg, first = _select(q, kvc, jnp.asarray(cbias, F32), jnp.asarray(overlap_t, BF16), n_sel)
    return _selected(first[:, :, 0, 0], q, selneg, gates, o_cmp, o_win, kps, jnp.asarray(tri_slc, F32))


def _merge_ffn_kernel(x_ref, gate1_ref, uprev_ref, u_ref, o_ref, gbr_ref, pw_ref, ps_ref, wa_ref, wb_ref, wo_ref,
                      g2_ref, b2_ref, shift2_ref, scale2_ref, gate2_ref, g3_ref, b3_ref, w_in_ref, w_out_ref,
                      out_ref, ubuf_ref):
    i = pl.program_id(1)
    tm = u_ref.shape[1]
    ubuf_ref[0:POOL_HALO, :] = jnp.where(i == 0, 0.0, uprev_ref[0])
    ubuf_ref[POOL_HALO:POOL_HALO + tm, :] = u_ref[0]
    t = i * tm + lax.broadcasted_iota(jnp.int32, (tm, 1), 0)

    mixed = []
    for gi, w in enumerate(POOL_WINDOWS):
        cs = slice(gi * POOL_GROUP, (gi + 1) * POOL_GROUP)
        cur = ubuf_ref[POOL_HALO:POOL_HALO + tm, cs]
        total = cur
        for k in range(1, w):
            total = total + ubuf_ref[POOL_HALO - k:POOL_HALO - k + tm, cs]
        inv_cnt = 1.0 / jnp.minimum(t + 1, w).astype(F32)
        delta = (total * inv_cnt - cur).astype(BF16)
        mixed.append((_dot(delta, pw_ref[gi]) * ps_ref[:, cs]).astype(BF16))
    y_a = _dot(jnp.concatenate(mixed, axis=1), wa_ref[...])
    y_b = _dot(o_ref[0], wb_ref[...])
    d = y_a.shape[1]
    y = (gbr_ref[0, :, 0:d].astype(F32) * y_a + gbr_ref[0, :, d:2 * d].astype(F32) * y_b).astype(BF16)
    y = _dot(y, wo_ref[...])
    x2 = _layer_norm(ALPHA * x_ref[0] + gate1_ref[0] * y, g2_ref[...], b2_ref[...])
    out_ref[0] = _swiglu_block(x2, shift2_ref[0], scale2_ref[0], gate2_ref[0], w_in_ref, w_out_ref,
                               g3_ref[...], b3_ref[...])


def _merge_ffn(x, gate1, u, o, gbr, pool_w, pool_scale, w_a, w_b, w_o, g2, b2,
               shift2, scale2, gate2, g3, b3, w_in, w_out):
    bsz, s, d = x.shape
    tm = min(ROW_TILE, s)
    halo_blocks = tm // POOL_HALO

    def row(n):
        return pl.BlockSpec((1, tm, n), lambda bi, i: (bi, i, 0))

    mod = pl.BlockSpec((1, 1, d), lambda bi, i: (bi, 0, 0))
    vec = _const_spec((1, d))
    return pl.pallas_call(
        _merge_ffn_kernel,
        grid=(bsz, s // tm),
        in_specs=[row(d), mod,
                  pl.BlockSpec((1, POOL_HALO, POOL_WIDTH),
                               lambda bi, i: (bi, jnp.maximum(i * halo_blocks - 1, 0), 0)),
                  row(POOL_WIDTH), row(Q_WIDTH), row(2 * d),
                  _const_spec(pool_w.shape), _const_spec((1, POOL_WIDTH)),
                  _const_spec(w_a.shape), _const_spec(w_b.shape), _const_spec(w_o.shape),
                  vec, vec, mod, mod, mod, vec, vec, _const_spec(w_in.shape), _const_spec(w_out.shape)],
        out_specs=row(d),
        out_shape=jax.ShapeDtypeStruct((bsz, s, d), F32),
        scratch_shapes=[pltpu.VMEM((POOL_HALO + tm, POOL_WIDTH), F32)],
        compiler_params=_params(2),
        name="pool_merge_ffn",
    )(x, gate1, u, u, o, gbr, pool_w, pool_scale.reshape(1, POOL_WIDTH), w_a, w_b, w_o,
      g2.reshape(1, d), b2.reshape(1, d), shift2, scale2, gate2, g3.reshape(1, d), b3.reshape(1, d), w_in, w_out)


def _mixer_in_weights(w):
    sizes = (POOL_WIDTH, Q_WIDTH) + (KV_WIDTH,) * 6 + (3 * N_HEADS, 2 * D_MODEL)
    offs = np.concatenate([[0], np.cumsum(sizes)])
    u, q, k_cmp, v_cmp, k_slc, v_slc, k_win, v_win, g_nsa, g_br = [w[:, offs[i]:offs[i + 1]] for i in range(10)]
    cols = [u, q, k_cmp, v_cmp]
    for k, v in ((k_slc, v_slc), (k_win, v_win)):
        for g in range(N_KV):
            cols += [k[:, g * HEAD_DIM:(g + 1) * HEAD_DIM], v[:, g * HEAD_DIM:(g + 1) * HEAD_DIM]]
    per_g = 3 * HEADS_PER_KV
    for g in range(N_KV):
        cols += [g_nsa[:, g * per_g:(g + 1) * per_g], jnp.zeros((w.shape[0], LANES - per_g), w.dtype)]
    cols.append(g_br)
    return jnp.concatenate(cols, axis=1).astype(BF16)


def _compress_weights(pos, w1, w2, value_slot):
    same_group = np.eye(N_KV, dtype=np.float32)

    def expand_w1(half):
        wh = half.reshape(CMP_STRIDE, 1, HEAD_DIM, 1, CMP_HIDDEN)
        z = wh * same_group.reshape(1, N_KV, 1, N_KV, 1)
        return z.reshape(CMP_STRIDE * KV_WIDTH, N_KV * CMP_HIDDEN)

    half_rows = CMP_STRIDE * HEAD_DIM
    w1_big = jnp.stack([expand_w1(w1[:half_rows]), expand_w1(w1[half_rows:])]).astype(BF16)
    slot = np.eye(2, dtype=np.float32)[value_slot]
    w2_big = (w2.reshape(1, CMP_HIDDEN, 1, 1, HEAD_DIM) * same_group.reshape(N_KV, 1, N_KV, 1, 1)
              * slot.reshape(1, 1, 1, 2, 1))
    w2_big = w2_big.reshape(N_KV * CMP_HIDDEN, N_KV * KV_PACK).astype(BF16)
    pos_rows = jnp.broadcast_to(pos.reshape(2, CMP_STRIDE, 1, HEAD_DIM), (2, CMP_STRIDE, N_KV, HEAD_DIM))
    return pos_rows.reshape(2, CMP_STRIDE * KV_WIDTH), w1_big, w2_big


def kernel(x, c, ln_in_g, ln_in_b, w_ada, b_ada, ffn1_w_in, ffn1_w_out, ln1_g, ln1_b, w_mix_in, pool_w, pool_scale,
           cmp_pos_k, cmp_k_w1, cmp_k_w2, cmp_pos_v, cmp_v_w1, cmp_v_w2, w_branch_a, w_branch_b, w_mix_out,
           ln2_g, ln2_b, ffn2_w_in, ffn2_w_out, ln3_g, ln3_b):
    bsz, s, d = x.shape
    for l in range(DEPTH):
        ada = _ada(c, w_ada[l], b_ada[l]).reshape(bsz, 3, 3, 1, d)
        mod = lambda i, j: ada[:, i, j]

        x, u, q, kc, vc, kvs, kvw, gates, gbr = _ffn_mixin(
            x, mod(0, 0), mod(0, 1), mod(0, 2), ln_in_g, ln_in_b, ln1_g[l], ln1_b[l],
            ffn1_w_in[l].astype(BF16), ffn1_w_out[l].astype(BF16),
            mod(1, 0), mod(1, 1), _mixer_in_weights(w_mix_in[l]), pre_ln=l == 0)
        posk, w1k, w2k = _compress_weights(cmp_pos_k[l], cmp_k_w1[l], cmp_k_w2[l], 0)
        posv, w1v, w2v = _compress_weights(cmp_pos_v[l], cmp_v_w1[l], cmp_v_w2[l], 1)
        kvc = _compress(kc, vc, posk, posv, w1k, w1v, w2k, w2v)
        o = _attention(q, gates, kvc, kvs, kvw)
        x = _merge_ffn(x, mod(1, 2), u, o, gbr, pool_w[l].astype(BF16), pool_scale[l],
                       w_branch_a[l].astype(BF16), w_branch_b[l].astype(BF16), w_mix_out[l].astype(BF16),
                       ln2_g[l], ln2_b[l], mod(2, 0), mod(2, 1), mod(2, 2), ln3_g[l], ln3_b[l],
                       ffn2_w_in[l].astype(BF16), ffn2_w_out[l].astype(BF16))
    return x
```

```python
import functools

import numpy as np
import jax
import jax.numpy as jnp
from jax import lax
from jax.experimental import pallas as pl
from jax.experimental.pallas import tpu as pltpu

F32 = jnp.float32
BF16 = jnp.bfloat16

D_MODEL = 1024
POOL_WIDTH = D_MODEL // 2
POOL_WINDOWS = (2, 4, 8, 16)
POOL_GROUP = POOL_WIDTH // len(POOL_WINDOWS)
POOL_HALO = 16
HEAD_DIM = 64
N_HEADS = (D_MODEL // 2) // HEAD_DIM
N_KV = 2
HEADS_PER_KV = N_HEADS // N_KV
Q_WIDTH = N_HEADS * HEAD_DIM
KV_WIDTH = N_KV * HEAD_DIM
CMP_STRIDE = 16
CMP_BLOCK = 2 * CMP_STRIDE
CMP_HIDDEN = 4 * HEAD_DIM
SLC_BLOCK = 64
N_SELECT = 16
WINDOW = 512
Q_BLOCK = SLC_BLOCK
ALIBI_MAX = 8.0
D_FF = 2816
DEPTH = 1
ALPHA = (2.0 * DEPTH) ** 0.25
LN_EPS = 1e-5
NEG = -1e30
FORCE = 1e9

LANES = 128
KV_PACK = 2 * HEAD_DIM
KEY_WIDTH = 2 * KV_PACK
FEAT_OFFSET_LANE = SLC_BLOCK
BIG = 1e30
M_INIT = -3e38
FF_CHUNK = 256
GATE_CHUNK = 512
ROW_TILE = 512
SLC_UNIT = 256
RANK_UNROLL = 2
WIN_NQ = 4
SEL_NQ = 2
SLC_NQ = 2
TRIP_WIDTHS = (6, 4, 2)
LOG2E = 1.4426950408889634
WIN_KEYS = WINDOW + 2 * Q_BLOCK
VMEM_LIMIT = 56 * 1024 * 1024

_C_U = 0
_C_Q = _C_U + POOL_WIDTH
_C_KC = _C_Q + Q_WIDTH
_C_VC = _C_KC + KV_WIDTH
_C_KVS = _C_VC + KV_WIDTH
_C_KVW = _C_KVS + N_KV * KV_PACK
_C_GN = _C_KVW + N_KV * KV_PACK
_C_GBR = _C_GN + N_KV * LANES
_C_END = _C_GBR + 2 * D_MODEL


def _dot(a, b):
    return jnp.dot(a, b, preferred_element_type=F32)


def _dot_nt(a, b):
    return lax.dot_general(a, b, (((1,), (1,)), ((), ())), preferred_element_type=F32)


def _layer_norm(x, g, b):
    mu = jnp.mean(x, axis=-1, keepdims=True)
    xc = x - mu
    var = jnp.mean(xc * xc, axis=-1, keepdims=True)
    return xc * lax.rsqrt(var + LN_EPS) * g + b


def _masked_softmax(s, mask):
    sm = jnp.where(mask, s, NEG)
    m = jnp.max(sm, axis=-1, keepdims=True)
    e = jnp.exp(sm - m)
    p = e * (1.0 / jnp.sum(e, axis=-1, keepdims=True))
    return jnp.where(mask, p, 0.0)


def _const_spec(shape):
    nd = len(shape)
    return pl.BlockSpec(shape, lambda *_: (0,) * nd, pipeline_mode=pl.Buffered(1))


def _params(n_grid):
    return pltpu.CompilerParams(dimension_semantics=("parallel",) * n_grid, vmem_limit_bytes=VMEM_LIMIT)


def _ada_kernel(c_ref, w_ref, b_ref, o_ref):
    c = c_ref[...]
    c_act = (c * jax.nn.sigmoid(c)).astype(BF16)
    o_ref[...] = _dot(c_act, w_ref[...].astype(BF16)) + b_ref[...]


def _ada(c, w, b):
    bsz, d = c.shape
    n = w.shape[1]
    tn = D_MODEL
    return pl.pallas_call(
        _ada_kernel,
        grid=(n // tn,),
        in_specs=[pl.BlockSpec((bsz, d), lambda j: (0, 0)),
                  pl.BlockSpec((d, tn), lambda j: (0, j)),
                  pl.BlockSpec((1, tn), lambda j: (0, j))],
        out_specs=pl.BlockSpec((bsz, tn), lambda j: (0, j)),
        out_shape=jax.ShapeDtypeStruct((bsz, n), F32),
        compiler_params=_params(1),
        name="ada_proj",
    )(c, w, b.reshape(1, n))


def _swiglu_block(x, shift, scale, gate, w_in_ref, w_out_ref, g, b):
    h = (x * (1.0 + scale) + shift).astype(BF16)
    acc = jnp.zeros(x.shape, F32)
    for j in range(D_FF // FF_CHUNK):
        c0 = j * FF_CHUNK
        gt = _dot(h, w_in_ref[:, c0:c0 + FF_CHUNK])
        up = _dot(h, w_in_ref[:, D_FF + c0:D_FF + c0 + FF_CHUNK])
        act = (gt * jax.nn.sigmoid(gt) * up).astype(BF16)
        acc = acc + _dot(act, w_out_ref[c0:c0 + FF_CHUNK, :])
    return _layer_norm(ALPHA * x + 0.5 * gate * acc, g, b)


def _ffn_mixin_kernel(x_ref, shift0_ref, scale0_ref, gate0_ref, g0_ref, b0_ref, g1_ref, b1_ref, w_in_ref, w_out_ref,
                      shift1_ref, scale1_ref, w_ref,
                      x1_ref, u_ref, q_ref, kc_ref, vc_ref, kvs_ref, kvw_ref, gn_ref, gbr_ref, *, pre_ln):
    x = x_ref[0]
    if pre_ln:
        x = _layer_norm(x, g0_ref[...], b0_ref[...])
    x1 = _swiglu_block(x, shift0_ref[0], scale0_ref[0], gate0_ref[0], w_in_ref, w_out_ref, g1_ref[...], b1_ref[...])
    x1_ref[0] = x1

    h = (x1 * (1.0 + scale1_ref[0]) + shift1_ref[0]).astype(BF16)

    def proj(c0, c1):
        return _dot(h, w_ref[:, c0:c1])

    tm = x_ref.shape[1]
    pos = pl.program_id(1) * tm + lax.broadcasted_iota(jnp.int32, (tm, LANES), 0)
    lane = lax.broadcasted_iota(jnp.int32, (tm, LANES), 1)
    feat = jnp.where(lane == pos // SLC_BLOCK, 1.0,
                     jnp.where(lane == FEAT_OFFSET_LANE, (pos % SLC_BLOCK).astype(F32), 0.0)).astype(BF16)

    def put_u(z):
        u_ref[0] = z

    def put_q(z):
        q_ref[0] = (z * HEAD_DIM ** -0.5).astype(BF16)

    def put_cmp(z):
        kc_ref[0] = z[:, 0:KV_WIDTH]
        vc_ref[0] = z[:, KV_WIDTH:2 * KV_WIDTH]

    def put_keys(ref):
        def put(z):
            zb = z.astype(BF16)
            for g in range(N_KV):
                ref[0, g, :, 0:KV_PACK] = zb[:, g * KV_PACK:(g + 1) * KV_PACK]
                ref[0, g, :, KV_PACK:KEY_WIDTH] = feat
        return put

    def put_head_gates(z):
        sg = jax.nn.sigmoid(z)
        for g in range(N_KV):
            gn_ref[0, g] = sg[:, g * LANES:(g + 1) * LANES]

    def put_branch_gates(c0):
        def put(z):
            gbr_ref[0, :, c0 - _C_GBR:c0 - _C_GBR + z.shape[1]] = jax.nn.sigmoid(z).astype(BF16)
        return put

    gate_cols = [(c0, c0 + GATE_CHUNK) for c0 in range(_C_GBR, _C_END, GATE_CHUNK)]
    light = [((_C_U, _C_Q), put_u), ((_C_Q, _C_KC), put_q), ((_C_KC, _C_KVS), put_cmp),
             ((_C_KVS, _C_KVW), put_keys(kvs_ref)), ((_C_KVW, _C_GN), put_keys(kvw_ref)),
             ((_C_GN, _C_GBR), put_head_gates)]
    stages = []
    for n in range(max(len(gate_cols), len(light))):
        if n < len(gate_cols):
            stages.append((gate_cols[n], put_branch_gates(gate_cols[n][0])))
        if n < len(light):
            stages.append(light[n])
    pending = proj(*stages[0][0])
    for n, (_, sink) in enumerate(stages):
        z = pending
        if n + 1 < len(stages):
            pending = proj(*stages[n + 1][0])
        sink(z)


def _ffn_mixin(x, shift0, scale0, gate0, g0, b0, g1, b1, w_in, w_out, shift1, scale1, w_mix, *, pre_ln):
    bsz, s, d = x.shape
    tm = min(ROW_TILE, s)

    def row(n, dt):
        return pl.BlockSpec((1, tm, n), lambda b, i: (b, i, 0)), jax.ShapeDtypeStruct((bsz, s, n), dt)

    def grouped(n, dt):
        return (pl.BlockSpec((1, N_KV, tm, n), lambda b, i: (b, 0, i, 0)),
                jax.ShapeDtypeStruct((bsz, N_KV, s, n), dt))

    outs = [row(d, F32), row(POOL_WIDTH, F32), row(Q_WIDTH, BF16), row(KV_WIDTH, F32), row(KV_WIDTH, F32),
            grouped(KEY_WIDTH, BF16), grouped(KEY_WIDTH, BF16), grouped(LANES, F32), row(2 * D_MODEL, BF16)]
    mod = pl.BlockSpec((1, 1, d), lambda b, i: (b, 0, 0))
    vec = _const_spec((1, d))
    return pl.pallas_call(
        functools.partial(_ffn_mixin_kernel, pre_ln=pre_ln),
        grid=(bsz, s // tm),
        in_specs=[row(d, F32)[0], mod, mod, mod, vec, vec, vec, vec,
                  _const_spec(w_in.shape), _const_spec(w_out.shape), mod, mod, _const_spec(w_mix.shape)],
        out_specs=[o[0] for o in outs],
        out_shape=[o[1] for o in outs],
        compiler_params=_params(2),
        name="ffn_mixer_in",
    )(x, shift0, scale0, gate0, g0.reshape(1, d), b0.reshape(1, d), g1.reshape(1, d), b1.reshape(1, d),
      w_in, w_out, shift1, scale1, w_mix)


def _compress_kernel(k_ref, v_ref, posk_ref, posv_ref, w1k_ref, w1v_ref, w2k_ref, w2v_ref, o_ref, nxt_ref):
    r = k_ref.shape[1] // CMP_STRIDE

    def hidden(x_ref, pos_ref, w1_ref):
        x = jnp.concatenate([x_ref[0, pl.ds(j, r, stride=CMP_STRIDE), :] for j in range(CMP_STRIDE)], axis=1)
        first = _dot((x + pos_ref[0:1, :]).astype(BF16), w1_ref[0])
        nxt_ref[0:r, :] = _dot((x + pos_ref[1:2, :]).astype(BF16), w1_ref[1])
        nxt_ref[r:r + 8, :] = jnp.zeros((8, nxt_ref.shape[1]), F32)
        return jax.nn.gelu(first + nxt_ref[1:r + 1, :]).astype(BF16)

    hk = hidden(k_ref, posk_ref, w1k_ref)
    hv = hidden(v_ref, posv_ref, w1v_ref)
    o_ref[0] = (_dot(hk, w2k_ref[...]) + _dot(hv, w2v_ref[...])).astype(BF16)


def _compress(kc, vc, posk, posv, w1k, w1v, w2k, w2v):
    bsz, s, n = kc.shape
    r = s // CMP_STRIDE
    hid = w1k.shape[2]
    row = pl.BlockSpec((1, s, n), lambda b: (b, 0, 0))
    return pl.pallas_call(
        _compress_kernel,
        grid=(bsz,),
        in_specs=[row, row, _const_spec(posk.shape), _const_spec(posv.shape),
                  _const_spec(w1k.shape), _const_spec(w1v.shape),
                  _const_spec(w2k.shape), _const_spec(w2v.shape)],
        out_specs=pl.BlockSpec((1, r, N_KV * KV_PACK), lambda b: (b, 0, 0)),
        out_shape=jax.ShapeDtypeStruct((bsz, r, N_KV * KV_PACK), BF16),
        scratch_shapes=[pltpu.VMEM((r + 8, hid), F32)],
        compiler_params=_params(1),
        name="compress_mlp",
    )(kc, vc, posk, posv, w1k, w1v, w2k, w2v)


def _alibi_slope(head):
    return 2.0 ** (-ALIBI_MAX * (head + 1) / N_HEADS)


def _padded_heads(q_tile):
    lane = lax.broadcasted_iota(jnp.int32, (Q_BLOCK, LANES), 1)
    qf = q_tile.astype(F32)
    heads = []
    for hd in range(N_HEADS):
        slab = qf[:, (hd // 2) * LANES:(hd // 2 + 1) * LANES]
        if hd % 2:
            slab = pltpu.roll(slab, HEAD_DIM, 1)
        heads.append(jnp.where(lane < HEAD_DIM, slab, 0.0))
    return heads


def _pack_heads(o_rows):
    lane = lax.broadcasted_iota(jnp.int32, (Q_BLOCK, LANES), 1)
    pairs = []
    for j in range(HEADS_PER_KV // 2):
        even = o_rows[(2 * j) * Q_BLOCK:(2 * j + 1) * Q_BLOCK]
        odd = o_rows[(2 * j + 1) * Q_BLOCK:(2 * j + 2) * Q_BLOCK]
        pairs.append(jnp.where(lane < HEAD_DIM, pltpu.roll(even, HEAD_DIM, 1), odd))
    return jnp.concatenate(pairs, axis=1)


def _feature_rows(qb, hd):
    lane_row = lax.broadcasted_iota(jnp.int32, (1, LANES), 1)
    slope = _alibi_slope(hd)
    tail = jnp.where(lane_row == FEAT_OFFSET_LANE, slope, 0.0)
    block_bias = slope * SLC_BLOCK * (lane_row - qb).astype(F32)
    in_window = (lane_row >= qb - WINDOW // SLC_BLOCK) & (lane_row <= qb)
    slc_row = jnp.where(lane_row < SLC_BLOCK, block_bias, tail)
    win_row = jnp.where(lane_row < SLC_BLOCK, jnp.where(in_window, block_bias, -BIG), tail)
    return slc_row, win_row


def _window_kernel(q_ref, kpw_ref, triw_ref, o_ref):
    gw = HEADS_PER_KV * HEAD_DIM
    chains = [(k, g) for k in range(WIN_NQ) for g in range(N_KV)]

    def scores(k, g):
        qb = pl.program_id(1) * WIN_NQ + k
        heads = _padded_heads(q_ref[0, k * Q_BLOCK:(k + 1) * Q_BLOCK, :])
        w0 = pl.multiple_of(jnp.maximum((qb + 1) * Q_BLOCK - WIN_KEYS, 0), Q_BLOCK)
        tri = jnp.concatenate([triw_ref[jnp.minimum(qb, WIN_KEYS // SLC_BLOCK - 1)]] * HEADS_PER_KV, axis=0)
        qw = [jnp.concatenate([heads[hd], jnp.broadcast_to(_feature_rows(qb, hd)[1], (Q_BLOCK, LANES))], axis=1)
              for hd in range(g * HEADS_PER_KV, (g + 1) * HEADS_PER_KV)]
        kp = kpw_ref[0, g, pl.ds(w0, WIN_KEYS), :]
        return _dot_nt(jnp.concatenate(qw, axis=0).astype(BF16), kp) + tri, kp

    pending = scores(*chains[0])
    for n, (k, g) in enumerate(chains):
        s, kp = pending
        if n + 1 < len(chains):
            pending = scores(*chains[n + 1])
        e = jnp.exp(s - jnp.max(s, axis=-1, keepdims=True))
        o = _dot(e.astype(BF16), kp[:, 0:KV_PACK]) * (1.0 / jnp.sum(e, axis=-1, keepdims=True))
        o_ref[0, k * Q_BLOCK:(k + 1) * Q_BLOCK, g * gw:(g + 1) * gw] = _pack_heads(o).astype(BF16)


def _window(q, kpw, tri_win):
    bsz, s, _ = q.shape
    qt = WIN_NQ * Q_BLOCK
    return pl.pallas_call(
        _window_kernel,
        grid=(bsz, s // qt),
        in_specs=[pl.BlockSpec((1, qt, Q_WIDTH), lambda b, i: (b, i, 0)),
                  pl.BlockSpec((1, N_KV, s, KEY_WIDTH), lambda b, i: (b, 0, 0, 0)),
                  _const_spec(tri_win.shape)],
        out_specs=pl.BlockSpec((1, qt, Q_WIDTH), lambda b, i: (b, i, 0)),
        out_shape=jax.ShapeDtypeStruct((bsz, s, Q_WIDTH), BF16),
        compiler_params=_params(2),
        name="window_attention",
    )(q, kpw, tri_win)


def _select_kernel(q_ref, kvc_ref, cbias_ref, ovl_ref, ocmp_ref, selneg_ref, first_ref, score_ref, *, n_sel):
    step = pl.program_id(1)
    rows = SEL_NQ * HEADS_PER_KV * Q_BLOCK
    n_slc = ovl_ref.shape[0]
    r = kvc_ref.shape[1]
    lanes_q = SEL_NQ * N_KV * Q_BLOCK
    heads = [_padded_heads(q_ref[0, k * Q_BLOCK:(k + 1) * Q_BLOCK, :]) for k in range(SEL_NQ)]

    row = lax.broadcasted_iota(jnp.int32, (rows, 1), 0)
    t = (step * SEL_NQ + row // (HEADS_PER_KV * Q_BLOCK)) * Q_BLOCK + (row & (Q_BLOCK - 1))
    last_cmp = (t - (CMP_BLOCK - 1)) >> 4
    visible = lax.broadcasted_iota(jnp.int32, (rows, r), 1) <= last_cmp
    p_sums = [[None] * N_KV for _ in range(SEL_NQ)]
    gw = HEADS_PER_KV * HEAD_DIM
    def cmp_scores(g):
        q_pad = jnp.concatenate([heads[k][g * HEADS_PER_KV + h] for k in range(SEL_NQ)
                                 for h in range(HEADS_PER_KV)], axis=0).astype(BF16)
        return _dot_nt(q_pad, kvc_ref[0, :, g * KV_PACK:(g + 1) * KV_PACK])

    raw = [cmp_scores(g) for g in range(N_KV)]
    for g in range(N_KV):
        kvc = kvc_ref[0, :, g * KV_PACK:(g + 1) * KV_PACK]
        p_cmp = _masked_softmax(raw[g] + cbias_ref[g], visible)
        o_cmp = _dot(p_cmp.astype(BF16), kvc)
        for k in range(SEL_NQ):
            base = k * HEADS_PER_KV * Q_BLOCK
            ocmp_ref[0, k * Q_BLOCK:(k + 1) * Q_BLOCK, g * gw:(g + 1) * gw] = _pack_heads(
                o_cmp[base:base + HEADS_PER_KV * Q_BLOCK]).astype(BF16)
            p_sum = p_cmp[base:base + Q_BLOCK]
            for h in range(1, HEADS_PER_KV):
                p_sum = p_sum + p_cmp[base + h * Q_BLOCK:base + (h + 1) * Q_BLOCK]
            p_sums[k][g] = p_sum

    p_all = jnp.concatenate([p_sums[k][g] for k in range(SEL_NQ) for g in range(N_KV)], axis=0)
    p_hi = p_all.astype(BF16)
    p_lo = (p_all - p_hi.astype(F32)).astype(BF16)
    ovl = ovl_ref[...]
    imp_t = _dot_nt(ovl, p_hi) + _dot_nt(ovl, p_lo)

    blk = lax.broadcasted_iota(jnp.int32, (n_slc, lanes_q), 0)
    qb = step * SEL_NQ + lax.broadcasted_iota(jnp.int32, (n_slc, lanes_q), 1) // (N_KV * Q_BLOCK)
    forced = (blk == 0) | (blk == qb) | (blk == qb - 1)
    score = jnp.where(blk > qb, NEG, jnp.where(forced, FORCE, imp_t))
    score_ref[...] = score

    def rank_step(i, ranks):
        out = []
        for u in range(RANK_UNROLL):
            ri = score_ref[pl.ds(i * RANK_UNROLL + u, 1), :]
            beats = (ri > score) | ((ri == score) & (blk > i * RANK_UNROLL + u))
            out.append(ranks[u] + jnp.where(beats, 1.0, 0.0))
        return tuple(out)

    ranks = lax.fori_loop(0, (step * SEL_NQ + SEL_NQ - 1) // RANK_UNROLL + 1, rank_step,
                          (jnp.zeros((n_slc, lanes_q), F32),) * RANK_UNROLL)
    chosen = (functools.reduce(jnp.add, ranks) < n_sel) & (blk <= qb)

    oldest = jnp.where(chosen & (blk >= 1), blk, n_slc).astype(F32)
    sel_t = jnp.where(chosen, 1.0, 0.0).astype(BF16)
    sel_t = jnp.concatenate([sel_t, jnp.zeros((LANES - n_slc, lanes_q), BF16)], axis=0)
    eye = jnp.where(lax.broadcasted_iota(jnp.int32, (lanes_q, lanes_q), 0)
                    == lax.broadcasted_iota(jnp.int32, (lanes_q, lanes_q), 1), 1.0, 0.0).astype(BF16)
    sel = _dot_nt(eye, sel_t)
    lane = lax.broadcasted_iota(jnp.int32, (Q_BLOCK, LANES), 1)
    for k in range(SEL_NQ):
        lanes_k = slice(k * N_KV * Q_BLOCK, (k + 1) * N_KV * Q_BLOCK)
        first = jnp.min(jnp.min(oldest[:, lanes_k], axis=1, keepdims=True), axis=0, keepdims=True)
        first_ref[0, k] = jnp.broadcast_to(first, first_ref.shape[2:]).astype(jnp.int32)
        for g in range(N_KV):
            base = (k * N_KV + g) * Q_BLOCK
            selneg_ref[0, g, k * Q_BLOCK:(k + 1) * Q_BLOCK, :] = jnp.where(
                lane < SLC_BLOCK, (sel[base:base + Q_BLOCK] - 1.0) * BIG, 0.0).astype(BF16)


def _select(q, kvc, cbias, overlap_t, n_sel):
    bsz, s, _ = q.shape
    r = kvc.shape[1]
    n_slc = overlap_t.shape[0]
    qt = SEL_NQ * Q_BLOCK
    return pl.pallas_call(
        functools.partial(_select_kernel, n_sel=n_sel),
        grid=(bsz, s // qt),
        in_specs=[pl.BlockSpec((1, qt, Q_WIDTH), lambda b, i: (b, i, 0)),
                  pl.BlockSpec((1, r, N_KV * KV_PACK), lambda b, i: (b, 0, 0)),
                  _const_spec(cbias.shape), _const_spec(overlap_t.shape)],
        out_specs=[pl.BlockSpec((1, qt, Q_WIDTH), lambda b, i: (b, i, 0)),
                   pl.BlockSpec((1, N_KV, qt, LANES), lambda b, i: (b, 0, i, 0)),
                   pl.BlockSpec((1, SEL_NQ, 8, LANES), lambda b, i: (b, i, 0, 0))],
        out_shape=[jax.ShapeDtypeStruct((bsz, s, Q_WIDTH), BF16),
                   jax.ShapeDtypeStruct((bsz, N_KV, s, LANES), BF16),
                   jax.ShapeDtypeStruct((bsz, s // Q_BLOCK, 8, LANES), jnp.int32)],
        scratch_shapes=[pltpu.VMEM((n_slc, SEL_NQ * N_KV * Q_BLOCK), F32)],
        compiler_params=_params(2),
        name="compressed_attention_select",
    )(q, kvc, cbias, overlap_t)


def _selected_kernel(first_ref, q_ref, selneg_ref, gate_ref, ocmp_ref, owin_ref, kps_ref, tris_ref,
                     o_ref, qs_ref, s_ref, mt_ref, m_ref, lt_ref, acc_ref):
    qbs = [pl.program_id(1) * SLC_NQ + k for k in range(SLC_NQ)]
    rows = SLC_NQ * HEADS_PER_KV * Q_BLOCK
    heads = [_padded_heads(q_ref[0, k * Q_BLOCK:(k + 1) * Q_BLOCK, :]) for k in range(SLC_NQ)]
    for g in range(N_KV):
        qs = []
        for k in range(SLC_NQ):
            masked_out = selneg_ref[0, g, k * Q_BLOCK:(k + 1) * Q_BLOCK, :].astype(F32)
            qs += [jnp.concatenate([heads[k][hd], masked_out + _feature_rows(qbs[k], hd)[0]], axis=1)
                   for hd in range(g * HEADS_PER_KV, (g + 1) * HEADS_PER_KV)]
        qs_ref[g] = jnp.concatenate(qs, axis=0).astype(BF16)

    unit_blocks = SLC_UNIT // SLC_BLOCK
    diag_unit = qbs[0] // unit_blocks
    first = functools.reduce(jnp.minimum, [first_ref[pl.program_id(0), qb] for qb in qbs])
    first_unit = jnp.minimum(first, qbs[0]) // unit_blocks
    lead = jnp.minimum(first_unit, 1)
    n_units = diag_unit - first_unit + 1 + lead

    def unit_of(i):
        return jnp.where((i < lead) | (i >= n_units), 0, first_unit + i - lead)

    def unit_start(i):
        return pl.multiple_of(unit_of(i) * SLC_UNIT, SLC_UNIT)

    def slot_start(i):
        return pl.multiple_of(i * SLC_UNIT, SLC_UNIT)

    def trip_chains(first_slot, width):
        return [(g, first_slot + u) for u in range(width) for g in range(N_KV)]

    def score_matmul(g, i):
        return _dot_nt(qs_ref[g], kps_ref[0, g, pl.ds(unit_start(i), SLC_UNIT), :])

    def score_finish(g, i, raw):
        tri = []
        for qb in qbs:
            variant = jnp.where(i >= n_units, unit_blocks + 1,
                                jnp.where(unit_of(i) == diag_unit, qb % unit_blocks, unit_blocks))
            tri += [tris_ref[variant]] * HEADS_PER_KV
        s = (raw + jnp.concatenate(tri, axis=0)) * LOG2E
        s_ref[g, :, pl.ds(slot_start(i), SLC_UNIT)] = s
        mt_ref[g] = jnp.maximum(jnp.maximum(mt_ref[g], s[:, 0:LANES]), s[:, LANES:SLC_UNIT])

    def score_trip(first_slot, width):
        chains = trip_chains(first_slot, width)
        pending = score_matmul(*chains[0])
        for n, (g, i) in enumerate(chains):
            raw = pending
            if n + 1 < len(chains):
                pending = score_matmul(*chains[n + 1])
            score_finish(g, i, raw)

    def exponentials(g, i):
        m = m_ref[g]
        es = [jnp.exp2(s_ref[g, :, pl.ds(slot_start(i) + j * LANES, LANES)] - m) for j in range(SLC_UNIT // LANES)]
        lt_ref[g] = lt_ref[g] + functools.reduce(jnp.add, es)
        return jnp.concatenate(es, axis=1).astype(BF16)

    def value_trip(first_slot, width):
        chains = trip_chains(first_slot, width)
        pending = exponentials(*chains[0])
        for n, (g, i) in enumerate(chains):
            e = pending
            if n + 1 < len(chains):
                pending = exponentials(*chains[n + 1])
            acc_ref[g] = acc_ref[g] + _dot(e, kps_ref[0, g, pl.ds(unit_start(i), SLC_UNIT), 0:KV_PACK])

    def sweep(trip):
        done = 0
        for tier, width in enumerate(TRIP_WIDTHS):
            left = n_units - done
            if tier + 1 < len(TRIP_WIDTHS):
                count = left // width + jnp.where(left % width > TRIP_WIDTHS[tier + 1], 1, 0)
            else:
                count = (left + width - 1) // width
            count = jnp.maximum(count, 0)

            def body(p, carry, base=done, width=width):
                trip(base + p * width, width)
                return carry

            lax.fori_loop(0, count, body, 0)
            done = done + count * width

    mt_ref[...] = jnp.full(mt_ref.shape, M_INIT, F32)
    sweep(score_trip)
    for g in range(N_KV):
        m_ref[g] = jnp.broadcast_to(jnp.max(mt_ref[g], axis=-1, keepdims=True), (rows, LANES))
    lt_ref[...] = jnp.zeros(lt_ref.shape, F32)
    acc_ref[...] = jnp.zeros(acc_ref.shape, F32)
    sweep(value_trip)

    lane = lax.broadcasted_iota(jnp.int32, (Q_BLOCK, LANES), 1)
    for g in range(N_KV):
        l = jnp.sum(lt_ref[g], axis=-1, keepdims=True)
        o_rows = acc_ref[g] * jnp.where(l > 0.0, 1.0 / l, 0.0)
        for k in range(SLC_NQ):
            qr = slice(k * Q_BLOCK, (k + 1) * Q_BLOCK)
            o_slc = _pack_heads(o_rows[k * HEADS_PER_KV * Q_BLOCK:(k + 1) * HEADS_PER_KV * Q_BLOCK])
            gt = gate_ref[0, g, qr, :]
            for j in range(HEADS_PER_KV // 2):
                cols = slice((g * HEADS_PER_KV // 2 + j) * LANES, (g * HEADS_PER_KV // 2 + j + 1) * LANES)
                branches = (ocmp_ref[0, qr, cols].astype(F32), o_slc[:, j * LANES:(j + 1) * LANES],
                            owin_ref[0, qr, cols].astype(F32))
                total = None
                for c, branch in enumerate(branches):
                    even, odd = 3 * (2 * j) + c, 3 * (2 * j + 1) + c
                    gate = jnp.take_along_axis(gt, jnp.where(lane < HEAD_DIM, even, odd), axis=1)
                    total = gate * branch if total is None else total + gate * branch
                o_ref[0, qr, cols] = total.astype(BF16)


def _selected(first, q, selneg, gates, o_cmp, o_win, kps, tri_slc):
    bsz, s, _ = q.shape
    qt = SLC_NQ * Q_BLOCK
    rows = HEADS_PER_KV * qt
    assert (SLC_UNIT // SLC_BLOCK) % SLC_NQ == 0 and s % qt == 0

    def row(n):
        return pl.BlockSpec((1, qt, n), lambda b, i, first_ref: (b, i, 0))

    def grouped(n):
        return pl.BlockSpec((1, N_KV, qt, n), lambda b, i, first_ref: (b, 0, i, 0))

    grid_spec = pltpu.PrefetchScalarGridSpec(
        num_scalar_prefetch=1,
        grid=(bsz, s // qt),
        in_specs=[row(Q_WIDTH), grouped(LANES), grouped(LANES), row(Q_WIDTH), row(Q_WIDTH),
                  pl.BlockSpec((1, N_KV, s, KEY_WIDTH), lambda b, i, first_ref: (b, 0, 0, 0)),
                  _const_spec(tri_slc.shape)],
        out_specs=row(Q_WIDTH),
        scratch_shapes=[pltpu.VMEM((N_KV, rows, KEY_WIDTH), BF16),
                        pltpu.VMEM((N_KV, rows, s + SLC_UNIT), F32)]
        + [pltpu.VMEM((N_KV, rows, LANES), F32)] * 4)
    return pl.pallas_call(
        _selected_kernel,
        grid_spec=grid_spec,
        out_shape=jax.ShapeDtypeStruct((bsz, s, Q_WIDTH), BF16),
        compiler_params=_params(2),
        name="selected_attention",
    )(first, q, selneg, gates, o_cmp, o_win, kps, tri_slc)


def _attention(q, gates, kvc, kps, kpw):
    bsz, s, _ = q.shape
    r = kvc.shape[1]
    n_cmp = r - 1
    n_slc = s // SLC_BLOCK
    n_sel = min(N_SELECT, n_slc)
    rows = HEADS_PER_KV * Q_BLOCK
    assert n_slc <= SLC_BLOCK and n_slc % RANK_UNROLL == 0 and s % SLC_UNIT == 0 and s >= WIN_KEYS
    assert s % (WIN_NQ * Q_BLOCK) == 0 and s % (SEL_NQ * Q_BLOCK) == 0

    slopes = np.array([_alibi_slope(hd) for hd in range(N_HEADS)])
    slope_rows = np.repeat(slopes.reshape(N_KV, HEADS_PER_KV), Q_BLOCK, axis=1).reshape(N_KV, rows, 1)
    cbias = np.tile(slope_rows * (CMP_STRIDE * np.arange(r))[None, None, :], (1, SEL_NQ, 1))
    start = np.arange(r)[None, :] * CMP_STRIDE
    blk = np.arange(n_slc)[:, None] * SLC_BLOCK
    overlap_t = ((start < blk + SLC_BLOCK) & (start + CMP_BLOCK > blk) & (np.arange(r)[None, :] < n_cmp))

    ql = np.arange(Q_BLOCK)[:, None]
    kl = np.arange(SLC_BLOCK)[None, :]
    lower = np.where(kl > ql, -BIG, 0.0)
    upper = np.where(kl <= ql, -BIG, 0.0)
    unit_blocks = SLC_UNIT // SLC_BLOCK
    tri_slc = np.zeros((unit_blocks + 2, Q_BLOCK, SLC_UNIT))
    for j in range(unit_blocks):
        tri_slc[j, :, j * SLC_BLOCK:(j + 1) * SLC_BLOCK] = lower
    tri_slc[unit_blocks + 1] = -BIG
    win_blocks = WINDOW // SLC_BLOCK
    lead = WIN_KEYS // SLC_BLOCK - 1
    tri_win = np.zeros((lead + 1, Q_BLOCK, WIN_KEYS))
    for v in range(lead + 1):
        diag = v
        tri_win[v, :, diag * SLC_BLOCK:(diag + 1) * SLC_BLOCK] = lower
        if diag >= win_blocks:
            old = diag - win_blocks
            tri_win[v, :, old * SLC_BLOCK:(old + 1) * SLC_BLOCK] = upper

    o_win = _window(q, kpw, jnp.asarray(tri_win, F32))
    o_cmp, selneg, first = _select(q, kvc, jnp.asarray(cbias, F32), jnp.asarray(overlap_t, BF16), n_sel)
    return _selected(first[:, :, 0, 0], q, selneg, gates, o_cmp, o_win, kps, jnp.asarray(tri_slc, F32))


def _merge_ffn_kernel(x_ref, gate1_ref, uprev_ref, u_ref, o_ref, gbr_ref, pw_ref, ps_ref, wa_ref, wb_ref, wo_ref,
                      g2_ref, b2_ref, shift2_ref, scale2_ref, gate2_ref, g3_ref, b3_ref, w_in_ref, w_out_ref,
                      out_ref, ubuf_ref):
    i = pl.program_id(1)
    tm = u_ref.shape[1]
    ubuf_ref[0:POOL_HALO, :] = jnp.where(i == 0, 0.0, uprev_ref[0])
    ubuf_ref[POOL_HALO:POOL_HALO + tm, :] = u_ref[0]
    t = i * tm + lax.broadcasted_iota(jnp.int32, (tm, 1), 0)

    y_b = _dot(o_ref[0], wb_ref[...])
    mixed = []
    for gi, w in enumerate(POOL_WINDOWS):
        cs = slice(gi * POOL_GROUP, (gi + 1) * POOL_GROUP)
        cur = ubuf_ref[POOL_HALO:POOL_HALO + tm, cs]
        total = cur
        for k in range(1, w):
            total = total + ubuf_ref[POOL_HALO - k:POOL_HALO - k + tm, cs]
        inv_cnt = 1.0 / jnp.minimum(t + 1, w).astype(F32)
        delta = (total * inv_cnt - cur).astype(BF16)
        mixed.append((_dot(delta, pw_ref[gi]) * ps_ref[:, cs]).astype(BF16))
    y_a = _dot(jnp.concatenate(mixed, axis=1), wa_ref[...])
    d = y_a.shape[1]
    y = (gbr_ref[0, :, 0:d].astype(F32) * y_a + gbr_ref[0, :, d:2 * d].astype(F32) * y_b).astype(BF16)
    y = _dot(y, wo_ref[...])
    x2 = _layer_norm(ALPHA * x_ref[0] + gate1_ref[0] * y, g2_ref[...], b2_ref[...])
    out_ref[0] = _swiglu_block(x2, shift2_ref[0], scale2_ref[0], gate2_ref[0], w_in_ref, w_out_ref,
                               g3_ref[...], b3_ref[...])


def _merge_ffn(x, gate1, u, o, gbr, pool_w, pool_scale, w_a, w_b, w_o, g2, b2,
               shift2, scale2, gate2, g3, b3, w_in, w_out):
    bsz, s, d = x.shape
    tm = min(ROW_TILE, s)
    halo_blocks = tm // POOL_HALO

    def row(n):
        return pl.BlockSpec((1, tm, n), lambda bi, i: (bi, i, 0))

    mod = pl.BlockSpec((1, 1, d), lambda bi, i: (bi, 0, 0))
    vec = _const_spec((1, d))
    return pl.pallas_call(
        _merge_ffn_kernel,
        grid=(bsz, s // tm),
        in_specs=[row(d), mod,
                  pl.BlockSpec((1, POOL_HALO, POOL_WIDTH),
                               lambda bi, i: (bi, jnp.maximum(i * halo_blocks - 1, 0), 0)),
                  row(POOL_WIDTH), row(Q_WIDTH), row(2 * d),
                  _const_spec(pool_w.shape), _const_spec((1, POOL_WIDTH)),
                  _const_spec(w_a.shape), _const_spec(w_b.shape), _const_spec(w_o.shape),
                  vec, vec, mod, mod, mod, vec, vec, _const_spec(w_in.shape), _const_spec(w_out.shape)],
        out_specs=row(d),
        out_shape=jax.ShapeDtypeStruct((bsz, s, d), F32),
        scratch_shapes=[pltpu.VMEM((POOL_HALO + tm, POOL_WIDTH), F32)],
        compiler_params=_params(2),
        name="pool_merge_ffn",
    )(x, gate1, u, u, o, gbr, pool_w, pool_scale.reshape(1, POOL_WIDTH), w_a, w_b, w_o,
      g2.reshape(1, d), b2.reshape(1, d), shift2, scale2, gate2, g3.reshape(1, d), b3.reshape(1, d), w_in, w_out)


def _mixer_in_weights(w):
    sizes = (POOL_WIDTH, Q_WIDTH) + (KV_WIDTH,) * 6 + (3 * N_HEADS, 2 * D_MODEL)
    offs = np.concatenate([[0], np.cumsum(sizes)])
    u, q, k_cmp, v_cmp, k_slc, v_slc, k_win, v_win, g_nsa, g_br = [w[:, offs[i]:offs[i + 1]] for i in range(10)]
    cols = [u, q, k_cmp, v_cmp]
    for k, v in ((k_slc, v_slc), (k_win, v_win)):
        for g in range(N_KV):
            cols += [k[:, g * HEAD_DIM:(g + 1) * HEAD_DIM], v[:, g * HEAD_DIM:(g + 1) * HEAD_DIM]]
    per_g = 3 * HEADS_PER_KV
    for g in range(N_KV):
        cols += [g_nsa[:, g * per_g:(g + 1) * per_g], jnp.zeros((w.shape[0], LANES - per_g), w.dtype)]
    cols.append(g_br)
    return jnp.concatenate(cols, axis=1).astype(BF16)


def _compress_weights(pos, w1, w2, value_slot):
    same_group = np.eye(N_KV, dtype=np.float32)

    def expand_w1(half):
        wh = half.reshape(CMP_STRIDE, 1, HEAD_DIM, 1, CMP_HIDDEN)
        z = wh * same_group.reshape(1, N_KV, 1, N_KV, 1)
        return z.reshape(CMP_STRIDE * KV_WIDTH, N_KV * CMP_HIDDEN)

    half_rows = CMP_STRIDE * HEAD_DIM
    w1_big = jnp.stack([expand_w1(w1[:half_rows]), expand_w1(w1[half_rows:])]).astype(BF16)
    slot = np.eye(2, dtype=np.float32)[value_slot]
    w2_big = (w2.reshape(1, CMP_HIDDEN, 1, 1, HEAD_DIM) * same_group.reshape(N_KV, 1, N_KV, 1, 1)
              * slot.reshape(1, 1, 1, 2, 1))
    w2_big = w2_big.reshape(N_KV * CMP_HIDDEN, N_KV * KV_PACK).astype(BF16)
    pos_rows = jnp.broadcast_to(pos.reshape(2, CMP_STRIDE, 1, HEAD_DIM), (2, CMP_STRIDE, N_KV, HEAD_DIM))
    return pos_rows.reshape(2, CMP_STRIDE * KV_WIDTH), w1_big, w2_big


def kernel(x, c, ln_in_g, ln_in_b, w_ada, b_ada, ffn1_w_in, ffn1_w_out, ln1_g, ln1_b, w_mix_in, pool_w, pool_scale,
           cmp_pos_k, cmp_k_w1, cmp_k_w2, cmp_pos_v, cmp_v_w1, cmp_v_w2, w_branch_a, w_branch_b, w_mix_out,
           ln2_g, ln2_b, ffn2_w_in, ffn2_w_out, ln3_g, ln3_b):
    bsz, s, d = x.shape
    for l in range(DEPTH):
        ada = _ada(c, w_ada[l], b_ada[l]).reshape(bsz, 3, 3, 1, d)
        mod = lambda i, j: ada[:, i, j]

        x, u, q, kc, vc, kvs, kvw, gates, gbr = _ffn_mixin(
            x, mod(0, 0), mod(0, 1), mod(0, 2), ln_in_g, ln_in_b, ln1_g[l], ln1_b[l],
            ffn1_w_in[l].astype(BF16), ffn1_w_out[l].astype(BF16),
            mod(1, 0), mod(1, 1), _mixer_in_weights(w_mix_in[l]), pre_ln=l == 0)
        posk, w1k, w2k = _compress_weights(cmp_pos_k[l], cmp_k_w1[l], cmp_k_w2[l], 0)
        posv, w1v, w2v = _compress_weights(cmp_pos_v[l], cmp_v_w1[l], cmp_v_w2[l], 1)
        kvc = _compress(kc, vc, posk, posv, w1k, w1v, w2k, w2v)
        o = _attention(q, gates, kvc, kvs, kvw)
        x = _merge_ffn(x, mod(1, 2), u, o, gbr, pool_w[l].astype(BF16), pool_scale[l],
                       w_branch_a[l].astype(BF16), w_branch_b[l].astype(BF16), w_mix_out[l].astype(BF16),
                       ln2_g[l], ln2_b[l], mod(2, 0), mod(2, 1), mod(2, 2), ln3_g[l], ln3_b[l],
                       ffn2_w_in[l].astype(BF16), ffn2_w_out[l].astype(BF16))
    return x
```

```python
import functools

import numpy as np
import jax
import jax.numpy as jnp
from jax import lax
from jax.experimental import pallas as pl
from jax.experimental.pallas import tpu as pltpu

F32 = jnp.float32
BF16 = jnp.bfloat16

D_MODEL = 1024
POOL_WIDTH = D_MODEL // 2
POOL_WINDOWS = (2, 4, 8, 16)
POOL_GROUP = POOL_WIDTH // len(POOL_WINDOWS)
POOL_HALO = 16
HEAD_DIM = 64
N_HEADS = (D_MODEL // 2) // HEAD_DIM
N_KV = 2
HEADS_PER_KV = N_HEADS // N_KV
Q_WIDTH = N_HEADS * HEAD_DIM
KV_WIDTH = N_KV * HEAD_DIM
CMP_STRIDE = 16
CMP_BLOCK = 2 * CMP_STRIDE
CMP_HIDDEN = 4 * HEAD_DIM
SLC_BLOCK = 64
N_SELECT = 16
WINDOW = 512
Q_BLOCK = SLC_BLOCK
ALIBI_MAX = 8.0
D_FF = 2816
DEPTH = 1
ALPHA = (2.0 * DEPTH) ** 0.25
LN_EPS = 1e-5
NEG = -1e30
FORCE = 1e9

LANES = 128
KV_PACK = 2 * HEAD_DIM
KEY_WIDTH = 2 * KV_PACK
FEAT_OFFSET_LANE = SLC_BLOCK
BIG = 1e30
M_INIT = -3e38
FF_CHUNK = 256
GATE_CHUNK = 512
ROW_TILE = 512
SLC_UNIT = 256
RANK_UNROLL = 2
WIN_NQ = 4
WIN_AHEAD = 1
SEL_NQ = 2
SLC_NQ = 2
TRIP_WIDTHS = (6, 4, 2)
LOG2E = 1.4426950408889634
WIN_KEYS = WINDOW + 2 * Q_BLOCK
VMEM_LIMIT = 56 * 1024 * 1024

_C_U = 0
_C_Q = _C_U + POOL_WIDTH
_C_KC = _C_Q + Q_WIDTH
_C_VC = _C_KC + KV_WIDTH
_C_KVS = _C_VC + KV_WIDTH
_C_KVW = _C_KVS + N_KV * KV_PACK
_C_GN = _C_KVW + N_KV * KV_PACK
_C_GBR = _C_GN + N_KV * LANES
_C_END = _C_GBR + 2 * D_MODEL


def _dot(a, b):
    return jnp.dot(a, b, preferred_element_type=F32)


def _dot_nt(a, b):
    return lax.dot_general(a, b, (((1,), (1,)), ((), ())), preferred_element_type=F32)


def _layer_norm(x, g, b):
    mu = jnp.mean(x, axis=-1, keepdims=True)
    xc = x - mu
    var = jnp.mean(xc * xc, axis=-1, keepdims=True)
    return xc * lax.rsqrt(var + LN_EPS) * g + b


def _masked_softmax(s, mask):
    sm = jnp.where(mask, s, NEG)
    m = jnp.max(sm, axis=-1, keepdims=True)
    e = jnp.exp(sm - m)
    p = e * (1.0 / jnp.sum(e, axis=-1, keepdims=True))
    return jnp.where(mask, p, 0.0)


def _const_spec(shape):
    nd = len(shape)
    return pl.BlockSpec(shape, lambda *_: (0,) * nd, pipeline_mode=pl.Buffered(1))


def _params(n_grid):
    return pltpu.CompilerParams(dimension_semantics=("parallel",) * n_grid, vmem_limit_bytes=VMEM_LIMIT)


def _ada_kernel(c_ref, w_ref, b_ref, o_ref):
    c = c_ref[...]
    c_act = (c * jax.nn.sigmoid(c)).astype(BF16)
    o_ref[...] = _dot(c_act, w_ref[...].astype(BF16)) + b_ref[...]


def _ada(c, w, b):
    bsz, d = c.shape
    n = w.shape[1]
    tn = D_MODEL
    return pl.pallas_call(
        _ada_kernel,
        grid=(n // tn,),
        in_specs=[pl.BlockSpec((bsz, d), lambda j: (0, 0)),
                  pl.BlockSpec((d, tn), lambda j: (0, j)),
                  pl.BlockSpec((1, tn), lambda j: (0, j))],
        out_specs=pl.BlockSpec((bsz, tn), lambda j: (0, j)),
        out_shape=jax.ShapeDtypeStruct((bsz, n), F32),
        compiler_params=_params(1),
        name="ada_proj",
    )(c, w, b.reshape(1, n))


def _swiglu_block(x, shift, scale, gate, w_in_ref, w_out_ref, g, b):
    h = (x * (1.0 + scale) + shift).astype(BF16)
    acc = jnp.zeros(x.shape, F32)
    for j in range(D_FF // FF_CHUNK):
        c0 = j * FF_CHUNK
        gt = _dot(h, w_in_ref[:, c0:c0 + FF_CHUNK])
        up = _dot(h, w_in_ref[:, D_FF + c0:D_FF + c0 + FF_CHUNK])
        act = (gt * jax.nn.sigmoid(gt) * up).astype(BF16)
        acc = acc + _dot(act, w_out_ref[c0:c0 + FF_CHUNK, :])
    return _layer_norm(ALPHA * x + 0.5 * gate * acc, g, b)


def _ffn_mixin_kernel(x_ref, shift0_ref, scale0_ref, gate0_ref, g0_ref, b0_ref, g1_ref, b1_ref, w_in_ref, w_out_ref,
                      shift1_ref, scale1_ref, w_ref,
                      x1_ref, u_ref, q_ref, kc_ref, vc_ref, kvs_ref, kvw_ref, gn_ref, gbr_ref, *, pre_ln):
    x = x_ref[0]
    if pre_ln:
        x = _layer_norm(x, g0_ref[...], b0_ref[...])
    x1 = _swiglu_block(x, shift0_ref[0], scale0_ref[0], gate0_ref[0], w_in_ref, w_out_ref, g1_ref[...], b1_ref[...])
    x1_ref[0] = x1

    h = (x1 * (1.0 + scale1_ref[0]) + shift1_ref[0]).astype(BF16)

    def proj(c0, c1):
        return _dot(h, w_ref[:, c0:c1])

    tm = x_ref.shape[1]
    pos = pl.program_id(1) * tm + lax.broadcasted_iota(jnp.int32, (tm, LANES), 0)
    lane = lax.broadcasted_iota(jnp.int32, (tm, LANES), 1)
    feat = jnp.where(lane == pos // SLC_BLOCK, 1.0,
                     jnp.where(lane == FEAT_OFFSET_LANE, (pos % SLC_BLOCK).astype(F32), 0.0)).astype(BF16)

    def put_u(z):
        u_ref[0] = z

    def put_q(z):
        q_ref[0] = (z * HEAD_DIM ** -0.5).astype(BF16)

    def put_cmp(z):
        kc_ref[0] = z[:, 0:KV_WIDTH]
        vc_ref[0] = z[:, KV_WIDTH:2 * KV_WIDTH]

    def put_keys(ref):
        def put(z):
            zb = z.astype(BF16)
            for g in range(N_KV):
                ref[0, g, :, 0:KV_PACK] = zb[:, g * KV_PACK:(g + 1) * KV_PACK]
                ref[0, g, :, KV_PACK:KEY_WIDTH] = feat
        return put

    def put_head_gates(z):
        sg = jax.nn.sigmoid(z)
        for g in range(N_KV):
            gn_ref[0, g] = sg[:, g * LANES:(g + 1) * LANES]

    def put_branch_gates(c0):
        def put(z):
            gbr_ref[0, :, c0 - _C_GBR:c0 - _C_GBR + z.shape[1]] = jax.nn.sigmoid(z).astype(BF16)
        return put

    gate_cols = [(c0, c0 + GATE_CHUNK) for c0 in range(_C_GBR, _C_END, GATE_CHUNK)]
    light = [((_C_U, _C_Q), put_u), ((_C_Q, _C_KC), put_q), ((_C_KC, _C_KVS), put_cmp),
             ((_C_KVS, _C_KVW), put_keys(kvs_ref)), ((_C_KVW, _C_GN), put_keys(kvw_ref)),
             ((_C_GN, _C_GBR), put_head_gates)]
    stages = []
    for n in range(max(len(gate_cols), len(light))):
        if n < len(gate_cols):
            stages.append((gate_cols[n], put_branch_gates(gate_cols[n][0])))
        if n < len(light):
            stages.append(light[n])
    pending = proj(*stages[0][0])
    for n, (_, sink) in enumerate(stages):
        z = pending
        if n + 1 < len(stages):
            pending = proj(*stages[n + 1][0])
        sink(z)


def _ffn_mixin(x, shift0, scale0, gate0, g0, b0, g1, b1, w_in, w_out, shift1, scale1, w_mix, *, pre_ln):
    bsz, s, d = x.shape
    tm = min(ROW_TILE, s)

    def row(n, dt):
        return pl.BlockSpec((1, tm, n), lambda b, i: (b, i, 0)), jax.ShapeDtypeStruct((bsz, s, n), dt)

    def grouped(n, dt):
        return (pl.BlockSpec((1, N_KV, tm, n), lambda b, i: (b, 0, i, 0)),
                jax.ShapeDtypeStruct((bsz, N_KV, s, n), dt))

    outs = [row(d, F32), row(POOL_WIDTH, F32), row(Q_WIDTH, BF16), row(KV_WIDTH, F32), row(KV_WIDTH, F32),
            grouped(KEY_WIDTH, BF16), grouped(KEY_WIDTH, BF16), grouped(LANES, F32), row(2 * D_MODEL, BF16)]
    mod = pl.BlockSpec((1, 1, d), lambda b, i: (b, 0, 0))
    vec = _const_spec((1, d))
    return pl.pallas_call(
        functools.partial(_ffn_mixin_kernel, pre_ln=pre_ln),
        grid=(bsz, s // tm),
        in_specs=[row(d, F32)[0], mod, mod, mod, vec, vec, vec, vec,
                  _const_spec(w_in.shape), _const_spec(w_out.shape), mod, mod, _const_spec(w_mix.shape)],
        out_specs=[o[0] for o in outs],
        out_shape=[o[1] for o in outs],
        compiler_params=_params(2),
        name="ffn_mixer_in",
    )(x, shift0, scale0, gate0, g0.reshape(1, d), b0.reshape(1, d), g1.reshape(1, d), b1.reshape(1, d),
      w_in, w_out, shift1, scale1, w_mix)


def _compress_kernel(k_ref, v_ref, posk_ref, posv_ref, w1k_ref, w1v_ref, w2k_ref, w2v_ref, o_ref, nxt_ref):
    r = k_ref.shape[1] // CMP_STRIDE

    def hidden(x_ref, pos_ref, w1_ref):
        x = jnp.concatenate([x_ref[0, pl.ds(j, r, stride=CMP_STRIDE), :] for j in range(CMP_STRIDE)], axis=1)
        first = _dot((x + pos_ref[0:1, :]).astype(BF16), w1_ref[0])
        nxt_ref[0:r, :] = _dot((x + pos_ref[1:2, :]).astype(BF16), w1_ref[1])
        nxt_ref[r:r + 8, :] = jnp.zeros((8, nxt_ref.shape[1]), F32)
        return jax.nn.gelu(first + nxt_ref[1:r + 1, :]).astype(BF16)

    hk = hidden(k_ref, posk_ref, w1k_ref)
    hv = hidden(v_ref, posv_ref, w1v_ref)
    o_ref[0] = (_dot(hk, w2k_ref[...]) + _dot(hv, w2v_ref[...])).astype(BF16)


def _compress(kc, vc, posk, posv, w1k, w1v, w2k, w2v):
    bsz, s, n = kc.shape
    r = s // CMP_STRIDE
    hid = w1k.shape[2]
    row = pl.BlockSpec((1, s, n), lambda b: (b, 0, 0))
    return pl.pallas_call(
        _compress_kernel,
        grid=(bsz,),
        in_specs=[row, row, _const_spec(posk.shape), _const_spec(posv.shape),
                  _const_spec(w1k.shape), _const_spec(w1v.shape),
                  _const_spec(w2k.shape), _const_spec(w2v.shape)],
        out_specs=pl.BlockSpec((1, r, N_KV * KV_PACK), lambda b: (b, 0, 0)),
        out_shape=jax.ShapeDtypeStruct((bsz, r, N_KV * KV_PACK), BF16),
        scratch_shapes=[pltpu.VMEM((r + 8, hid), F32)],
        compiler_params=_params(1),
        name="compress_mlp",
    )(kc, vc, posk, posv, w1k, w1v, w2k, w2v)


def _alibi_slope(head):
    return 2.0 ** (-ALIBI_MAX * (head + 1) / N_HEADS)


def _padded_heads(q_tile):
    lane = lax.broadcasted_iota(jnp.int32, (Q_BLOCK, LANES), 1)
    qf = q_tile.astype(F32)
    heads = []
    for hd in range(N_HEADS):
        slab = qf[:, (hd // 2) * LANES:(hd // 2 + 1) * LANES]
        if hd % 2:
            slab = pltpu.roll(slab, HEAD_DIM, 1)
        heads.append(jnp.where(lane < HEAD_DIM, slab, 0.0))
    return heads


def _pack_heads(o_rows):
    lane = lax.broadcasted_iota(jnp.int32, (Q_BLOCK, LANES), 1)
    pairs = []
    for j in range(HEADS_PER_KV // 2):
        even = o_rows[(2 * j) * Q_BLOCK:(2 * j + 1) * Q_BLOCK]
        odd = o_rows[(2 * j + 1) * Q_BLOCK:(2 * j + 2) * Q_BLOCK]
        pairs.append(jnp.where(lane < HEAD_DIM, pltpu.roll(even, HEAD_DIM, 1), odd))
    return jnp.concatenate(pairs, axis=1)


def _feature_rows(qb, hd):
    lane_row = lax.broadcasted_iota(jnp.int32, (1, LANES), 1)
    slope = _alibi_slope(hd)
    tail = jnp.where(lane_row == FEAT_OFFSET_LANE, slope, 0.0)
    block_bias = slope * SLC_BLOCK * (lane_row - qb).astype(F32)
    in_window = (lane_row >= qb - WINDOW // SLC_BLOCK) & (lane_row <= qb)
    slc_row = jnp.where(lane_row < SLC_BLOCK, block_bias, tail)
    win_row = jnp.where(lane_row < SLC_BLOCK, jnp.where(in_window, block_bias, -BIG), tail)
    return slc_row, win_row


def _window_kernel(q_ref, kpw_ref, triw_ref, o_ref):
    gw = HEADS_PER_KV * HEAD_DIM
    chains = [(k, g) for k in range(WIN_NQ) for g in range(N_KV)]

    def scores(k, g):
        qb = pl.program_id(1) * WIN_NQ + k
        heads = _padded_heads(q_ref[0, k * Q_BLOCK:(k + 1) * Q_BLOCK, :])
        w0 = pl.multiple_of(jnp.maximum((qb + 1) * Q_BLOCK - WIN_KEYS, 0), Q_BLOCK)
        tri = jnp.concatenate([triw_ref[jnp.minimum(qb, WIN_KEYS // SLC_BLOCK - 1)]] * HEADS_PER_KV, axis=0)
        qw = [jnp.concatenate([heads[hd], jnp.broadcast_to(_feature_rows(qb, hd)[1], (Q_BLOCK, LANES))], axis=1)
              for hd in range(g * HEADS_PER_KV, (g + 1) * HEADS_PER_KV)]
        kp = kpw_ref[0, g, pl.ds(w0, WIN_KEYS), :]
        return _dot_nt(jnp.concatenate(qw, axis=0).astype(BF16), kp) + tri, kp

    pending = [scores(*chains[n]) for n in range(WIN_AHEAD)]
    for n, (k, g) in enumerate(chains):
        s, kp = pending.pop(0)
        if n + WIN_AHEAD < len(chains):
            pending.append(scores(*chains[n + WIN_AHEAD]))
        e = jnp.exp(s - jnp.max(s, axis=-1, keepdims=True))
        o = _dot(e.astype(BF16), kp[:, 0:KV_PACK]) * (1.0 / jnp.sum(e, axis=-1, keepdims=True))
        o_ref[0, k * Q_BLOCK:(k + 1) * Q_BLOCK, g * gw:(g + 1) * gw] = _pack_heads(o).astype(BF16)


def _window(q, kpw, tri_win):
    bsz, s, _ = q.shape
    qt = WIN_NQ * Q_BLOCK
    return pl.pallas_call(
        _window_kernel,
        grid=(bsz, s // qt),
        in_specs=[pl.BlockSpec((1, qt, Q_WIDTH), lambda b, i: (b, i, 0)),
                  pl.BlockSpec((1, N_KV, s, KEY_WIDTH), lambda b, i: (b, 0, 0, 0)),
                  _const_spec(tri_win.shape)],
        out_specs=pl.BlockSpec((1, qt, Q_WIDTH), lambda b, i: (b, i, 0)),
        out_shape=jax.ShapeDtypeStruct((bsz, s, Q_WIDTH), BF16),
        compiler_params=_params(2),
        name="window_attention",
    )(q, kpw, tri_win)


def _select_kernel(q_ref, kvc_ref, cbias_ref, ovl_ref, ocmp_ref, selneg_ref, first_ref, score_ref, *, n_sel):
    step = pl.program_id(1)
    rows = SEL_NQ * HEADS_PER_KV * Q_BLOCK
    n_slc = ovl_ref.shape[0]
    r = kvc_ref.shape[1]
    lanes_q = SEL_NQ * N_KV * Q_BLOCK
    heads = [_padded_heads(q_ref[0, k * Q_BLOCK:(k + 1) * Q_BLOCK, :]) for k in range(SEL_NQ)]

    row = lax.broadcasted_iota(jnp.int32, (rows, 1), 0)
    t = (step * SEL_NQ + row // (HEADS_PER_KV * Q_BLOCK)) * Q_BLOCK + (row & (Q_BLOCK - 1))
    last_cmp = (t - (CMP_BLOCK - 1)) >> 4
    visible = lax.broadcasted_iota(jnp.int32, (rows, r), 1) <= last_cmp
    p_sums = [[None] * N_KV for _ in range(SEL_NQ)]
    gw = HEADS_PER_KV * HEAD_DIM
    def cmp_scores(g):
        q_pad = jnp.concatenate([heads[k][g * HEADS_PER_KV + h] for k in range(SEL_NQ)
                                 for h in range(HEADS_PER_KV)], axis=0).astype(BF16)
        return _dot_nt(q_pad, kvc_ref[0, :, g * KV_PACK:(g + 1) * KV_PACK])

    raw = [cmp_scores(g) for g in range(N_KV)]
    for g in range(N_KV):
        kvc = kvc_ref[0, :, g * KV_PACK:(g + 1) * KV_PACK]
        p_cmp = _masked_softmax(raw[g] + cbias_ref[g], visible)
        o_cmp = _dot(p_cmp.astype(BF16), kvc)
        for k in range(SEL_NQ):
            base = k * HEADS_PER_KV * Q_BLOCK
            ocmp_ref[0, k * Q_BLOCK:(k + 1) * Q_BLOCK, g * gw:(g + 1) * gw] = _pack_heads(
                o_cmp[base:base + HEADS_PER_KV * Q_BLOCK]).astype(BF16)
            p_sum = p_cmp[base:base + Q_BLOCK]
            for h in range(1, HEADS_PER_KV):
                p_sum = p_sum + p_cmp[base + h * Q_BLOCK:base + (h + 1) * Q_BLOCK]
            p_sums[k][g] = p_sum

    p_all = jnp.concatenate([p_sums[k][g] for k in range(SEL_NQ) for g in range(N_KV)], axis=0)
    p_hi = p_all.astype(BF16)
    p_lo = (p_all - p_hi.astype(F32)).astype(BF16)
    ovl = ovl_ref[...]
    imp_t = _dot_nt(ovl, p_hi) + _dot_nt(ovl, p_lo)

    blk = lax.broadcasted_iota(jnp.int32, (n_slc, lanes_q), 0)
    qb = step * SEL_NQ + lax.broadcasted_iota(jnp.int32, (n_slc, lanes_q), 1) // (N_KV * Q_BLOCK)
    forced = (blk == 0) | (blk == qb) | (blk == qb - 1)
    score = jnp.where(blk > qb, NEG, jnp.where(forced, FORCE, jnp.where(imp_t > 0.0, imp_t, 0.0)))
    key = lax.bitcast_convert_type(score, jnp.int32)
    key_prev = key - 1
    score_ref[...] = key

    def rank_step(i, ranks):
        out = []
        for u in range(RANK_UNROLL):
            row_i = i * RANK_UNROLL + u
            ri = score_ref[pl.ds(row_i, 1), :]
            out.append(ranks[u] + jnp.where(ri > jnp.where(blk > row_i, key_prev, key), 1.0, 0.0))
        return tuple(out)

    ranks = lax.fori_loop(0, (step * SEL_NQ + SEL_NQ - 1) // RANK_UNROLL + 1, rank_step,
                          (jnp.zeros((n_slc, lanes_q), F32),) * RANK_UNROLL)
    chosen = (functools.reduce(jnp.add, ranks) < n_sel) & (blk <= qb)

    oldest = jnp.where(chosen & (blk >= 1), blk, n_slc).astype(F32)
    sel_t = jnp.where(chosen, 1.0, 0.0).astype(BF16)
    sel_t = jnp.concatenate([sel_t, jnp.zeros((LANES - n_slc, lanes_q), BF16)], axis=0)
    eye = jnp.where(lax.broadcasted_iota(jnp.int32, (lanes_q, lanes_q), 0)
                    == lax.broadcasted_iota(jnp.int32, (lanes_q, lanes_q), 1), 1.0, 0.0).astype(BF16)
    sel = _dot_nt(eye, sel_t)
    lane = lax.broadcasted_iota(jnp.int32, (Q_BLOCK, LANES), 1)
    for k in range(SEL_NQ):
        lanes_k = slice(k * N_KV * Q_BLOCK, (k + 1) * N_KV * Q_BLOCK)
        first = jnp.min(jnp.min(oldest[:, lanes_k], axis=1, keepdims=True), axis=0, keepdims=True)
        first_ref[0, k] = jnp.broadcast_to(first, first_ref.shape[2:]).astype(jnp.int32)
        for g in range(N_KV):
            base = (k * N_KV + g) * Q_BLOCK
            selneg_ref[0, g, k * Q_BLOCK:(k + 1) * Q_BLOCK, :] = jnp.where(
                lane < SLC_BLOCK, (sel[base:base + Q_BLOCK] - 1.0) * BIG, 0.0).astype(BF16)


def _select(q, kvc, cbias, overlap_t, n_sel):
    bsz, s, _ = q.shape
    r = kvc.shape[1]
    n_slc = overlap_t.shape[0]
    qt = SEL_NQ * Q_BLOCK
    return pl.pallas_call(
        functools.partial(_select_kernel, n_sel=n_sel),
        grid=(bsz, s // qt),
        in_specs=[pl.BlockSpec((1, qt, Q_WIDTH), lambda b, i: (b, i, 0)),
                  pl.BlockSpec((1, r, N_KV * KV_PACK), lambda b, i: (b, 0, 0)),
                  _const_spec(cbias.shape), _const_spec(overlap_t.shape)],
        out_specs=[pl.BlockSpec((1, qt, Q_WIDTH), lambda b, i: (b, i, 0)),
                   pl.BlockSpec((1, N_KV, qt, LANES), lambda b, i: (b, 0, i, 0)),
                   pl.BlockSpec((1, SEL_NQ, 8, LANES), lambda b, i: (b, i, 0, 0))],
        out_shape=[jax.ShapeDtypeStruct((bsz, s, Q_WIDTH), BF16),
                   jax.ShapeDtypeStruct((bsz, N_KV, s, LANES), BF16),
                   jax.ShapeDtypeStruct((bsz, s // Q_BLOCK, 8, LANES), jnp.int32)],
        scratch_shapes=[pltpu.VMEM((n_slc, SEL_NQ * N_KV * Q_BLOCK), jnp.int32)],
        compiler_params=_params(2),
        name="compressed_attention_select",
    )(q, kvc, cbias, overlap_t)


def _selected_kernel(first_ref, q_ref, selneg_ref, gate_ref, ocmp_ref, owin_ref, kps_ref, tris_ref,
                     o_ref, qs_ref, s_ref, mt_ref, m_ref, lt_ref, acc_ref):
    qbs = [pl.program_id(1) * SLC_NQ + k for k in range(SLC_NQ)]
    rows = SLC_NQ * HEADS_PER_KV * Q_BLOCK
    heads = [_padded_heads(q_ref[0, k * Q_BLOCK:(k + 1) * Q_BLOCK, :]) for k in range(SLC_NQ)]
    for g in range(N_KV):
        qs = []
        for k in range(SLC_NQ):
            masked_out = selneg_ref[0, g, k * Q_BLOCK:(k + 1) * Q_BLOCK, :].astype(F32)
            qs += [jnp.concatenate([heads[k][hd], masked_out + _feature_rows(qbs[k], hd)[0]], axis=1)
                   for hd in range(g * HEADS_PER_KV, (g + 1) * HEADS_PER_KV)]
        qs_ref[g] = jnp.concatenate(qs, axis=0).astype(BF16)

    unit_blocks = SLC_UNIT // SLC_BLOCK
    diag_unit = qbs[0] // unit_blocks
    first = functools.reduce(jnp.minimum, [first_ref[pl.program_id(0), qb] for qb in qbs])
    first_unit = jnp.minimum(first, qbs[0]) // unit_blocks
    lead = jnp.minimum(first_unit, 1)
    n_units = diag_unit - first_unit + 1 + lead

    def unit_of(i):
        return jnp.where((i < lead) | (i >= n_units), 0, first_unit + i - lead)

    def unit_start(i):
        return pl.multiple_of(unit_of(i) * SLC_UNIT, SLC_UNIT)

    def slot_start(i):
        return pl.multiple_of(i * SLC_UNIT, SLC_UNIT)

    def trip_chains(first_slot, width):
        return [(g, first_slot + u) for u in range(width) for g in range(N_KV)]

    def score_matmul(g, i):
        return _dot_nt(qs_ref[g], kps_ref[0, g, pl.ds(unit_start(i), SLC_UNIT), :])

    def score_finish(g, i, raw):
        tri = []
        for qb in qbs:
            variant = jnp.where(i >= n_units, unit_blocks + 1,
                                jnp.where(unit_of(i) == diag_unit, qb % unit_blocks, unit_blocks))
            tri += [tris_ref[variant]] * HEADS_PER_KV
        s = (raw + jnp.concatenate(tri, axis=0)) * LOG2E
        s_ref[g, :, pl.ds(slot_start(i), SLC_UNIT)] = s
        mt_ref[g] = jnp.maximum(jnp.maximum(mt_ref[g], s[:, 0:LANES]), s[:, LANES:SLC_UNIT])

    def score_trip(first_slot, width):
        chains = trip_chains(first_slot, width)
        pending = score_matmul(*chains[0])
        for n, (g, i) in enumerate(chains):
            raw = pending
            if n + 1 < len(chains):
                pending = score_matmul(*chains[n + 1])
            score_finish(g, i, raw)

    def exponentials(g, i):
        m = m_ref[g]
        es = [jnp.exp2(s_ref[g, :, pl.ds(slot_start(i) + j * LANES, LANES)] - m) for j in range(SLC_UNIT // LANES)]
        lt_ref[g] = lt_ref[g] + functools.reduce(jnp.add, es)
        return jnp.concatenate(es, axis=1).astype(BF16)

    def value_trip(first_slot, width):
        chains = trip_chains(first_slot, width)
        pending = exponentials(*chains[0])
        for n, (g, i) in enumerate(chains):
            e = pending
            if n + 1 < len(chains):
                pending = exponentials(*chains[n + 1])
            acc_ref[g] = acc_ref[g] + _dot(e, kps_ref[0, g, pl.ds(unit_start(i), SLC_UNIT), 0:KV_PACK])

    def sweep(trip):
        done = 0
        for tier, width in enumerate(TRIP_WIDTHS):
            left = n_units - done
            if tier + 1 < len(TRIP_WIDTHS):
                count = left // width + jnp.where(left % width > TRIP_WIDTHS[tier + 1], 1, 0)
            else:
                count = (left + width - 1) // width
            count = jnp.maximum(count, 0)

            def body(p, carry, base=done, width=width):
                trip(base + p * width, width)
                return carry

            lax.fori_loop(0, count, body, 0)
            done = done + count * width

    mt_ref[...] = jnp.full(mt_ref.shape, M_INIT, F32)
    sweep(score_trip)
    for g in range(N_KV):
        m_ref[g] = jnp.broadcast_to(jnp.max(mt_ref[g], axis=-1, keepdims=True), (rows, LANES))
    lt_ref[...] = jnp.zeros(lt_ref.shape, F32)
    acc_ref[...] = jnp.zeros(acc_ref.shape, F32)
    sweep(value_trip)

    lane = lax.broadcasted_iota(jnp.int32, (Q_BLOCK, LANES), 1)
    for g in range(N_KV):
        l = jnp.sum(lt_ref[g], axis=-1, keepdims=True)
        o_rows = acc_ref[g] * jnp.where(l > 0.0, 1.0 / l, 0.0)
        for k in range(SLC_NQ):
            qr = slice(k * Q_BLOCK, (k + 1) * Q_BLOCK)
            o_slc = _pack_heads(o_rows[k * HEADS_PER_KV * Q_BLOCK:(k + 1) * HEADS_PER_KV * Q_BLOCK])
            gt = gate_ref[0, g, qr, :]
            for j in range(HEADS_PER_KV // 2):
                cols = slice((g * HEADS_PER_KV // 2 + j) * LANES, (g * HEADS_PER_KV // 2 + j + 1) * LANES)
                branches = (ocmp_ref[0, qr, cols].astype(F32), o_slc[:, j * LANES:(j + 1) * LANES],
                            owin_ref[0, qr, cols].astype(F32))
                total = None
                for c, branch in enumerate(branches):
                    even, odd = 3 * (2 * j) + c, 3 * (2 * j + 1) + c
                    gate = jnp.take_along_axis(gt, jnp.where(lane < HEAD_DIM, even, odd), axis=1)
                    total = gate * branch if total is None else total + gate * branch
                o_ref[0, qr, cols] = total.astype(BF16)


def _selected(first, q, selneg, gates, o_cmp, o_win, kps, tri_slc):
    bsz, s, _ = q.shape
    qt = SLC_NQ * Q_BLOCK
    rows = HEADS_PER_KV * qt
    assert (SLC_UNIT // SLC_BLOCK) % SLC_NQ == 0 and s % qt == 0

    def row(n):
        return pl.BlockSpec((1, qt, n), lambda b, i, first_ref: (b, i, 0))

    def grouped(n):
        return pl.BlockSpec((1, N_KV, qt, n), lambda b, i, first_ref: (b, 0, i, 0))

    grid_spec = pltpu.PrefetchScalarGridSpec(
        num_scalar_prefetch=1,
        grid=(bsz, s // qt),
        in_specs=[row(Q_WIDTH), grouped(LANES), grouped(LANES), row(Q_WIDTH), row(Q_WIDTH),
                  pl.BlockSpec((1, N_KV, s, KEY_WIDTH), lambda b, i, first_ref: (b, 0, 0, 0)),
                  _const_spec(tri_slc.shape)],
        out_specs=row(Q_WIDTH),
        scratch_shapes=[pltpu.VMEM((N_KV, rows, KEY_WIDTH), BF16),
                        pltpu.VMEM((N_KV, rows, s + SLC_UNIT), F32)]
        + [pltpu.VMEM((N_KV, rows, LANES), F32)] * 4)
    return pl.pallas_call(
        _selected_kernel,
        grid_spec=grid_spec,
        out_shape=jax.ShapeDtypeStruct((bsz, s, Q_WIDTH), BF16),
        compiler_params=_params(2),
        name="selected_attention",
    )(first, q, selneg, gates, o_cmp, o_win, kps, tri_slc)


def _attention(q, gates, kvc, kps, kpw):
    bsz, s, _ = q.shape
    r = kvc.shape[1]
    n_cmp = r - 1
    n_slc = s // SLC_BLOCK
    n_sel = min(N_SELECT, n_slc)
    rows = HEADS_PER_KV * Q_BLOCK
    assert n_slc <= SLC_BLOCK and n_slc % RANK_UNROLL == 0 and s % SLC_UNIT == 0 and s >= WIN_KEYS
    assert s % (WIN_NQ * Q_BLOCK) == 0 and s % (SEL_NQ * Q_BLOCK) == 0

    slopes = np.array([_alibi_slope(hd) for hd in range(N_HEADS)])
    slope_rows = np.repeat(slopes.reshape(N_KV, HEADS_PER_KV), Q_BLOCK, axis=1).reshape(N_KV, rows, 1)
    cbias = np.tile(slope_rows * (CMP_STRIDE * np.arange(r))[None, None, :], (1, SEL_NQ, 1))
    start = np.arange(r)[None, :] * CMP_STRIDE
    blk = np.arange(n_slc)[:, None] * SLC_BLOCK
    overlap_t = ((start < blk + SLC_BLOCK) & (start + CMP_BLOCK > blk) & (np.arange(r)[None, :] < n_cmp))

    ql = np.arange(Q_BLOCK)[:, None]
    kl = np.arange(SLC_BLOCK)[None, :]
    lower = np.where(kl > ql, -BIG, 0.0)
    upper = np.where(kl <= ql, -BIG, 0.0)
    unit_blocks = SLC_UNIT // SLC_BLOCK
    tri_slc = np.zeros((unit_blocks + 2, Q_BLOCK, SLC_UNIT))
    for j in range(unit_blocks):
        tri_slc[j, :, j * SLC_BLOCK:(j + 1) * SLC_BLOCK] = lower
    tri_slc[unit_blocks + 1] = -BIG
    win_blocks = WINDOW // SLC_BLOCK
    lead = WIN_KEYS // SLC_BLOCK - 1
    tri_win = np.zeros((lead + 1, Q_BLOCK, WIN_KEYS))
    for v in range(lead + 1):
        diag = v
        tri_win[v, :, diag * SLC_BLOCK:(diag + 1) * SLC_BLOCK] = lower
        if diag >= win_blocks:
            old = diag - win_blocks
            tri_win[v, :, old * SLC_BLOCK:(old + 1) * SLC_BLOCK] = upper

    o_win = _window(q, kpw, jnp.asarray(tri_win, F32))
    o_cmp, selneg, first = _select(q, kvc, jnp.asarray(cbias, F32), jnp.asarray(overlap_t, BF16), n_sel)
    return _selected(first[:, :, 0, 0], q, selneg, gates, o_cmp, o_win, kps, jnp.asarray(tri_slc, F32))


def _merge_ffn_kernel(x_ref, gate1_ref, uprev_ref, u_ref, o_ref, gbr_ref, pw_ref, ps_ref, wa_ref, wb_ref, wo_ref,
                      g2_ref, b2_ref, shift2_ref, scale2_ref, gate2_ref, g3_ref, b3_ref, w_in_ref, w_out_ref,
                      out_ref, ubuf_ref):
    i = pl.program_id(1)
    tm = u_ref.shape[1]
    ubuf_ref[0:POOL_HALO, :] = jnp.where(i == 0, 0.0, uprev_ref[0])
    ubuf_ref[POOL_HALO:POOL_HALO + tm, :] = u_ref[0]
    t = i * tm + lax.broadcasted_iota(jnp.int32, (tm, 1), 0)

    y_b = _dot(o_ref[0], wb_ref[...])
    mixed = []
    for gi, w in enumerate(POOL_WINDOWS):
        cs = slice(gi * POOL_GROUP, (gi + 1) * POOL_GROUP)
        cur = ubuf_ref[POOL_HALO:POOL_HALO + tm, cs]
        total = cur
        for k in range(1, w):
            total = total + ubuf_ref[POOL_HALO - k:POOL_HALO - k + tm, cs]
        inv_cnt = 1.0 / jnp.minimum(t + 1, w).astype(F32)
        delta = (total * inv_cnt - cur).astype(BF16)
        mixed.append((_dot(delta, pw_ref[gi]) * ps_ref[:, cs]).astype(BF16))
    y_a = _dot(jnp.concatenate(mixed, axis=1), wa_ref[...])
    d = y_a.shape[1]
    y = (gbr_ref[0, :, 0:d].astype(F32) * y_a + gbr_ref[0, :, d:2 * d].astype(F32) * y_b).astype(BF16)
    y = _dot(y, wo_ref[...])
    x2 = _layer_norm(ALPHA * x_ref[0] + gate1_ref[0] * y, g2_ref[...], b2_ref[...])
    out_ref[0] = _swiglu_block(x2, shift2_ref[0], scale2_ref[0], gate2_ref[0], w_in_ref, w_out_ref,
                               g3_ref[...], b3_ref[...])


def _merge_ffn(x, gate1, u, o, gbr, pool_w, pool_scale, w_a, w_b, w_o, g2, b2,
               shift2, scale2, gate2, g3, b3, w_in, w_out):
    bsz, s, d = x.shape
    tm = min(ROW_TILE, s)
    halo_blocks = tm // POOL_HALO

    def row(n):
        return pl.BlockSpec((1, tm, n), lambda bi, i: (bi, i, 0))

    mod = pl.BlockSpec((1, 1, d), lambda bi, i: (bi, 0, 0))
    vec = _const_spec((1, d))
    return pl.pallas_call(
        _merge_ffn_kernel,
        grid=(bsz, s // tm),
        in_specs=[row(d), mod,
                  pl.BlockSpec((1, POOL_HALO, POOL_WIDTH),
                               lambda bi, i: (bi, jnp.maximum(i * halo_blocks - 1, 0), 0)),
                  row(POOL_WIDTH), row(Q_WIDTH), row(2 * d),
                  _const_spec(pool_w.shape), _const_spec((1, POOL_WIDTH)),
                  _const_spec(w_a.shape), _const_spec(w_b.shape), _const_spec(w_o.shape),
                  vec, vec, mod, mod, mod, vec, vec, _const_spec(w_in.shape), _const_spec(w_out.shape)],
        out_specs=row(d),
        out_shape=jax.ShapeDtypeStruct((bsz, s, d), F32),
        scratch_shapes=[pltpu.VMEM((POOL_HALO + tm, POOL_WIDTH), F32)],
        compiler_params=_params(2),
        name="pool_merge_ffn",
    )(x, gate1, u, u, o, gbr, pool_w, pool_scale.reshape(1, POOL_WIDTH), w_a, w_b, w_o,
      g2.reshape(1, d), b2.reshape(1, d), shift2, scale2, gate2, g3.reshape(1, d), b3.reshape(1, d), w_in, w_out)


def _mixer_in_weights(w):
    sizes = (POOL_WIDTH, Q_WIDTH) + (KV_WIDTH,) * 6 + (3 * N_HEADS, 2 * D_MODEL)
    offs = np.concatenate([[0], np.cumsum(sizes)])
    u, q, k_cmp, v_cmp, k_slc, v_slc, k_win, v_win, g_nsa, g_br = [w[:, offs[i]:offs[i + 1]] for i in range(10)]
    cols = [u, q, k_cmp, v_cmp]
    for k, v in ((k_slc, v_slc), (k_win, v_win)):
        for g in range(N_KV):
            cols += [k[:, g * HEAD_DIM:(g + 1) * HEAD_DIM], v[:, g * HEAD_DIM:(g + 1) * HEAD_DIM]]
    per_g = 3 * HEADS_PER_KV
    for g in range(N_KV):
        cols += [g_nsa[:, g * per_g:(g + 1) * per_g], jnp.zeros((w.shape[0], LANES - per_g), w.dtype)]
    cols.append(g_br)
    return jnp.concatenate(cols, axis=1).astype(BF16)


def _compress_weights(pos, w1, w2, value_slot):
    same_group = np.eye(N_KV, dtype=np.float32)

    def expand_w1(half):
        wh = half.reshape(CMP_STRIDE, 1, HEAD_DIM, 1, CMP_HIDDEN)
        z = wh * same_group.reshape(1, N_KV, 1, N_KV, 1)
        return z.reshape(CMP_STRIDE * KV_WIDTH, N_KV * CMP_HIDDEN)

    half_rows = CMP_STRIDE * HEAD_DIM
    w1_big = jnp.stack([expand_w1(w1[:half_rows]), expand_w1(w1[half_rows:])]).astype(BF16)
    slot = np.eye(2, dtype=np.float32)[value_slot]
    w2_big = (w2.reshape(1, CMP_HIDDEN, 1, 1, HEAD_DIM) * same_group.reshape(N_KV, 1, N_KV, 1, 1)
              * slot.reshape(1, 1, 1, 2, 1))
    w2_big = w2_big.reshape(N_KV * CMP_HIDDEN, N_KV * KV_PACK).astype(BF16)
    pos_rows = jnp.broadcast_to(pos.reshape(2, CMP_STRIDE, 1, HEAD_DIM), (2, CMP_STRIDE, N_KV, HEAD_DIM))
    return pos_rows.reshape(2, CMP_STRIDE * KV_WIDTH), w1_big, w2_big


def kernel(x, c, ln_in_g, ln_in_b, w_ada, b_ada, ffn1_w_in, ffn1_w_out, ln1_g, ln1_b, w_mix_in, pool_w, pool_scale,
           cmp_pos_k, cmp_k_w1, cmp_k_w2, cmp_pos_v, cmp_v_w1, cmp_v_w2, w_branch_a, w_branch_b, w_mix_out,
           ln2_g, ln2_b, ffn2_w_in, ffn2_w_out, ln3_g, ln3_b):
    bsz, s, d = x.shape
    for l in range(DEPTH):
        ada = _ada(c, w_ada[l], b_ada[l]).reshape(bsz, 3, 3, 1, d)
        mod = lambda i, j: ada[:, i, j]

        x, u, q, kc, vc, kvs, kvw, gates, gbr = _ffn_mixin(
            x, mod(0, 0), mod(0, 1), mod(0, 2), ln_in_g, ln_in_b, ln1_g[l], ln1_b[l],
            ffn1_w_in[l].astype(BF16), ffn1_w_out[l].astype(BF16),
            mod(1, 0), mod(1, 1), _mixer_in_weights(w_mix_in[l]), pre_ln=l == 0)
        posk, w1k, w2k = _compress_weights(cmp_pos_k[l], cmp_k_w1[l], cmp_k_w2[l], 0)
        posv, w1v, w2v = _compress_weights(cmp_pos_v[l], cmp_v_w1[l], cmp_v_w2[l], 1)
        kvc = _compress(kc, vc, posk, posv, w1k, w1v, w2k, w2v)
        o = _attention(q, gates, kvc, kvs, kvw)
        x = _merge_ffn(x, mod(1, 2), u, o, gbr, pool_w[l].astype(BF16), pool_scale[l],
                       w_branch_a[l].astype(BF16), w_branch_b[l].astype(BF16), w_mix_out[l].astype(BF16),
                       ln2_g[l], ln2_b[l], mod(2, 0), mod(2, 1), mod(2, 2), ln3_g[l], ln3_b[l],
                       ffn2_w_in[l].astype(BF16), ffn2_w_out[l].astype(BF16))
    return x
```

```python
import functools

import numpy as np
import jax
import jax.numpy as jnp
from jax import lax
from jax.experimental import pallas as pl
from jax.experimental.pallas import tpu as pltpu

F32 = jnp.float32
BF16 = jnp.bfloat16

D_MODEL = 1024
POOL_WIDTH = D_MODEL // 2
POOL_WINDOWS = (2, 4, 8, 16)
POOL_GROUP = POOL_WIDTH // len(POOL_WINDOWS)
POOL_HALO = 16
HEAD_DIM = 64
N_HEADS = (D_MODEL // 2) // HEAD_DIM
N_KV = 2
HEADS_PER_KV = N_HEADS // N_KV
Q_WIDTH = N_HEADS * HEAD_DIM
KV_WIDTH = N_KV * HEAD_DIM
CMP_STRIDE = 16
CMP_BLOCK = 2 * CMP_STRIDE
CMP_HIDDEN = 4 * HEAD_DIM
SLC_BLOCK = 64
N_SELECT = 16
WINDOW = 512
Q_BLOCK = SLC_BLOCK
ALIBI_MAX = 8.0
D_FF = 2816
DEPTH = 1
ALPHA = (2.0 * DEPTH) ** 0.25
LN_EPS = 1e-5
NEG = -1e30
FORCE = 1e9

LANES = 128
KV_PACK = 2 * HEAD_DIM
KEY_WIDTH = 2 * KV_PACK
FEAT_OFFSET_LANE = SLC_BLOCK
BIG = 1e30
M_INIT = -3e38
FF_CHUNK = 256
GATE_CHUNK = 512
ROW_TILE = 512
SLC_UNIT = 256
RANK_UNROLL = 2
WIN_NQ = 4
WIN_AHEAD = 1
SEL_NQ = 2
SLC_NQ = 2
TRIP_WIDTHS = (8, 6, 4, 2)
LOG2E = 1.4426950408889634
WIN_KEYS = WINDOW + 2 * Q_BLOCK
VMEM_LIMIT = 56 * 1024 * 1024

_C_U = 0
_C_Q = _C_U + POOL_WIDTH
_C_KC = _C_Q + Q_WIDTH
_C_VC = _C_KC + KV_WIDTH
_C_KVS = _C_VC + KV_WIDTH
_C_KVW = _C_KVS + N_KV * KV_PACK
_C_GN = _C_KVW + N_KV * KV_PACK
_C_GBR = _C_GN + N_KV * LANES
_C_END = _C_GBR + 2 * D_MODEL


def _dot(a, b):
    return jnp.dot(a, b, preferred_element_type=F32)


def _dot_nt(a, b):
    return lax.dot_general(a, b, (((1,), (1,)), ((), ())), preferred_element_type=F32)


def _layer_norm(x, g, b):
    mu = jnp.mean(x, axis=-1, keepdims=True)
    xc = x - mu
    var = jnp.mean(xc * xc, axis=-1, keepdims=True)
    return xc * lax.rsqrt(var + LN_EPS) * g + b


def _masked_softmax(s, mask):
    sm = jnp.where(mask, s, NEG)
    m = jnp.max(sm, axis=-1, keepdims=True)
    e = jnp.exp(sm - m)
    p = e * (1.0 / jnp.sum(e, axis=-1, keepdims=True))
    return jnp.where(mask, p, 0.0)


def _const_spec(shape):
    nd = len(shape)
    return pl.BlockSpec(shape, lambda *_: (0,) * nd, pipeline_mode=pl.Buffered(1))


def _params(n_grid):
    return pltpu.CompilerParams(dimension_semantics=("parallel",) * n_grid, vmem_limit_bytes=VMEM_LIMIT)


def _ada_kernel(c_ref, w_ref, b_ref, o_ref):
    c = c_ref[...]
    c_act = (c * jax.nn.sigmoid(c)).astype(BF16)
    o_ref[...] = _dot(c_act, w_ref[...].astype(BF16)) + b_ref[...]


def _ada(c, w, b):
    bsz, d = c.shape
    n = w.shape[1]
    tn = D_MODEL
    return pl.pallas_call(
        _ada_kernel,
        grid=(n // tn,),
        in_specs=[pl.BlockSpec((bsz, d), lambda j: (0, 0)),
                  pl.BlockSpec((d, tn), lambda j: (0, j)),
                  pl.BlockSpec((1, tn), lambda j: (0, j))],
        out_specs=pl.BlockSpec((bsz, tn), lambda j: (0, j)),
        out_shape=jax.ShapeDtypeStruct((bsz, n), F32),
        compiler_params=_params(1),
        name="ada_proj",
    )(c, w, b.reshape(1, n))


def _swiglu(h, w_in_ref, w_out_ref):
    acc = jnp.zeros((h.shape[0], w_out_ref.shape[1]), F32)
    for j in range(D_FF // FF_CHUNK):
        c0 = j * FF_CHUNK
        gt = _dot(h, w_in_ref[:, c0:c0 + FF_CHUNK])
        up = _dot(h, w_in_ref[:, D_FF + c0:D_FF + c0 + FF_CHUNK])
        act = (gt * jax.nn.sigmoid(gt) * up).astype(BF16)
        acc = acc + _dot(act, w_out_ref[c0:c0 + FF_CHUNK, :])
    return acc


def _swiglu_block(x, shift, scale, gate, w_in_ref, w_out_ref, g, b):
    h = (x * (1.0 + scale) + shift).astype(BF16)
    return _layer_norm(ALPHA * x + 0.5 * gate * _swiglu(h, w_in_ref, w_out_ref), g, b)


def _ffn_mixin_kernel(x_ref, shift0_ref, scale0_ref, gate0_ref, g0_ref, b0_ref, g1_ref, b1_ref, w_in_ref, w_out_ref,
                      shift1_ref, scale1_ref, w_ref,
                      x1_ref, u_ref, q_ref, kc_ref, vc_ref, kvs_ref, kvw_ref, gn_ref, gbr_ref, *, pre_ln):
    x = x_ref[0]
    if pre_ln:
        x = _layer_norm(x, g0_ref[...], b0_ref[...])
    x1 = _swiglu_block(x, shift0_ref[0], scale0_ref[0], gate0_ref[0], w_in_ref, w_out_ref, g1_ref[...], b1_ref[...])
    x1_ref[0] = x1

    h = (x1 * (1.0 + scale1_ref[0]) + shift1_ref[0]).astype(BF16)

    def proj(c0, c1):
        return _dot(h, w_ref[:, c0:c1])

    tm = x_ref.shape[1]
    pos = pl.program_id(1) * tm + lax.broadcasted_iota(jnp.int32, (tm, LANES), 0)
    lane = lax.broadcasted_iota(jnp.int32, (tm, LANES), 1)
    feat = jnp.where(lane == pos // SLC_BLOCK, 1.0,
                     jnp.where(lane == FEAT_OFFSET_LANE, (pos % SLC_BLOCK).astype(F32), 0.0)).astype(BF16)

    def put_u(z):
        u_ref[0] = z

    def put_q(z):
        q_ref[0] = (z * HEAD_DIM ** -0.5).astype(BF16)

    def put_cmp(z):
        kc_ref[0] = z[:, 0:KV_WIDTH]
        vc_ref[0] = z[:, KV_WIDTH:2 * KV_WIDTH]

    def put_keys(ref):
        def put(z):
            zb = z.astype(BF16)
            for g in range(N_KV):
                ref[0, g, :, 0:KV_PACK] = zb[:, g * KV_PACK:(g + 1) * KV_PACK]
                ref[0, g, :, KV_PACK:KEY_WIDTH] = feat
        return put

    def put_head_gates(z):
        sg = jax.nn.sigmoid(z)
        for g in range(N_KV):
            gn_ref[0, g] = sg[:, g * LANES:(g + 1) * LANES]

    def put_branch_gates(c0):
        def put(z):
            gbr_ref[0, :, c0 - _C_GBR:c0 - _C_GBR + z.shape[1]] = jax.nn.sigmoid(z).astype(BF16)
        return put

    gate_cols = [(c0, c0 + GATE_CHUNK) for c0 in range(_C_GBR, _C_END, GATE_CHUNK)]
    light = [((_C_U, _C_Q), put_u), ((_C_Q, _C_KC), put_q), ((_C_KC, _C_KVS), put_cmp),
             ((_C_KVS, _C_KVW), put_keys(kvs_ref)), ((_C_KVW, _C_GN), put_keys(kvw_ref)),
             ((_C_GN, _C_GBR), put_head_gates)]
    stages = []
    for n in range(max(len(gate_cols), len(light))):
        if n < len(gate_cols):
            stages.append((gate_cols[n], put_branch_gates(gate_cols[n][0])))
        if n < len(light):
            stages.append(light[n])
    pending = proj(*stages[0][0])
    for n, (_, sink) in enumerate(stages):
        z = pending
        if n + 1 < len(stages):
            pending = proj(*stages[n + 1][0])
        sink(z)


def _ffn_mixin(x, shift0, scale0, gate0, g0, b0, g1, b1, w_in, w_out, shift1, scale1, w_mix, *, pre_ln):
    bsz, s, d = x.shape
    tm = min(ROW_TILE, s)

    def row(n, dt):
        return pl.BlockSpec((1, tm, n), lambda b, i: (b, i, 0)), jax.ShapeDtypeStruct((bsz, s, n), dt)

    def grouped(n, dt):
        return (pl.BlockSpec((1, N_KV, tm, n), lambda b, i: (b, 0, i, 0)),
                jax.ShapeDtypeStruct((bsz, N_KV, s, n), dt))

    outs = [row(d, F32), row(POOL_WIDTH, F32), row(Q_WIDTH, BF16), row(KV_WIDTH, F32), row(KV_WIDTH, F32),
            grouped(KEY_WIDTH, BF16), grouped(KEY_WIDTH, BF16), grouped(LANES, F32), row(2 * D_MODEL, BF16)]
    mod = pl.BlockSpec((1, 1, d), lambda b, i: (b, 0, 0))
    vec = _const_spec((1, d))
    return pl.pallas_call(
        functools.partial(_ffn_mixin_kernel, pre_ln=pre_ln),
        grid=(bsz, s // tm),
        in_specs=[row(d, F32)[0], mod, mod, mod, vec, vec, vec, vec,
                  _const_spec(w_in.shape), _const_spec(w_out.shape), mod, mod, _const_spec(w_mix.shape)],
        out_specs=[o[0] for o in outs],
        out_shape=[o[1] for o in outs],
        compiler_params=_params(2),
        name="ffn_mixer_in",
    )(x, shift0, scale0, gate0, g0.reshape(1, d), b0.reshape(1, d), g1.reshape(1, d), b1.reshape(1, d),
      w_in, w_out, shift1, scale1, w_mix)


def _compress_kernel(k_ref, v_ref, posk_ref, posv_ref, w1k_ref, w1v_ref, w2k_ref, w2v_ref, o_ref, nxt_ref):
    r = k_ref.shape[1] // CMP_STRIDE

    def hidden(x_ref, pos_ref, w1_ref):
        x = jnp.concatenate([x_ref[0, pl.ds(j, r, stride=CMP_STRIDE), :] for j in range(CMP_STRIDE)], axis=1)
        first = _dot((x + pos_ref[0:1, :]).astype(BF16), w1_ref[0])
        nxt_ref[0:r, :] = _dot((x + pos_ref[1:2, :]).astype(BF16), w1_ref[1])
        nxt_ref[r:r + 8, :] = jnp.zeros((8, nxt_ref.shape[1]), F32)
        return jax.nn.gelu(first + nxt_ref[1:r + 1, :]).astype(BF16)

    hk = hidden(k_ref, posk_ref, w1k_ref)
    hv = hidden(v_ref, posv_ref, w1v_ref)
    o_ref[0] = (_dot(hk, w2k_ref[...]) + _dot(hv, w2v_ref[...])).astype(BF16)


def _compress(kc, vc, posk, posv, w1k, w1v, w2k, w2v):
    bsz, s, n = kc.shape
    r = s // CMP_STRIDE
    hid = w1k.shape[2]
    row = pl.BlockSpec((1, s, n), lambda b: (b, 0, 0))
    return pl.pallas_call(
        _compress_kernel,
        grid=(bsz,),
        in_specs=[row, row, _const_spec(posk.shape), _const_spec(posv.shape),
                  _const_spec(w1k.shape), _const_spec(w1v.shape),
                  _const_spec(w2k.shape), _const_spec(w2v.shape)],
        out_specs=pl.BlockSpec((1, r, N_KV * KV_PACK), lambda b: (b, 0, 0)),
        out_shape=jax.ShapeDtypeStruct((bsz, r, N_KV * KV_PACK), BF16),
        scratch_shapes=[pltpu.VMEM((r + 8, hid), F32)],
        compiler_params=_params(1),
        name="compress_mlp",
    )(kc, vc, posk, posv, w1k, w1v, w2k, w2v)


def _alibi_slope(head):
    return 2.0 ** (-ALIBI_MAX * (head + 1) / N_HEADS)


def _padded_heads(q_tile):
    lane = lax.broadcasted_iota(jnp.int32, (Q_BLOCK, LANES), 1)
    qf = q_tile.astype(F32)
    heads = []
    for hd in range(N_HEADS):
        slab = qf[:, (hd // 2) * LANES:(hd // 2 + 1) * LANES]
        if hd % 2:
            slab = pltpu.roll(slab, HEAD_DIM, 1)
        heads.append(jnp.where(lane < HEAD_DIM, slab, 0.0))
    return heads


def _pack_heads(o_rows):
    lane = lax.broadcasted_iota(jnp.int32, (Q_BLOCK, LANES), 1)
    pairs = []
    for j in range(HEADS_PER_KV // 2):
        even = o_rows[(2 * j) * Q_BLOCK:(2 * j + 1) * Q_BLOCK]
        odd = o_rows[(2 * j + 1) * Q_BLOCK:(2 * j + 2) * Q_BLOCK]
        pairs.append(jnp.where(lane < HEAD_DIM, pltpu.roll(even, HEAD_DIM, 1), odd))
    return jnp.concatenate(pairs, axis=1)


def _feature_rows(qb, hd):
    lane_row = lax.broadcasted_iota(jnp.int32, (1, LANES), 1)
    slope = _alibi_slope(hd)
    tail = jnp.where(lane_row == FEAT_OFFSET_LANE, slope, 0.0)
    block_bias = slope * SLC_BLOCK * (lane_row - qb).astype(F32)
    in_window = (lane_row >= qb - WINDOW // SLC_BLOCK) & (lane_row <= qb)
    slc_row = jnp.where(lane_row < SLC_BLOCK, block_bias, tail)
    win_row = jnp.where(lane_row < SLC_BLOCK, jnp.where(in_window, block_bias, -BIG), tail)
    return slc_row, win_row


def _window_kernel(q_ref, kpw_ref, triw_ref, o_ref):
    gw = HEADS_PER_KV * HEAD_DIM
    chains = [(k, g) for k in range(WIN_NQ) for g in range(N_KV)]

    def scores(k, g):
        qb = pl.program_id(1) * WIN_NQ + k
        heads = _padded_heads(q_ref[0, k * Q_BLOCK:(k + 1) * Q_BLOCK, :])
        w0 = pl.multiple_of(jnp.maximum((qb + 1) * Q_BLOCK - WIN_KEYS, 0), Q_BLOCK)
        tri = jnp.concatenate([triw_ref[jnp.minimum(qb, WIN_KEYS // SLC_BLOCK - 1)]] * HEADS_PER_KV, axis=0)
        qw = [jnp.concatenate([heads[hd], jnp.broadcast_to(_feature_rows(qb, hd)[1], (Q_BLOCK, LANES))], axis=1)
              for hd in range(g * HEADS_PER_KV, (g + 1) * HEADS_PER_KV)]
        kp = kpw_ref[0, g, pl.ds(w0, WIN_KEYS), :]
        return _dot_nt(jnp.concatenate(qw, axis=0).astype(BF16), kp) + tri, kp

    def finish(k, g, o_raw, inv_l):
        o_ref[0, k * Q_BLOCK:(k + 1) * Q_BLOCK, g * gw:(g + 1) * gw] = _pack_heads(o_raw * inv_l).astype(BF16)

    pending = [scores(*chains[n]) for n in range(WIN_AHEAD)]
    unfinished = None
    for n, (k, g) in enumerate(chains):
        s, kp = pending.pop(0)
        if n + WIN_AHEAD < len(chains):
            pending.append(scores(*chains[n + WIN_AHEAD]))
        e = jnp.exp(s - jnp.max(s, axis=-1, keepdims=True))
        o_raw = _dot(e.astype(BF16), kp[:, 0:KV_PACK])
        inv_l = 1.0 / jnp.sum(e, axis=-1, keepdims=True)
        if unfinished is not None:
            finish(*unfinished)
        unfinished = (k, g, o_raw, inv_l)
    finish(*unfinished)


def _window(q, kpw, tri_win):
    bsz, s, _ = q.shape
    qt = WIN_NQ * Q_BLOCK
    return pl.pallas_call(
        _window_kernel,
        grid=(bsz, s // qt),
        in_specs=[pl.BlockSpec((1, qt, Q_WIDTH), lambda b, i: (b, i, 0)),
                  pl.BlockSpec((1, N_KV, s, KEY_WIDTH), lambda b, i: (b, 0, 0, 0)),
                  _const_spec(tri_win.shape)],
        out_specs=pl.BlockSpec((1, qt, Q_WIDTH), lambda b, i: (b, i, 0)),
        out_shape=jax.ShapeDtypeStruct((bsz, s, Q_WIDTH), BF16),
        compiler_params=_params(2),
        name="window_attention",
    )(q, kpw, tri_win)


def _select_kernel(q_ref, kvc_ref, cbias_ref, ovl_ref, ocmp_ref, selneg_ref, first_ref, score_ref, *, n_sel):
    step = pl.program_id(1)
    rows = SEL_NQ * HEADS_PER_KV * Q_BLOCK
    n_slc = ovl_ref.shape[0]
    r = kvc_ref.shape[1]
    lanes_q = SEL_NQ * N_KV * Q_BLOCK
    heads = [_padded_heads(q_ref[0, k * Q_BLOCK:(k + 1) * Q_BLOCK, :]) for k in range(SEL_NQ)]

    row = lax.broadcasted_iota(jnp.int32, (rows, 1), 0)
    t = (step * SEL_NQ + row // (HEADS_PER_KV * Q_BLOCK)) * Q_BLOCK + (row & (Q_BLOCK - 1))
    last_cmp = (t - (CMP_BLOCK - 1)) >> 4
    visible = lax.broadcasted_iota(jnp.int32, (rows, r), 1) <= last_cmp
    p_sums = [[None] * N_KV for _ in range(SEL_NQ)]
    gw = HEADS_PER_KV * HEAD_DIM
    def cmp_scores(g):
        q_pad = jnp.concatenate([heads[k][g * HEADS_PER_KV + h] for k in range(SEL_NQ)
                                 for h in range(HEADS_PER_KV)], axis=0).astype(BF16)
        return _dot_nt(q_pad, kvc_ref[0, :, g * KV_PACK:(g + 1) * KV_PACK])

    raw = [cmp_scores(g) for g in range(N_KV)]
    for g in range(N_KV):
        kvc = kvc_ref[0, :, g * KV_PACK:(g + 1) * KV_PACK]
        p_cmp = _masked_softmax(raw[g] + cbias_ref[g], visible)
        o_cmp = _dot(p_cmp.astype(BF16), kvc)
        for k in range(SEL_NQ):
            base = k * HEADS_PER_KV * Q_BLOCK
            ocmp_ref[0, k * Q_BLOCK:(k + 1) * Q_BLOCK, g * gw:(g + 1) * gw] = _pack_heads(
                o_cmp[base:base + HEADS_PER_KV * Q_BLOCK]).astype(BF16)
            p_sum = p_cmp[base:base + Q_BLOCK]
            for h in range(1, HEADS_PER_KV):
                p_sum = p_sum + p_cmp[base + h * Q_BLOCK:base + (h + 1) * Q_BLOCK]
            p_sums[k][g] = p_sum

    p_all = jnp.concatenate([p_sums[k][g] for k in range(SEL_NQ) for g in range(N_KV)], axis=0)
    p_hi = p_all.astype(BF16)
    p_lo = (p_all - p_hi.astype(F32)).astype(BF16)
    ovl = ovl_ref[...]
    imp_t = _dot_nt(ovl, p_hi) + _dot_nt(ovl, p_lo)

    blk = lax.broadcasted_iota(jnp.int32, (n_slc, lanes_q), 0)
    qb = step * SEL_NQ + lax.broadcasted_iota(jnp.int32, (n_slc, lanes_q), 1) // (N_KV * Q_BLOCK)
    forced = (blk == 0) | (blk == qb) | (blk == qb - 1)
    score = jnp.where(blk > qb, NEG, jnp.where(forced, FORCE, jnp.where(imp_t > 0.0, imp_t, 0.0)))
    key = lax.bitcast_convert_type(score, jnp.int32)
    key_prev = key - 1
    score_ref[...] = key

    def rank_step(i, ranks):
        out = []
        for u in range(RANK_UNROLL):
            row_i = i * RANK_UNROLL + u
            ri = score_ref[pl.ds(row_i, 1), :]
            out.append(ranks[u] + jnp.where(ri > jnp.where(blk > row_i, key_prev, key), 1.0, 0.0))
        return tuple(out)

    ranks = lax.fori_loop(0, (step * SEL_NQ + SEL_NQ - 1) // RANK_UNROLL + 1, rank_step,
                          (jnp.zeros((n_slc, lanes_q), F32),) * RANK_UNROLL)
    chosen = (functools.reduce(jnp.add, ranks) < n_sel) & (blk <= qb)

    oldest = jnp.where(chosen & (blk >= 1), blk, n_slc).astype(F32)
    sel_t = jnp.where(chosen, 1.0, 0.0).astype(BF16)
    sel_t = jnp.concatenate([sel_t, jnp.zeros((LANES - n_slc, lanes_q), BF16)], axis=0)
    eye = jnp.where(lax.broadcasted_iota(jnp.int32, (lanes_q, lanes_q), 0)
                    == lax.broadcasted_iota(jnp.int32, (lanes_q, lanes_q), 1), 1.0, 0.0).astype(BF16)
    sel = _dot_nt(eye, sel_t)
    lane = lax.broadcasted_iota(jnp.int32, (Q_BLOCK, LANES), 1)
    for k in range(SEL_NQ):
        lanes_k = slice(k * N_KV * Q_BLOCK, (k + 1) * N_KV * Q_BLOCK)
        first = jnp.min(jnp.min(oldest[:, lanes_k], axis=1, keepdims=True), axis=0, keepdims=True)
        first_ref[0, k] = jnp.broadcast_to(first, first_ref.shape[2:]).astype(jnp.int32)
        for g in range(N_KV):
            base = (k * N_KV + g) * Q_BLOCK
            selneg_ref[0, g, k * Q_BLOCK:(k + 1) * Q_BLOCK, :] = jnp.where(
                lane < SLC_BLOCK, (sel[base:base + Q_BLOCK] - 1.0) * BIG, 0.0).astype(BF16)


def _select(q, kvc, cbias, overlap_t, n_sel):
    bsz, s, _ = q.shape
    r = kvc.shape[1]
    n_slc = overlap_t.shape[0]
    qt = SEL_NQ * Q_BLOCK
    return pl.pallas_call(
        functools.partial(_select_kernel, n_sel=n_sel),
        grid=(bsz, s // qt),
        in_specs=[pl.BlockSpec((1, qt, Q_WIDTH), lambda b, i: (b, i, 0)),
                  pl.BlockSpec((1, r, N_KV * KV_PACK), lambda b, i: (b, 0, 0)),
                  _const_spec(cbias.shape), _const_spec(overlap_t.shape)],
        out_specs=[pl.BlockSpec((1, qt, Q_WIDTH), lambda b, i: (b, i, 0)),
                   pl.BlockSpec((1, N_KV, qt, LANES), lambda b, i: (b, 0, i, 0)),
                   pl.BlockSpec((1, SEL_NQ, 8, LANES), lambda b, i: (b, i, 0, 0))],
        out_shape=[jax.ShapeDtypeStruct((bsz, s, Q_WIDTH), BF16),
                   jax.ShapeDtypeStruct((bsz, N_KV, s, LANES), BF16),
                   jax.ShapeDtypeStruct((bsz, s // Q_BLOCK, 8, LANES), jnp.int32)],
        scratch_shapes=[pltpu.VMEM((n_slc, SEL_NQ * N_KV * Q_BLOCK), jnp.int32)],
        compiler_params=_params(2),
        name="compressed_attention_select",
    )(q, kvc, cbias, overlap_t)


def _selected_kernel(first_ref, q_ref, selneg_ref, gate_ref, ocmp_ref, owin_ref, kps_ref, tris_ref,
                     o_ref, qs_ref, s_ref, mt_ref, m_ref, lt_ref, acc_ref):
    qbs = [pl.program_id(1) * SLC_NQ + k for k in range(SLC_NQ)]
    rows = SLC_NQ * HEADS_PER_KV * Q_BLOCK
    heads = [_padded_heads(q_ref[0, k * Q_BLOCK:(k + 1) * Q_BLOCK, :]) for k in range(SLC_NQ)]
    for g in range(N_KV):
        qs = []
        for k in range(SLC_NQ):
            masked_out = selneg_ref[0, g, k * Q_BLOCK:(k + 1) * Q_BLOCK, :].astype(F32)
            qs += [jnp.concatenate([heads[k][hd], masked_out + _feature_rows(qbs[k], hd)[0]], axis=1)
                   for hd in range(g * HEADS_PER_KV, (g + 1) * HEADS_PER_KV)]
        qs_ref[g] = jnp.concatenate(qs, axis=0).astype(BF16)

    unit_blocks = SLC_UNIT // SLC_BLOCK
    diag_unit = qbs[0] // unit_blocks
    first = functools.reduce(jnp.minimum, [first_ref[pl.program_id(0), qb] for qb in qbs])
    first_unit = jnp.minimum(first, qbs[0]) // unit_blocks
    lead = jnp.minimum(first_unit, 1)
    n_units = diag_unit - first_unit + 1 + lead

    def unit_of(i):
        return jnp.where((i < lead) | (i >= n_units), 0, first_unit + i - lead)

    def unit_start(i):
        return pl.multiple_of(unit_of(i) * SLC_UNIT, SLC_UNIT)

    def slot_start(i):
        return pl.multiple_of(i * SLC_UNIT, SLC_UNIT)

    def trip_chains(first_slot, width):
        return [(g, first_slot + u) for u in range(width) for g in range(N_KV)]

    def score_matmul(g, i):
        return _dot_nt(qs_ref[g], kps_ref[0, g, pl.ds(unit_start(i), SLC_UNIT), :])

    def score_finish(g, i, raw):
        tri = []
        for qb in qbs:
            variant = jnp.where(i >= n_units, unit_blocks + 1,
                                jnp.where(unit_of(i) == diag_unit, qb % unit_blocks, unit_blocks))
            tri += [tris_ref[variant]] * HEADS_PER_KV
        s = (raw + jnp.concatenate(tri, axis=0)) * LOG2E
        s_ref[g, :, pl.ds(slot_start(i), SLC_UNIT)] = s
        mt_ref[g] = jnp.maximum(jnp.maximum(mt_ref[g], s[:, 0:LANES]), s[:, LANES:SLC_UNIT])

    def score_trip(first_slot, width):
        chains = trip_chains(first_slot, width)
        pending = score_matmul(*chains[0])
        for n, (g, i) in enumerate(chains):
            raw = pending
            if n + 1 < len(chains):
                pending = score_matmul(*chains[n + 1])
            score_finish(g, i, raw)

    def exponentials(g, i):
        m = m_ref[g]
        es = [jnp.exp2(s_ref[g, :, pl.ds(slot_start(i) + j * LANES, LANES)] - m) for j in range(SLC_UNIT // LANES)]
        lt_ref[g] = lt_ref[g] + functools.reduce(jnp.add, es)
        return jnp.concatenate(es, axis=1).astype(BF16)

    def value_trip(first_slot, width):
        chains = trip_chains(first_slot, width)
        pending = exponentials(*chains[0])
        for n, (g, i) in enumerate(chains):
            e = pending
            if n + 1 < len(chains):
                pending = exponentials(*chains[n + 1])
            acc_ref[g] = acc_ref[g] + _dot(e, kps_ref[0, g, pl.ds(unit_start(i), SLC_UNIT), 0:KV_PACK])

    def sweep(trip):
        done = 0
        for tier, width in enumerate(TRIP_WIDTHS):
            left = n_units - done
            if tier + 1 < len(TRIP_WIDTHS):
                count = left // width + jnp.where(left % width > TRIP_WIDTHS[tier + 1], 1, 0)
            else:
                count = (left + width - 1) // width
            count = jnp.maximum(count, 0)

            def body(p, carry, base=done, width=width):
                trip(base + p * width, width)
                return carry

            lax.fori_loop(0, count, body, 0)
            done = done + count * width

    mt_ref[...] = jnp.full(mt_ref.shape, M_INIT, F32)
    sweep(score_trip)
    for g in range(N_KV):
        m_ref[g] = jnp.broadcast_to(jnp.max(mt_ref[g], axis=-1, keepdims=True), (rows, LANES))
    lt_ref[...] = jnp.zeros(lt_ref.shape, F32)
    acc_ref[...] = jnp.zeros(acc_ref.shape, F32)
    sweep(value_trip)

    lane = lax.broadcasted_iota(jnp.int32, (Q_BLOCK, LANES), 1)
    for g in range(N_KV):
        l = jnp.sum(lt_ref[g], axis=-1, keepdims=True)
        o_rows = acc_ref[g] * jnp.where(l > 0.0, 1.0 / l, 0.0)
        for k in range(SLC_NQ):
            qr = slice(k * Q_BLOCK, (k + 1) * Q_BLOCK)
            o_slc = _pack_heads(o_rows[k * HEADS_PER_KV * Q_BLOCK:(k + 1) * HEADS_PER_KV * Q_BLOCK])
            gt = gate_ref[0, g, qr, :]
            for j in range(HEADS_PER_KV // 2):
                cols = slice((g * HEADS_PER_KV // 2 + j) * LANES, (g * HEADS_PER_KV // 2 + j + 1) * LANES)
                branches = (ocmp_ref[0, qr, cols].astype(F32), o_slc[:, j * LANES:(j + 1) * LANES],
                            owin_ref[0, qr, cols].astype(F32))
                total = None
                for c, branch in enumerate(branches):
                    even, odd = 3 * (2 * j) + c, 3 * (2 * j + 1) + c
                    gate = jnp.take_along_axis(gt, jnp.where(lane < HEAD_DIM, even, odd), axis=1)
                    total = gate * branch if total is None else total + gate * branch
                o_ref[0, qr, cols] = total.astype(BF16)


def _selected(first, q, selneg, gates, o_cmp, o_win, kps, tri_slc):
    bsz, s, _ = q.shape
    qt = SLC_NQ * Q_BLOCK
    rows = HEADS_PER_KV * qt
    assert (SLC_UNIT // SLC_BLOCK) % SLC_NQ == 0 and s % qt == 0

    def row(n):
        return pl.BlockSpec((1, qt, n), lambda b, i, first_ref: (b, i, 0))

    def grouped(n):
        return pl.BlockSpec((1, N_KV, qt, n), lambda b, i, first_ref: (b, 0, i, 0))

    grid_spec = pltpu.PrefetchScalarGridSpec(
        num_scalar_prefetch=1,
        grid=(bsz, s // qt),
        in_specs=[row(Q_WIDTH), grouped(LANES), grouped(LANES), row(Q_WIDTH), row(Q_WIDTH),
                  pl.BlockSpec((1, N_KV, s, KEY_WIDTH), lambda b, i, first_ref: (b, 0, 0, 0)),
                  _const_spec(tri_slc.shape)],
        out_specs=row(Q_WIDTH),
        scratch_shapes=[pltpu.VMEM((N_KV, rows, KEY_WIDTH), BF16),
                        pltpu.VMEM((N_KV, rows, s + SLC_UNIT), F32)]
        + [pltpu.VMEM((N_KV, rows, LANES), F32)] * 4)
    return pl.pallas_call(
        _selected_kernel,
        grid_spec=grid_spec,
        out_shape=jax.ShapeDtypeStruct((bsz, s, Q_WIDTH), BF16),
        compiler_params=_params(2),
        name="selected_attention",
    )(first, q, selneg, gates, o_cmp, o_win, kps, tri_slc)


def _attention(q, gates, kvc, kps, kpw):
    bsz, s, _ = q.shape
    r = kvc.shape[1]
    n_cmp = r - 1
    n_slc = s // SLC_BLOCK
    n_sel = min(N_SELECT, n_slc)
    rows = HEADS_PER_KV * Q_BLOCK
    assert n_slc <= SLC_BLOCK and n_slc % RANK_UNROLL == 0 and s % SLC_UNIT == 0 and s >= WIN_KEYS
    assert s % (WIN_NQ * Q_BLOCK) == 0 and s % (SEL_NQ * Q_BLOCK) == 0

    slopes = np.array([_alibi_slope(hd) for hd in range(N_HEADS)])
    slope_rows = np.repeat(slopes.reshape(N_KV, HEADS_PER_KV), Q_BLOCK, axis=1).reshape(N_KV, rows, 1)
    cbias = np.tile(slope_rows * (CMP_STRIDE * np.arange(r))[None, None, :], (1, SEL_NQ, 1))
    start = np.arange(r)[None, :] * CMP_STRIDE
    blk = np.arange(n_slc)[:, None] * SLC_BLOCK
    overlap_t = ((start < blk + SLC_BLOCK) & (start + CMP_BLOCK > blk) & (np.arange(r)[None, :] < n_cmp))

    ql = np.arange(Q_BLOCK)[:, None]
    kl = np.arange(SLC_BLOCK)[None, :]
    lower = np.where(kl > ql, -BIG, 0.0)
    upper = np.where(kl <= ql, -BIG, 0.0)
    unit_blocks = SLC_UNIT // SLC_BLOCK
    tri_slc = np.zeros((unit_blocks + 2, Q_BLOCK, SLC_UNIT))
    for j in range(unit_blocks):
        tri_slc[j, :, j * SLC_BLOCK:(j + 1) * SLC_BLOCK] = lower
    tri_slc[unit_blocks + 1] = -BIG
    win_blocks = WINDOW // SLC_BLOCK
    lead = WIN_KEYS // SLC_BLOCK - 1
    tri_win = np.zeros((lead + 1, Q_BLOCK, WIN_KEYS))
    for v in range(lead + 1):
        diag = v
        tri_win[v, :, diag * SLC_BLOCK:(diag + 1) * SLC_BLOCK] = lower
        if diag >= win_blocks:
            old = diag - win_blocks
            tri_win[v, :, old * SLC_BLOCK:(old + 1) * SLC_BLOCK] = upper

    o_win = _window(q, kpw, jnp.asarray(tri_win, F32))
    o_cmp, selneg, first = _select(q, kvc, jnp.asarray(cbias, F32), jnp.asarray(overlap_t, BF16), n_sel)
    return _selected(first[:, :, 0, 0], q, selneg, gates, o_cmp, o_win, kps, jnp.asarray(tri_slc, F32))


def _merge_ffn_kernel(x_ref, gate1_ref, uprev_ref, u_ref, o_ref, gbr_ref, pw_ref, ps_ref, wa_ref, wb_ref, wo_ref,
                      g2_ref, b2_ref, shift2_ref, scale2_ref, gate2_ref, g3_ref, b3_ref, w_in_ref, w_out_ref,
                      out_ref, ubuf_ref):
    i = pl.program_id(1)
    tm = u_ref.shape[1]
    ubuf_ref[0:POOL_HALO, :] = jnp.where(i == 0, 0.0, uprev_ref[0])
    ubuf_ref[POOL_HALO:POOL_HALO + tm, :] = u_ref[0]
    t = i * tm + lax.broadcasted_iota(jnp.int32, (tm, 1), 0)

    y_b = _dot(o_ref[0], wb_ref[...])
    mixed = []
    for gi, w in enumerate(POOL_WINDOWS):
        cs = slice(gi * POOL_GROUP, (gi + 1) * POOL_GROUP)
        cur = ubuf_ref[POOL_HALO:POOL_HALO + tm, cs]
        total = cur
        for k in range(1, w):
            total = total + ubuf_ref[POOL_HALO - k:POOL_HALO - k + tm, cs]
        inv_cnt = 1.0 / jnp.minimum(t + 1, w).astype(F32)
        delta = (total * inv_cnt - cur).astype(BF16)
        mixed.append((_dot(delta, pw_ref[gi]) * ps_ref[:, cs]).astype(BF16))
    y_a = _dot(jnp.concatenate(mixed, axis=1), wa_ref[...])
    d = y_a.shape[1]
    y = (gbr_ref[0, :, 0:d].astype(F32) * y_a + gbr_ref[0, :, d:2 * d].astype(F32) * y_b).astype(BF16)
    y = _dot(y, wo_ref[...])
    x2 = _layer_norm(ALPHA * x_ref[0] + gate1_ref[0] * y, g2_ref[...], b2_ref[...])
    out_ref[0] = _swiglu_block(x2, shift2_ref[0], scale2_ref[0], gate2_ref[0], w_in_ref, w_out_ref,
                               g3_ref[...], b3_ref[...])


def _merge_ffn(x, gate1, u, o, gbr, pool_w, pool_scale, w_a, w_b, w_o, g2, b2,
               shift2, scale2, gate2, g3, b3, w_in, w_out):
    bsz, s, d = x.shape
    tm = min(ROW_TILE, s)
    halo_blocks = tm // POOL_HALO

    def row(n):
        return pl.BlockSpec((1, tm, n), lambda bi, i: (bi, i, 0))

    mod = pl.BlockSpec((1, 1, d), lambda bi, i: (bi, 0, 0))
    vec = _const_spec((1, d))
    return pl.pallas_call(
        _merge_ffn_kernel,
        grid=(bsz, s // tm),
        in_specs=[row(d), mod,
                  pl.BlockSpec((1, POOL_HALO, POOL_WIDTH),
                               lambda bi, i: (bi, jnp.maximum(i * halo_blocks - 1, 0), 0)),
                  row(POOL_WIDTH), row(Q_WIDTH), row(2 * d),
                  _const_spec(pool_w.shape), _const_spec((1, POOL_WIDTH)),
                  _const_spec(w_a.shape), _const_spec(w_b.shape), _const_spec(w_o.shape),
                  vec, vec, mod, mod, mod, vec, vec, _const_spec(w_in.shape), _const_spec(w_out.shape)],
        out_specs=row(d),
        out_shape=jax.ShapeDtypeStruct((bsz, s, d), F32),
        scratch_shapes=[pltpu.VMEM((POOL_HALO + tm, POOL_WIDTH), F32)],
        compiler_params=_params(2),
        name="pool_merge_ffn",
    )(x, gate1, u, u, o, gbr, pool_w, pool_scale.reshape(1, POOL_WIDTH), w_a, w_b, w_o,
      g2.reshape(1, d), b2.reshape(1, d), shift2, scale2, gate2, g3.reshape(1, d), b3.reshape(1, d), w_in, w_out)


def _mixer_in_weights(w):
    sizes = (POOL_WIDTH, Q_WIDTH) + (KV_WIDTH,) * 6 + (3 * N_HEADS, 2 * D_MODEL)
    offs = np.concatenate([[0], np.cumsum(sizes)])
    u, q, k_cmp, v_cmp, k_slc, v_slc, k_win, v_win, g_nsa, g_br = [w[:, offs[i]:offs[i + 1]] for i in range(10)]
    cols = [u, q, k_cmp, v_cmp]
    for k, v in ((k_slc, v_slc), (k_win, v_win)):
        for g in range(N_KV):
            cols += [k[:, g * HEAD_DIM:(g + 1) * HEAD_DIM], v[:, g * HEAD_DIM:(g + 1) * HEAD_DIM]]
    per_g = 3 * HEADS_PER_KV
    for g in range(N_KV):
        cols += [g_nsa[:, g * per_g:(g + 1) * per_g], jnp.zeros((w.shape[0], LANES - per_g), w.dtype)]
    cols.append(g_br)
    return jnp.concatenate(cols, axis=1).astype(BF16)


def _compress_weights(pos, w1, w2, value_slot):
    same_group = np.eye(N_KV, dtype=np.float32)

    def expand_w1(half):
        wh = half.reshape(CMP_STRIDE, 1, HEAD_DIM, 1, CMP_HIDDEN)
        z = wh * same_group.reshape(1, N_KV, 1, N_KV, 1)
        return z.reshape(CMP_STRIDE * KV_WIDTH, N_KV * CMP_HIDDEN)

    half_rows = CMP_STRIDE * HEAD_DIM
    w1_big = jnp.stack([expand_w1(w1[:half_rows]), expand_w1(w1[half_rows:])]).astype(BF16)
    slot = np.eye(2, dtype=np.float32)[value_slot]
    w2_big = (w2.reshape(1, CMP_HIDDEN, 1, 1, HEAD_DIM) * same_group.reshape(N_KV, 1, N_KV, 1, 1)
              * slot.reshape(1, 1, 1, 2, 1))
    w2_big = w2_big.reshape(N_KV * CMP_HIDDEN, N_KV * KV_PACK).astype(BF16)
    pos_rows = jnp.broadcast_to(pos.reshape(2, CMP_STRIDE, 1, HEAD_DIM), (2, CMP_STRIDE, N_KV, HEAD_DIM))
    return pos_rows.reshape(2, CMP_STRIDE * KV_WIDTH), w1_big, w2_big


def kernel(x, c, ln_in_g, ln_in_b, w_ada, b_ada, ffn1_w_in, ffn1_w_out, ln1_g, ln1_b, w_mix_in, pool_w, pool_scale,
           cmp_pos_k, cmp_k_w1, cmp_k_w2, cmp_pos_v, cmp_v_w1, cmp_v_w2, w_branch_a, w_branch_b, w_mix_out,
           ln2_g, ln2_b, ffn2_w_in, ffn2_w_out, ln3_g, ln3_b):
    bsz, s, d = x.shape
    for l in range(DEPTH):
        ada = _ada(c, w_ada[l], b_ada[l]).reshape(bsz, 3, 3, 1, d)
        mod = lambda i, j: ada[:, i, j]

        x, u, q, kc, vc, kvs, kvw, gates, gbr = _ffn_mixin(
            x, mod(0, 0), mod(0, 1), mod(0, 2), ln_in_g, ln_in_b, ln1_g[l], ln1_b[l],
            ffn1_w_in[l].astype(BF16), ffn1_w_out[l].astype(BF16),
            mod(1, 0), mod(1, 1), _mixer_in_weights(w_mix_in[l]), pre_ln=l == 0)
        posk, w1k, w2k = _compress_weights(cmp_pos_k[l], cmp_k_w1[l], cmp_k_w2[l], 0)
        posv, w1v, w2v = _compress_weights(cmp_pos_v[l], cmp_v_w1[l], cmp_v_w2[l], 1)
        kvc = _compress(kc, vc, posk, posv, w1k, w1v, w2k, w2v)
        o = _attention(q, gates, kvc, kvs, kvw)
        x = _merge_ffn(x, mod(1, 2), u, o, gbr, pool_w[l].astype(BF16), pool_scale[l],
                       w_branch_a[l].astype(BF16), w_branch_b[l].astype(BF16), w_mix_out[l].astype(BF16),
                       ln2_g[l], ln2_b[l], mod(2, 0), mod(2, 1), mod(2, 2), ln3_g[l], ln3_b[l],
                       ffn2_w_in[l].astype(BF16), ffn2_w_out[l].astype(BF16))
    return x
```

```python
import functools

import numpy as np
import jax
import jax.numpy as jnp
from jax import lax
from jax.experimental import pallas as pl
from jax.experimental.pallas import tpu as pltpu

F32 = jnp.float32
BF16 = jnp.bfloat16

D_MODEL = 1024
POOL_WIDTH = D_MODEL // 2
POOL_WINDOWS = (2, 4, 8, 16)
POOL_GROUP = POOL_WIDTH // len(POOL_WINDOWS)
POOL_HALO = 32
HEAD_DIM = 64
N_HEADS = (D_MODEL // 2) // HEAD_DIM
N_KV = 2
HEADS_PER_KV = N_HEADS // N_KV
Q_WIDTH = N_HEADS * HEAD_DIM
KV_WIDTH = N_KV * HEAD_DIM
CMP_STRIDE = 16
CMP_BLOCK = 2 * CMP_STRIDE
CMP_HIDDEN = 4 * HEAD_DIM
SLC_BLOCK = 64
N_SELECT = 16
WINDOW = 512
Q_BLOCK = SLC_BLOCK
ALIBI_MAX = 8.0
D_FF = 2816
DEPTH = 1
ALPHA = (2.0 * DEPTH) ** 0.25
LN_EPS = 1e-5
NEG = -1e30
FORCE = 1e9

LANES = 128
KV_PACK = 2 * HEAD_DIM
KEY_WIDTH = 2 * KV_PACK
FEAT_OFFSET_LANE = SLC_BLOCK
BIG = 1e30
M_INIT = -3e38
FF_CHUNK = 256
GATE_CHUNK = 512
ROW_TILE = 512
SLC_UNIT = 256
RANK_UNROLL = 2
WIN_NQ = 4
WIN_AHEAD = 1
SEL_NQ = 2
SLC_NQ = 2
TRIP_WIDTHS = (8, 6, 4, 2)
LOG2E = 1.4426950408889634
WIN_KEYS = WINDOW + 2 * Q_BLOCK
VMEM_LIMIT = 56 * 1024 * 1024

_C_U = 0
_C_Q = _C_U + POOL_WIDTH
_C_KC = _C_Q + Q_WIDTH
_C_VC = _C_KC + KV_WIDTH
_C_KVS = _C_VC + KV_WIDTH
_C_KVW = _C_KVS + N_KV * KV_PACK
_C_GN = _C_KVW + N_KV * KV_PACK
_C_GBR = _C_GN + N_KV * LANES
_C_END = _C_GBR + 2 * D_MODEL


def _dot(a, b):
    return jnp.dot(a, b, preferred_element_type=F32)


def _dot_nt(a, b):
    return lax.dot_general(a, b, (((1,), (1,)), ((), ())), preferred_element_type=F32)


def _layer_norm(x, g, b):
    mu = jnp.mean(x, axis=-1, keepdims=True)
    xc = x - mu
    var = jnp.mean(xc * xc, axis=-1, keepdims=True)
    return xc * lax.rsqrt(var + LN_EPS) * g + b


def _masked_softmax(s, mask):
    sm = jnp.where(mask, s, NEG)
    m = jnp.max(sm, axis=-1, keepdims=True)
    e = jnp.exp(sm - m)
    p = e * (1.0 / jnp.sum(e, axis=-1, keepdims=True))
    return jnp.where(mask, p, 0.0)


def _const_spec(shape):
    nd = len(shape)
    return pl.BlockSpec(shape, lambda *_: (0,) * nd, pipeline_mode=pl.Buffered(1))


def _params(n_grid):
    return pltpu.CompilerParams(dimension_semantics=("parallel",) * n_grid, vmem_limit_bytes=VMEM_LIMIT)


def _ada_kernel(c_ref, w_ref, b_ref, o_ref):
    c = c_ref[...]
    c_act = (c * jax.nn.sigmoid(c)).astype(BF16)
    o_ref[...] = _dot(c_act, w_ref[...].astype(BF16)) + b_ref[...]


def _ada(c, w, b):
    bsz, d = c.shape
    n = w.shape[1]
    tn = D_MODEL
    return pl.pallas_call(
        _ada_kernel,
        grid=(n // tn,),
        in_specs=[pl.BlockSpec((bsz, d), lambda j: (0, 0)),
                  pl.BlockSpec((d, tn), lambda j: (0, j)),
                  pl.BlockSpec((1, tn), lambda j: (0, j))],
        out_specs=pl.BlockSpec((bsz, tn), lambda j: (0, j)),
        out_shape=jax.ShapeDtypeStruct((bsz, n), F32),
        compiler_params=_params(1),
        name="ada_proj",
    )(c, w, b.reshape(1, n))


def _swiglu(h, w_in_ref, w_out_ref):
    acc = jnp.zeros((h.shape[0], w_out_ref.shape[1]), F32)
    for j in range(D_FF // FF_CHUNK):
        c0 = j * FF_CHUNK
        gt = _dot(h, w_in_ref[:, c0:c0 + FF_CHUNK])
        up = _dot(h, w_in_ref[:, D_FF + c0:D_FF + c0 + FF_CHUNK])
        act = (gt * jax.nn.sigmoid(gt) * up).astype(BF16)
        acc = acc + _dot(act, w_out_ref[c0:c0 + FF_CHUNK, :])
    return acc


def _swiglu_block(x, shift, scale, gate, w_in_ref, w_out_ref, g, b):
    h = (x * (1.0 + scale) + shift).astype(BF16)
    return _layer_norm(ALPHA * x + 0.5 * gate * _swiglu(h, w_in_ref, w_out_ref), g, b)


def _ffn_mixin_kernel(x_ref, shift0_ref, scale0_ref, gate0_ref, g0_ref, b0_ref, g1_ref, b1_ref, w_in_ref, w_out_ref,
                      shift1_ref, scale1_ref, w_ref,
                      x1_ref, u_ref, q_ref, kc_ref, vc_ref, kvs_ref, kvw_ref, gn_ref, gbr_ref, *, pre_ln):
    x = x_ref[0]
    if pre_ln:
        x = _layer_norm(x, g0_ref[...], b0_ref[...])
    x1 = _swiglu_block(x, shift0_ref[0], scale0_ref[0], gate0_ref[0], w_in_ref, w_out_ref, g1_ref[...], b1_ref[...])
    x1_ref[0] = x1

    h = (x1 * (1.0 + scale1_ref[0]) + shift1_ref[0]).astype(BF16)

    def proj(c0, c1):
        return _dot(h, w_ref[:, c0:c1])

    tm = x_ref.shape[1]
    pos = pl.program_id(1) * tm + lax.broadcasted_iota(jnp.int32, (tm, LANES), 0)
    lane = lax.broadcasted_iota(jnp.int32, (tm, LANES), 1)
    feat = jnp.where(lane == pos // SLC_BLOCK, 1.0,
                     jnp.where(lane == FEAT_OFFSET_LANE, (pos % SLC_BLOCK).astype(F32), 0.0)).astype(BF16)

    def put_u(z):
        u_ref[0] = z

    def put_q(z):
        q_ref[0] = (z * HEAD_DIM ** -0.5).astype(BF16)

    def put_cmp(z):
        kc_ref[0] = z[:, 0:KV_WIDTH]
        vc_ref[0] = z[:, KV_WIDTH:2 * KV_WIDTH]

    def put_keys(ref):
        def put(z):
            zb = z.astype(BF16)
            for g in range(N_KV):
                ref[0, g, :, 0:KV_PACK] = zb[:, g * KV_PACK:(g + 1) * KV_PACK]
                ref[0, g, :, KV_PACK:KEY_WIDTH] = feat
        return put

    def put_head_gates(z):
        sg = jax.nn.sigmoid(z)
        for g in range(N_KV):
            gn_ref[0, g] = sg[:, g * LANES:(g + 1) * LANES]

    def put_branch_gates(c0):
        def put(z):
            gbr_ref[0, :, c0 - _C_GBR:c0 - _C_GBR + z.shape[1]] = jax.nn.sigmoid(z).astype(BF16)
        return put

    gate_cols = [(c0, c0 + GATE_CHUNK) for c0 in range(_C_GBR, _C_END, GATE_CHUNK)]
    light = [((_C_U, _C_Q), put_u), ((_C_Q, _C_KC), put_q), ((_C_KC, _C_KVS), put_cmp),
             ((_C_KVS, _C_KVW), put_keys(kvs_ref)), ((_C_KVW, _C_GN), put_keys(kvw_ref)),
             ((_C_GN, _C_GBR), put_head_gates)]
    stages = []
    for n in range(max(len(gate_cols), len(light))):
        if n < len(gate_cols):
            stages.append((gate_cols[n], put_branch_gates(gate_cols[n][0])))
        if n < len(light):
            stages.append(light[n])
    pending = proj(*stages[0][0])
    for n, (_, sink) in enumerate(stages):
        z = pending
        if n + 1 < len(stages):
            pending = proj(*stages[n + 1][0])
        sink(z)


def _ffn_mixin(x, shift0, scale0, gate0, g0, b0, g1, b1, w_in, w_out, shift1, scale1, w_mix, *, pre_ln):
    bsz, s, d = x.shape
    tm = min(ROW_TILE, s)

    def row(n, dt):
        return pl.BlockSpec((1, tm, n), lambda b, i: (b, i, 0)), jax.ShapeDtypeStruct((bsz, s, n), dt)

    def grouped(n, dt):
        return (pl.BlockSpec((1, N_KV, tm, n), lambda b, i: (b, 0, i, 0)),
                jax.ShapeDtypeStruct((bsz, N_KV, s, n), dt))

    outs = [row(d, F32), row(POOL_WIDTH, F32), row(Q_WIDTH, BF16), row(KV_WIDTH, F32), row(KV_WIDTH, F32),
            grouped(KEY_WIDTH, BF16), grouped(KEY_WIDTH, BF16), grouped(LANES, F32), row(2 * D_MODEL, BF16)]
    mod = pl.BlockSpec((1, 1, d), lambda b, i: (b, 0, 0))
    vec = _const_spec((1, d))
    return pl.pallas_call(
        functools.partial(_ffn_mixin_kernel, pre_ln=pre_ln),
        grid=(bsz, s // tm),
        in_specs=[row(d, F32)[0], mod, mod, mod, vec, vec, vec, vec,
                  _const_spec(w_in.shape), _const_spec(w_out.shape), mod, mod, _const_spec(w_mix.shape)],
        out_specs=[o[0] for o in outs],
        out_shape=[o[1] for o in outs],
        compiler_params=_params(2),
        name="ffn_mixer_in",
    )(x, shift0, scale0, gate0, g0.reshape(1, d), b0.reshape(1, d), g1.reshape(1, d), b1.reshape(1, d),
      w_in, w_out, shift1, scale1, w_mix)


def _compress_kernel(k_ref, v_ref, posk_ref, posv_ref, w1k_ref, w1v_ref, w2k_ref, w2v_ref, o_ref, nxt_ref):
    r = k_ref.shape[1] // CMP_STRIDE

    def hidden(x_ref, pos_ref, w1_ref):
        x = jnp.concatenate([x_ref[0, pl.ds(j, r, stride=CMP_STRIDE), :] for j in range(CMP_STRIDE)], axis=1)
        first = _dot((x + pos_ref[0:1, :]).astype(BF16), w1_ref[0])
        nxt_ref[0:r, :] = _dot((x + pos_ref[1:2, :]).astype(BF16), w1_ref[1])
        nxt_ref[r:r + 8, :] = jnp.zeros((8, nxt_ref.shape[1]), F32)
        return jax.nn.gelu(first + nxt_ref[1:r + 1, :]).astype(BF16)

    hk = hidden(k_ref, posk_ref, w1k_ref)
    hv = hidden(v_ref, posv_ref, w1v_ref)
    o_ref[0] = (_dot(hk, w2k_ref[...]) + _dot(hv, w2v_ref[...])).astype(BF16)


def _compress(kc, vc, posk, posv, w1k, w1v, w2k, w2v):
    bsz, s, n = kc.shape
    r = s // CMP_STRIDE
    hid = w1k.shape[2]
    row = pl.BlockSpec((1, s, n), lambda b: (b, 0, 0))
    return pl.pallas_call(
        _compress_kernel,
        grid=(bsz,),
        in_specs=[row, row, _const_spec(posk.shape), _const_spec(posv.shape),
                  _const_spec(w1k.shape), _const_spec(w1v.shape),
                  _const_spec(w2k.shape), _const_spec(w2v.shape)],
        out_specs=pl.BlockSpec((1, r, N_KV * KV_PACK), lambda b: (b, 0, 0)),
        out_shape=jax.ShapeDtypeStruct((bsz, r, N_KV * KV_PACK), BF16),
        scratch_shapes=[pltpu.VMEM((r + 8, hid), F32)],
        compiler_params=_params(1),
        name="compress_mlp",
    )(kc, vc, posk, posv, w1k, w1v, w2k, w2v)


def _alibi_slope(head):
    return 2.0 ** (-ALIBI_MAX * (head + 1) / N_HEADS)


def _padded_heads(q_tile):
    lane = lax.broadcasted_iota(jnp.int32, (Q_BLOCK, LANES), 1)
    qf = q_tile.astype(F32)
    heads = []
    for hd in range(N_HEADS):
        slab = qf[:, (hd // 2) * LANES:(hd // 2 + 1) * LANES]
        if hd % 2:
            slab = pltpu.roll(slab, HEAD_DIM, 1)
        heads.append(jnp.where(lane < HEAD_DIM, slab, 0.0))
    return heads


def _pack_heads(o_rows):
    lane = lax.broadcasted_iota(jnp.int32, (Q_BLOCK, LANES), 1)
    pairs = []
    for j in range(HEADS_PER_KV // 2):
        even = o_rows[(2 * j) * Q_BLOCK:(2 * j + 1) * Q_BLOCK]
        odd = o_rows[(2 * j + 1) * Q_BLOCK:(2 * j + 2) * Q_BLOCK]
        pairs.append(jnp.where(lane < HEAD_DIM, pltpu.roll(even, HEAD_DIM, 1), odd))
    return jnp.concatenate(pairs, axis=1)


def _feature_rows(qb, hd):
    lane_row = lax.broadcasted_iota(jnp.int32, (1, LANES), 1)
    slope = _alibi_slope(hd)
    tail = jnp.where(lane_row == FEAT_OFFSET_LANE, slope, 0.0)
    block_bias = slope * SLC_BLOCK * (lane_row - qb).astype(F32)
    in_window = (lane_row >= qb - WINDOW // SLC_BLOCK) & (lane_row <= qb)
    slc_row = jnp.where(lane_row < SLC_BLOCK, block_bias, tail)
    win_row = jnp.where(lane_row < SLC_BLOCK, jnp.where(in_window, block_bias, -BIG), tail)
    return slc_row, win_row


def _window_kernel(q_ref, kpw_ref, triw_ref, o_ref):
    gw = HEADS_PER_KV * HEAD_DIM
    chains = [(k, g) for k in range(WIN_NQ) for g in range(N_KV)]

    def scores(k, g):
        qb = pl.program_id(1) * WIN_NQ + k
        heads = _padded_heads(q_ref[0, k * Q_BLOCK:(k + 1) * Q_BLOCK, :])
        w0 = pl.multiple_of(jnp.maximum((qb + 1) * Q_BLOCK - WIN_KEYS, 0), Q_BLOCK)
        tri = jnp.concatenate([triw_ref[jnp.minimum(qb, WIN_KEYS // SLC_BLOCK - 1)]] * HEADS_PER_KV, axis=0)
        qw = [jnp.concatenate([heads[hd], jnp.broadcast_to(_feature_rows(qb, hd)[1], (Q_BLOCK, LANES))], axis=1)
              for hd in range(g * HEADS_PER_KV, (g + 1) * HEADS_PER_KV)]
        kp = kpw_ref[0, g, pl.ds(w0, WIN_KEYS), :]
        return _dot_nt(jnp.concatenate(qw, axis=0).astype(BF16), kp) + tri, kp

    def finish(k, g, o_raw, inv_l):
        o_ref[0, k * Q_BLOCK:(k + 1) * Q_BLOCK, g * gw:(g + 1) * gw] = _pack_heads(o_raw * inv_l).astype(BF16)

    pending = [scores(*chains[n]) for n in range(WIN_AHEAD)]
    unfinished = None
    for n, (k, g) in enumerate(chains):
        s, kp = pending.pop(0)
        if n + WIN_AHEAD < len(chains):
            pending.append(scores(*chains[n + WIN_AHEAD]))
        e = jnp.exp(s - jnp.max(s, axis=-1, keepdims=True))
        o_raw = _dot(e.astype(BF16), kp[:, 0:KV_PACK])
        inv_l = 1.0 / jnp.sum(e, axis=-1, keepdims=True)
        if unfinished is not None:
            finish(*unfinished)
        unfinished = (k, g, o_raw, inv_l)
    finish(*unfinished)


def _window(q, kpw, tri_win):
    bsz, s, _ = q.shape
    qt = WIN_NQ * Q_BLOCK
    return pl.pallas_call(
        _window_kernel,
        grid=(bsz, s // qt),
        in_specs=[pl.BlockSpec((1, qt, Q_WIDTH), lambda b, i: (b, i, 0)),
                  pl.BlockSpec((1, N_KV, s, KEY_WIDTH), lambda b, i: (b, 0, 0, 0)),
                  _const_spec(tri_win.shape)],
        out_specs=pl.BlockSpec((1, qt, Q_WIDTH), lambda b, i: (b, i, 0)),
        out_shape=jax.ShapeDtypeStruct((bsz, s, Q_WIDTH), BF16),
        compiler_params=_params(2),
        name="window_attention",
    )(q, kpw, tri_win)


def _select_kernel(q_ref, kvc_ref, cbias_ref, ovl_ref, ocmp_ref, selneg_ref, first_ref, score_ref, *, n_sel):
    step = pl.program_id(1)
    rows = SEL_NQ * HEADS_PER_KV * Q_BLOCK
    n_slc = ovl_ref.shape[0]
    r = kvc_ref.shape[1]
    lanes_q = SEL_NQ * N_KV * Q_BLOCK
    heads = [_padded_heads(q_ref[0, k * Q_BLOCK:(k + 1) * Q_BLOCK, :]) for k in range(SEL_NQ)]

    row = lax.broadcasted_iota(jnp.int32, (rows, 1), 0)
    t = (step * SEL_NQ + row // (HEADS_PER_KV * Q_BLOCK)) * Q_BLOCK + (row & (Q_BLOCK - 1))
    last_cmp = (t - (CMP_BLOCK - 1)) >> 4
    visible = lax.broadcasted_iota(jnp.int32, (rows, r), 1) <= last_cmp
    p_sums = [[None] * N_KV for _ in range(SEL_NQ)]
    gw = HEADS_PER_KV * HEAD_DIM
    def cmp_scores(g):
        q_pad = jnp.concatenate([heads[k][g * HEADS_PER_KV + h] for k in range(SEL_NQ)
                                 for h in range(HEADS_PER_KV)], axis=0).astype(BF16)
        return _dot_nt(q_pad, kvc_ref[0, :, g * KV_PACK:(g + 1) * KV_PACK])

    raw = [cmp_scores(g) for g in range(N_KV)]
    for g in range(N_KV):
        kvc = kvc_ref[0, :, g * KV_PACK:(g + 1) * KV_PACK]
        p_cmp = _masked_softmax(raw[g] + cbias_ref[g], visible)
        o_cmp = _dot(p_cmp.astype(BF16), kvc)
        for k in range(SEL_NQ):
            base = k * HEADS_PER_KV * Q_BLOCK
            ocmp_ref[0, k * Q_BLOCK:(k + 1) * Q_BLOCK, g * gw:(g + 1) * gw] = _pack_heads(
                o_cmp[base:base + HEADS_PER_KV * Q_BLOCK]).astype(BF16)
            p_sum = p_cmp[base:base + Q_BLOCK]
            for h in range(1, HEADS_PER_KV):
                p_sum = p_sum + p_cmp[base + h * Q_BLOCK:base + (h + 1) * Q_BLOCK]
            p_sums[k][g] = p_sum

    p_all = jnp.concatenate([p_sums[k][g] for k in range(SEL_NQ) for g in range(N_KV)], axis=0)
    p_hi = p_all.astype(BF16)
    p_lo = (p_all - p_hi.astype(F32)).astype(BF16)
    ovl = ovl_ref[...]
    imp_t = _dot_nt(ovl, p_hi) + _dot_nt(ovl, p_lo)

    blk = lax.broadcasted_iota(jnp.int32, (n_slc, lanes_q), 0)
    qb = step * SEL_NQ + lax.broadcasted_iota(jnp.int32, (n_slc, lanes_q), 1) // (N_KV * Q_BLOCK)
    forced = (blk == 0) | (blk == qb) | (blk == qb - 1)
    score = jnp.where(blk > qb, NEG, jnp.where(forced, FORCE, jnp.where(imp_t > 0.0, imp_t, 0.0)))
    key = lax.bitcast_convert_type(score, jnp.int32)
    key_prev = key - 1
    score_ref[...] = key

    def rank_step(i, ranks):
        out = []
        for u in range(RANK_UNROLL):
            row_i = i * RANK_UNROLL + u
            ri = score_ref[pl.ds(row_i, 1), :]
            out.append(ranks[u] + jnp.where(ri > jnp.where(blk > row_i, key_prev, key), 1.0, 0.0))
        return tuple(out)

    ranks = lax.fori_loop(0, (step * SEL_NQ + SEL_NQ - 1) // RANK_UNROLL + 1, rank_step,
                          (jnp.zeros((n_slc, lanes_q), F32),) * RANK_UNROLL)
    chosen = (functools.reduce(jnp.add, ranks) < n_sel) & (blk <= qb)

    oldest = jnp.where(chosen & (blk >= 1), blk, n_slc).astype(F32)
    sel_t = jnp.where(chosen, 1.0, 0.0).astype(BF16)
    sel_t = jnp.concatenate([sel_t, jnp.zeros((LANES - n_slc, lanes_q), BF16)], axis=0)
    eye = jnp.where(lax.broadcasted_iota(jnp.int32, (lanes_q, lanes_q), 0)
                    == lax.broadcasted_iota(jnp.int32, (lanes_q, lanes_q), 1), 1.0, 0.0).astype(BF16)
    sel = _dot_nt(eye, sel_t)
    lane = lax.broadcasted_iota(jnp.int32, (Q_BLOCK, LANES), 1)
    for k in range(SEL_NQ):
        lanes_k = slice(k * N_KV * Q_BLOCK, (k + 1) * N_KV * Q_BLOCK)
        first = jnp.min(jnp.min(oldest[:, lanes_k], axis=1, keepdims=True), axis=0, keepdims=True)
        first_ref[0, k] = jnp.broadcast_to(first, first_ref.shape[2:]).astype(jnp.int32)
        for g in range(N_KV):
            base = (k * N_KV + g) * Q_BLOCK
            selneg_ref[0, g, k * Q_BLOCK:(k + 1) * Q_BLOCK, :] = jnp.where(
                lane < SLC_BLOCK, (sel[base:base + Q_BLOCK] - 1.0) * BIG, 0.0).astype(BF16)


def _select(q, kvc, cbias, overlap_t, n_sel):
    bsz, s, _ = q.shape
    r = kvc.shape[1]
    n_slc = overlap_t.shape[0]
    qt = SEL_NQ * Q_BLOCK
    return pl.pallas_call(
        functools.partial(_select_kernel, n_sel=n_sel),
        grid=(bsz, s // qt),
        in_specs=[pl.BlockSpec((1, qt, Q_WIDTH), lambda b, i: (b, i, 0)),
                  pl.BlockSpec((1, r, N_KV * KV_PACK), lambda b, i: (b, 0, 0)),
                  _const_spec(cbias.shape), _const_spec(overlap_t.shape)],
        out_specs=[pl.BlockSpec((1, qt, Q_WIDTH), lambda b, i: (b, i, 0)),
                   pl.BlockSpec((1, N_KV, qt, LANES), lambda b, i: (b, 0, i, 0)),
                   pl.BlockSpec((1, SEL_NQ, 8, LANES), lambda b, i: (b, i, 0, 0))],
        out_shape=[jax.ShapeDtypeStruct((bsz, s, Q_WIDTH), BF16),
                   jax.ShapeDtypeStruct((bsz, N_KV, s, LANES), BF16),
                   jax.ShapeDtypeStruct((bsz, s // Q_BLOCK, 8, LANES), jnp.int32)],
        scratch_shapes=[pltpu.VMEM((n_slc, SEL_NQ * N_KV * Q_BLOCK), jnp.int32)],
        compiler_params=_params(2),
        name="compressed_attention_select",
    )(q, kvc, cbias, overlap_t)


def _selected_kernel(first_ref, q_ref, selneg_ref, gate_ref, ocmp_ref, owin_ref, kps_ref, tris_ref,
                     o_ref, qs_ref, s_ref, mt_ref, m_ref, lt_ref, acc_ref):
    qbs = [pl.program_id(1) * SLC_NQ + k for k in range(SLC_NQ)]
    rows = SLC_NQ * HEADS_PER_KV * Q_BLOCK
    heads = [_padded_heads(q_ref[0, k * Q_BLOCK:(k + 1) * Q_BLOCK, :]) for k in range(SLC_NQ)]
    for g in range(N_KV):
        qs = []
        for k in range(SLC_NQ):
            masked_out = selneg_ref[0, g, k * Q_BLOCK:(k + 1) * Q_BLOCK, :].astype(F32)
            qs += [jnp.concatenate([heads[k][hd], masked_out + _feature_rows(qbs[k], hd)[0]], axis=1)
                   for hd in range(g * HEADS_PER_KV, (g + 1) * HEADS_PER_KV)]
        qs_ref[g] = jnp.concatenate(qs, axis=0).astype(BF16)

    unit_blocks = SLC_UNIT // SLC_BLOCK
    diag_unit = qbs[0] // unit_blocks
    first = functools.reduce(jnp.minimum, [first_ref[pl.program_id(0), qb] for qb in qbs])
    first_unit = jnp.minimum(first, qbs[0]) // unit_blocks
    lead = jnp.minimum(first_unit, 1)
    n_units = diag_unit - first_unit + 1 + lead

    def unit_of(i):
        return jnp.where((i < lead) | (i >= n_units), 0, first_unit + i - lead)

    def unit_start(i):
        return pl.multiple_of(unit_of(i) * SLC_UNIT, SLC_UNIT)

    def slot_start(i):
        return pl.multiple_of(i * SLC_UNIT, SLC_UNIT)

    def trip_chains(first_slot, width):
        return [(g, first_slot + u) for u in range(width) for g in range(N_KV)]

    def score_matmul(g, i):
        return _dot_nt(qs_ref[g], kps_ref[0, g, pl.ds(unit_start(i), SLC_UNIT), :])

    def score_finish(g, i, raw):
        tri = []
        for qb in qbs:
            variant = jnp.where(i >= n_units, unit_blocks + 1,
                                jnp.where(unit_of(i) == diag_unit, qb % unit_blocks, unit_blocks))
            tri += [tris_ref[variant]] * HEADS_PER_KV
        s = (raw + jnp.concatenate(tri, axis=0)) * LOG2E
        s_ref[g, :, pl.ds(slot_start(i), SLC_UNIT)] = s
        mt_ref[g] = jnp.maximum(jnp.maximum(mt_ref[g], s[:, 0:LANES]), s[:, LANES:SLC_UNIT])

    def score_trip(first_slot, width):
        chains = trip_chains(first_slot, width)
        pending = score_matmul(*chains[0])
        for n, (g, i) in enumerate(chains):
            raw = pending
            if n + 1 < len(chains):
                pending = score_matmul(*chains[n + 1])
            score_finish(g, i, raw)

    def exponentials(g, i):
        m = m_ref[g]
        es = [jnp.exp2(s_ref[g, :, pl.ds(slot_start(i) + j * LANES, LANES)] - m) for j in range(SLC_UNIT // LANES)]
        lt_ref[g] = lt_ref[g] + functools.reduce(jnp.add, es)
        return jnp.concatenate(es, axis=1).astype(BF16)

    def value_trip(first_slot, width):
        chains = trip_chains(first_slot, width)
        pending = exponentials(*chains[0])
        for n, (g, i) in enumerate(chains):
            e = pending
            if n + 1 < len(chains):
                pending = exponentials(*chains[n + 1])
            acc_ref[g] = acc_ref[g] + _dot(e, kps_ref[0, g, pl.ds(unit_start(i), SLC_UNIT), 0:KV_PACK])

    def sweep(trip):
        done = 0
        for tier, width in enumerate(TRIP_WIDTHS):
            left = n_units - done
            if tier + 1 < len(TRIP_WIDTHS):
                count = left // width + jnp.where(left % width > TRIP_WIDTHS[tier + 1], 1, 0)
            else:
                count = (left + width - 1) // width
            count = jnp.maximum(count, 0)

            def body(p, carry, base=done, width=width):
                trip(base + p * width, width)
                return carry

            lax.fori_loop(0, count, body, 0)
            done = done + count * width

    mt_ref[...] = jnp.full(mt_ref.shape, M_INIT, F32)
    sweep(score_trip)
    for g in range(N_KV):
        m_ref[g] = jnp.broadcast_to(jnp.max(mt_ref[g], axis=-1, keepdims=True), (rows, LANES))
    lt_ref[...] = jnp.zeros(lt_ref.shape, F32)
    acc_ref[...] = jnp.zeros(acc_ref.shape, F32)
    sweep(value_trip)

    lane = lax.broadcasted_iota(jnp.int32, (Q_BLOCK, LANES), 1)
    for g in range(N_KV):
        l = jnp.sum(lt_ref[g], axis=-1, keepdims=True)
        o_rows = acc_ref[g] * jnp.where(l > 0.0, 1.0 / l, 0.0)
        for k in range(SLC_NQ):
            qr = slice(k * Q_BLOCK, (k + 1) * Q_BLOCK)
            o_slc = _pack_heads(o_rows[k * HEADS_PER_KV * Q_BLOCK:(k + 1) * HEADS_PER_KV * Q_BLOCK])
            gt = gate_ref[0, g, qr, :]
            for j in range(HEADS_PER_KV // 2):
                cols = slice((g * HEADS_PER_KV // 2 + j) * LANES, (g * HEADS_PER_KV // 2 + j + 1) * LANES)
                branches = (ocmp_ref[0, qr, cols].astype(F32), o_slc[:, j * LANES:(j + 1) * LANES],
                            owin_ref[0, qr, cols].astype(F32))
                total = None
                for c, branch in enumerate(branches):
                    even, odd = 3 * (2 * j) + c, 3 * (2 * j + 1) + c
                    gate = jnp.take_along_axis(gt, jnp.where(lane < HEAD_DIM, even, odd), axis=1)
                    total = gate * branch if total is None else total + gate * branch
                o_ref[0, qr, cols] = total.astype(BF16)


def _selected(first, q, selneg, gates, o_cmp, o_win, kps, tri_slc):
    bsz, s, _ = q.shape
    qt = SLC_NQ * Q_BLOCK
    rows = HEADS_PER_KV * qt
    assert (SLC_UNIT // SLC_BLOCK) % SLC_NQ == 0 and s % qt == 0

    def row(n):
        return pl.BlockSpec((1, qt, n), lambda b, i, first_ref: (b, i, 0))

    def grouped(n):
        return pl.BlockSpec((1, N_KV, qt, n), lambda b, i, first_ref: (b, 0, i, 0))

    grid_spec = pltpu.PrefetchScalarGridSpec(
        num_scalar_prefetch=1,
        grid=(bsz, s // qt),
        in_specs=[row(Q_WIDTH), grouped(LANES), grouped(LANES), row(Q_WIDTH), row(Q_WIDTH),
                  pl.BlockSpec((1, N_KV, s, KEY_WIDTH), lambda b, i, first_ref: (b, 0, 0, 0)),
                  _const_spec(tri_slc.shape)],
        out_specs=row(Q_WIDTH),
        scratch_shapes=[pltpu.VMEM((N_KV, rows, KEY_WIDTH), BF16),
                        pltpu.VMEM((N_KV, rows, s + SLC_UNIT), F32)]
        + [pltpu.VMEM((N_KV, rows, LANES), F32)] * 4)
    return pl.pallas_call(
        _selected_kernel,
        grid_spec=grid_spec,
        out_shape=jax.ShapeDtypeStruct((bsz, s, Q_WIDTH), BF16),
        compiler_params=_params(2),
        name="selected_attention",
    )(first, q, selneg, gates, o_cmp, o_win, kps, tri_slc)


def _attention(q, gates, kvc, kps, kpw):
    bsz, s, _ = q.shape
    r = kvc.shape[1]
    n_cmp = r - 1
    n_slc = s // SLC_BLOCK
    n_sel = min(N_SELECT, n_slc)
    rows = HEADS_PER_KV * Q_BLOCK
    assert n_slc <= SLC_BLOCK and n_slc % RANK_UNROLL == 0 and s % SLC_UNIT == 0 and s >= WIN_KEYS
    assert s % (WIN_NQ * Q_BLOCK) == 0 and s % (SEL_NQ * Q_BLOCK) == 0

    slopes = np.array([_alibi_slope(hd) for hd in range(N_HEADS)])
    slope_rows = np.repeat(slopes.reshape(N_KV, HEADS_PER_KV), Q_BLOCK, axis=1).reshape(N_KV, rows, 1)
    cbias = np.tile(slope_rows * (CMP_STRIDE * np.arange(r))[None, None, :], (1, SEL_NQ, 1))
    start = np.arange(r)[None, :] * CMP_STRIDE
    blk = np.arange(n_slc)[:, None] * SLC_BLOCK
    overlap_t = ((start < blk + SLC_BLOCK) & (start + CMP_BLOCK > blk) & (np.arange(r)[None, :] < n_cmp))

    ql = np.arange(Q_BLOCK)[:, None]
    kl = np.arange(SLC_BLOCK)[None, :]
    lower = np.where(kl > ql, -BIG, 0.0)
    upper = np.where(kl <= ql, -BIG, 0.0)
    unit_blocks = SLC_UNIT // SLC_BLOCK
    tri_slc = np.zeros((unit_blocks + 2, Q_BLOCK, SLC_UNIT))
    for j in range(unit_blocks):
        tri_slc[j, :, j * SLC_BLOCK:(j + 1) * SLC_BLOCK] = lower
    tri_slc[unit_blocks + 1] = -BIG
    win_blocks = WINDOW // SLC_BLOCK
    lead = WIN_KEYS // SLC_BLOCK - 1
    tri_win = np.zeros((lead + 1, Q_BLOCK, WIN_KEYS))
    for v in range(lead + 1):
        diag = v
        tri_win[v, :, diag * SLC_BLOCK:(diag + 1) * SLC_BLOCK] = lower
        if diag >= win_blocks:
            old = diag - win_blocks
            tri_win[v, :, old * SLC_BLOCK:(old + 1) * SLC_BLOCK] = upper

    o_win = _window(q, kpw, jnp.asarray(tri_win, F32))
    o_cmp, selneg, first = _select(q, kvc, jnp.asarray(cbias, F32), jnp.asarray(overlap_t, BF16), n_sel)
    return _selected(first[:, :, 0, 0], q, selneg, gates, o_cmp, o_win, kps, jnp.asarray(tri_slc, F32))


def _merge_ffn_kernel(x_ref, gate1_ref, uprev_ref, u_ref, o_ref, gbr_ref, pw_ref, ps_ref, wa_ref, wb_ref, wo_ref,
                      g2_ref, b2_ref, shift2_ref, scale2_ref, gate2_ref, g3_ref, b3_ref, w_in_ref, w_out_ref,
                      out_ref, ubuf_ref, sums_ref):
    i = pl.program_id(1)
    tm = u_ref.shape[1]
    ubuf_ref[0:POOL_HALO, :] = jnp.where(i == 0, 0.0, uprev_ref[0])
    ubuf_ref[POOL_HALO:POOL_HALO + tm, :] = u_ref[0]
    t = i * tm + lax.broadcasted_iota(jnp.int32, (tm, 1), 0)

    y_b = _dot(o_ref[0], wb_ref[...])

    end = POOL_HALO + tm
    totals = []
    for level in range(len(POOL_WINDOWS)):
        width = 1 << level
        start = 8 * (level + 1)
        c0 = level * POOL_GROUP
        src = (lambda r0, r1, c=c0: ubuf_ref[r0:r1, c:POOL_WIDTH]) if level == 0 else (
            lambda r0, r1, c=c0, n=level - 1: sums_ref[n, r0:r1, c:POOL_WIDTH])
        doubled = src(start, end) + src(start - width, end - width)
        if level + 1 < len(POOL_WINDOWS):
            sums_ref[level, start:end, c0:POOL_WIDTH] = doubled
        totals.append(doubled[POOL_HALO - start:, 0:POOL_GROUP])

    mixed = []
    for gi, w in enumerate(POOL_WINDOWS):
        assert w == 2 << gi
        cs = slice(gi * POOL_GROUP, (gi + 1) * POOL_GROUP)
        cur = ubuf_ref[POOL_HALO:POOL_HALO + tm, cs]
        total = totals[gi]
        inv_cnt = 1.0 / jnp.minimum(t + 1, w).astype(F32)
        delta = (total * inv_cnt - cur).astype(BF16)
        mixed.append((_dot(delta, pw_ref[gi]) * ps_ref[:, cs]).astype(BF16))
    y_a = _dot(jnp.concatenate(mixed, axis=1), wa_ref[...])
    d = y_a.shape[1]
    y = (gbr_ref[0, :, 0:d].astype(F32) * y_a + gbr_ref[0, :, d:2 * d].astype(F32) * y_b).astype(BF16)
    y = _dot(y, wo_ref[...])
    x2 = _layer_norm(ALPHA * x_ref[0] + gate1_ref[0] * y, g2_ref[...], b2_ref[...])
    out_ref[0] = _swiglu_block(x2, shift2_ref[0], scale2_ref[0], gate2_ref[0], w_in_ref, w_out_ref,
                               g3_ref[...], b3_ref[...])


def _merge_ffn(x, gate1, u, o, gbr, pool_w, pool_scale, w_a, w_b, w_o, g2, b2,
               shift2, scale2, gate2, g3, b3, w_in, w_out):
    bsz, s, d = x.shape
    tm = min(ROW_TILE, s)
    halo_blocks = tm // POOL_HALO

    def row(n):
        return pl.BlockSpec((1, tm, n), lambda bi, i: (bi, i, 0))

    mod = pl.BlockSpec((1, 1, d), lambda bi, i: (bi, 0, 0))
    vec = _const_spec((1, d))
    return pl.pallas_call(
        _merge_ffn_kernel,
        grid=(bsz, s // tm),
        in_specs=[row(d), mod,
                  pl.BlockSpec((1, POOL_HALO, POOL_WIDTH),
                               lambda bi, i: (bi, jnp.maximum(i * halo_blocks - 1, 0), 0)),
                  row(POOL_WIDTH), row(Q_WIDTH), row(2 * d),
                  _const_spec(pool_w.shape), _const_spec((1, POOL_WIDTH)),
                  _const_spec(w_a.shape), _const_spec(w_b.shape), _const_spec(w_o.shape),
                  vec, vec, mod, mod, mod, vec, vec, _const_spec(w_in.shape), _const_spec(w_out.shape)],
        out_specs=row(d),
        out_shape=jax.ShapeDtypeStruct((bsz, s, d), F32),
        scratch_shapes=[pltpu.VMEM((POOL_HALO + tm, POOL_WIDTH), F32),
                        pltpu.VMEM((len(POOL_WINDOWS) - 1, POOL_HALO + tm, POOL_WIDTH), F32)],
        compiler_params=_params(2),
        name="pool_merge_ffn",
    )(x, gate1, u, u, o, gbr, pool_w, pool_scale.reshape(1, POOL_WIDTH), w_a, w_b, w_o,
      g2.reshape(1, d), b2.reshape(1, d), shift2, scale2, gate2, g3.reshape(1, d), b3.reshape(1, d), w_in, w_out)


def _mixer_in_weights(w):
    sizes = (POOL_WIDTH, Q_WIDTH) + (KV_WIDTH,) * 6 + (3 * N_HEADS, 2 * D_MODEL)
    offs = np.concatenate([[0], np.cumsum(sizes)])
    u, q, k_cmp, v_cmp, k_slc, v_slc, k_win, v_win, g_nsa, g_br = [w[:, offs[i]:offs[i + 1]] for i in range(10)]
    cols = [u, q, k_cmp, v_cmp]
    for k, v in ((k_slc, v_slc), (k_win, v_win)):
        for g in range(N_KV):
            cols += [k[:, g * HEAD_DIM:(g + 1) * HEAD_DIM], v[:, g * HEAD_DIM:(g + 1) * HEAD_DIM]]
    per_g = 3 * HEADS_PER_KV
    for g in range(N_KV):
        cols += [g_nsa[:, g * per_g:(g + 1) * per_g], jnp.zeros((w.shape[0], LANES - per_g), w.dtype)]
    cols.append(g_br)
    return jnp.concatenate(cols, axis=1).astype(BF16)


def _compress_weights(pos, w1, w2, value_slot):
    same_group = np.eye(N_KV, dtype=np.float32)

    def expand_w1(half):
        wh = half.reshape(CMP_STRIDE, 1, HEAD_DIM, 1, CMP_HIDDEN)
        z = wh * same_group.reshape(1, N_KV, 1, N_KV, 1)
        return z.reshape(CMP_STRIDE * KV_WIDTH, N_KV * CMP_HIDDEN)

    half_rows = CMP_STRIDE * HEAD_DIM
    w1_big = jnp.stack([expand_w1(w1[:half_rows]), expand_w1(w1[half_rows:])]).astype(BF16)
    slot = np.eye(2, dtype=np.float32)[value_slot]
    w2_big = (w2.reshape(1, CMP_HIDDEN, 1, 1, HEAD_DIM) * same_group.reshape(N_KV, 1, N_KV, 1, 1)
              * slot.reshape(1, 1, 1, 2, 1))
    w2_big = w2_big.reshape(N_KV * CMP_HIDDEN, N_KV * KV_PACK).astype(BF16)
    pos_rows = jnp.broadcast_to(pos.reshape(2, CMP_STRIDE, 1, HEAD_DIM), (2, CMP_STRIDE, N_KV, HEAD_DIM))
    return pos_rows.reshape(2, CMP_STRIDE * KV_WIDTH), w1_big, w2_big


def kernel(x, c, ln_in_g, ln_in_b, w_ada, b_ada, ffn1_w_in, ffn1_w_out, ln1_g, ln1_b, w_mix_in, pool_w, pool_scale,
           cmp_pos_k, cmp_k_w1, cmp_k_w2, cmp_pos_v, cmp_v_w1, cmp_v_w2, w_branch_a, w_branch_b, w_mix_out,
           ln2_g, ln2_b, ffn2_w_in, ffn2_w_out, ln3_g, ln3_b):
    bsz, s, d = x.shape
    for l in range(DEPTH):
        ada = _ada(c, w_ada[l], b_ada[l]).reshape(bsz, 3, 3, 1, d)
        mod = lambda i, j: ada[:, i, j]

        x, u, q, kc, vc, kvs, kvw, gates, gbr = _ffn_mixin(
            x, mod(0, 0), mod(0, 1), mod(0, 2), ln_in_g, ln_in_b, ln1_g[l], ln1_b[l],
            ffn1_w_in[l].astype(BF16), ffn1_w_out[l].astype(BF16),
            mod(1, 0), mod(1, 1), _mixer_in_weights(w_mix_in[l]), pre_ln=l == 0)
        posk, w1k, w2k = _compress_weights(cmp_pos_k[l], cmp_k_w1[l], cmp_k_w2[l], 0)
        posv, w1v, w2v = _compress_weights(cmp_pos_v[l], cmp_v_w1[l], cmp_v_w2[l], 1)
        kvc = _compress(kc, vc, posk, posv, w1k, w1v, w2k, w2v)
        o = _attention(q, gates, kvc, kvs, kvw)
        x = _merge_ffn(x, mod(1, 2), u, o, gbr, pool_w[l].astype(BF16), pool_scale[l],
                       w_branch_a[l].astype(BF16), w_branch_b[l].astype(BF16), w_mix_out[l].astype(BF16),
                       ln2_g[l], ln2_b[l], mod(2, 0), mod(2, 1), mod(2, 2), ln3_g[l], ln3_b[l],
                       ffn2_w_in[l].astype(BF16), ffn2_w_out[l].astype(BF16))
    return x
```

```python
import functools

import numpy as np
import jax
import jax.numpy as jnp
from jax import lax
from jax.experimental import pallas as pl
from jax.experimental.pallas import tpu as pltpu

F32 = jnp.float32
BF16 = jnp.bfloat16

D_MODEL = 1024
POOL_WIDTH = D_MODEL // 2
POOL_WINDOWS = (2, 4, 8, 16)
POOL_GROUP = POOL_WIDTH // len(POOL_WINDOWS)
POOL_HALO = 32
HEAD_DIM = 64
N_HEADS = (D_MODEL // 2) // HEAD_DIM
N_KV = 2
HEADS_PER_KV = N_HEADS // N_KV
Q_WIDTH = N_HEADS * HEAD_DIM
KV_WIDTH = N_KV * HEAD_DIM
CMP_STRIDE = 16
CMP_BLOCK = 2 * CMP_STRIDE
CMP_HIDDEN = 4 * HEAD_DIM
SLC_BLOCK = 64
N_SELECT = 16
WINDOW = 512
Q_BLOCK = SLC_BLOCK
ALIBI_MAX = 8.0
D_FF = 2816
DEPTH = 1
ALPHA = (2.0 * DEPTH) ** 0.25
LN_EPS = 1e-5
NEG = -1e30
FORCE = 1e9

LANES = 128
KV_PACK = 2 * HEAD_DIM
KEY_WIDTH = 2 * KV_PACK
FEAT_OFFSET_LANE = SLC_BLOCK
BIG = 1e30
M_INIT = -3e38
FF_CHUNK = 256
GATE_CHUNK = 512
ROW_TILE = 512
SLC_UNIT = 256
RANK_UNROLL = 1
WIN_NQ = 8
WIN_AHEAD = 1
SEL_NQ = 4
SLC_NQ = 2
TRIP_WIDTHS = (8, 6, 4, 2)
LOG2E = 1.4426950408889634
WIN_KEYS = WINDOW + 2 * Q_BLOCK
VMEM_LIMIT = 56 * 1024 * 1024

_C_U = 0
_C_Q = _C_U + POOL_WIDTH
_C_KC = _C_Q + Q_WIDTH
_C_VC = _C_KC + KV_WIDTH
_C_KVS = _C_VC + KV_WIDTH
_C_KVW = _C_KVS + N_KV * KV_PACK
_C_GN = _C_KVW + N_KV * KV_PACK
_C_GBR = _C_GN + N_KV * LANES
_C_END = _C_GBR + 2 * D_MODEL


def _dot(a, b):
    return jnp.dot(a, b, preferred_element_type=F32)


def _dot_nt(a, b):
    return lax.dot_general(a, b, (((1,), (1,)), ((), ())), preferred_element_type=F32)


def _layer_norm(x, g, b):
    mu = jnp.mean(x, axis=-1, keepdims=True)
    xc = x - mu
    var = jnp.mean(xc * xc, axis=-1, keepdims=True)
    return xc * lax.rsqrt(var + LN_EPS) * g + b


def _masked_softmax(s, mask):
    sm = jnp.where(mask, s, NEG)
    m = jnp.max(sm, axis=-1, keepdims=True)
    e = jnp.exp(sm - m)
    p = e * (1.0 / jnp.sum(e, axis=-1, keepdims=True))
    return jnp.where(mask, p, 0.0)


def _const_spec(shape):
    nd = len(shape)
    return pl.BlockSpec(shape, lambda *_: (0,) * nd, pipeline_mode=pl.Buffered(1))


def _params(n_grid):
    return pltpu.CompilerParams(dimension_semantics=("parallel",) * n_grid, vmem_limit_bytes=VMEM_LIMIT)


def _ada_kernel(c_ref, w_ref, b_ref, o_ref):
    c = c_ref[...]
    c_act = (c * jax.nn.sigmoid(c)).astype(BF16)
    o_ref[...] = _dot(c_act, w_ref[...].astype(BF16)) + b_ref[...]


def _ada(c, w, b):
    bsz, d = c.shape
    n = w.shape[1]
    tn = D_MODEL
    return pl.pallas_call(
        _ada_kernel,
        grid=(n // tn,),
        in_specs=[pl.BlockSpec((bsz, d), lambda j: (0, 0)),
                  pl.BlockSpec((d, tn), lambda j: (0, j)),
                  pl.BlockSpec((1, tn), lambda j: (0, j))],
        out_specs=pl.BlockSpec((bsz, tn), lambda j: (0, j)),
        out_shape=jax.ShapeDtypeStruct((bsz, n), F32),
        compiler_params=_params(1),
        name="ada_proj",
    )(c, w, b.reshape(1, n))


def _swiglu(h, w_in_ref, w_out_ref):
    acc = jnp.zeros((h.shape[0], w_out_ref.shape[1]), F32)
    for j in range(D_FF // FF_CHUNK):
        c0 = j * FF_CHUNK
        gt = _dot(h, w_in_ref[:, c0:c0 + FF_CHUNK])
        up = _dot(h, w_in_ref[:, D_FF + c0:D_FF + c0 + FF_CHUNK])
        act = (gt * jax.nn.sigmoid(gt) * up).astype(BF16)
        acc = acc + _dot(act, w_out_ref[c0:c0 + FF_CHUNK, :])
    return acc


def _swiglu_block(x, shift, scale, gate, w_in_ref, w_out_ref, g, b):
    h = (x * (1.0 + scale) + shift).astype(BF16)
    return _layer_norm(ALPHA * x + 0.5 * gate * _swiglu(h, w_in_ref, w_out_ref), g, b)


def _ffn_mixin_kernel(x_ref, shift0_ref, scale0_ref, gate0_ref, g0_ref, b0_ref, g1_ref, b1_ref, w_in_ref, w_out_ref,
                      shift1_ref, scale1_ref, w_ref,
                      x1_ref, u_ref, q_ref, kc_ref, vc_ref, kvs_ref, kvw_ref, gn_ref, gbr_ref, *, pre_ln):
    x = x_ref[0]
    if pre_ln:
        x = _layer_norm(x, g0_ref[...], b0_ref[...])
    x1 = _swiglu_block(x, shift0_ref[0], scale0_ref[0], gate0_ref[0], w_in_ref, w_out_ref, g1_ref[...], b1_ref[...])
    x1_ref[0] = x1

    h = (x1 * (1.0 + scale1_ref[0]) + shift1_ref[0]).astype(BF16)

    def proj(c0, c1):
        return _dot(h, w_ref[:, c0:c1])

    tm = x_ref.shape[1]
    pos = pl.program_id(1) * tm + lax.broadcasted_iota(jnp.int32, (tm, LANES), 0)
    lane = lax.broadcasted_iota(jnp.int32, (tm, LANES), 1)
    feat = jnp.where(lane == pos // SLC_BLOCK, 1.0,
                     jnp.where(lane == FEAT_OFFSET_LANE, (pos % SLC_BLOCK).astype(F32), 0.0)).astype(BF16)

    def put_u(z):
        u_ref[0] = z

    def put_q(z):
        q_ref[0] = (z * HEAD_DIM ** -0.5).astype(BF16)

    def put_cmp(z):
        kc_ref[0] = z[:, 0:KV_WIDTH]
        vc_ref[0] = z[:, KV_WIDTH:2 * KV_WIDTH]

    def put_keys(ref):
        def put(z):
            zb = z.astype(BF16)
            for g in range(N_KV):
                ref[0, g, :, 0:KV_PACK] = zb[:, g * KV_PACK:(g + 1) * KV_PACK]
                ref[0, g, :, KV_PACK:KEY_WIDTH] = feat
        return put

    def put_head_gates(z):
        sg = jax.nn.sigmoid(z)
        for g in range(N_KV):
            gn_ref[0, g] = sg[:, g * LANES:(g + 1) * LANES]

    def put_branch_gates(c0):
        def put(z):
            gbr_ref[0, :, c0 - _C_GBR:c0 - _C_GBR + z.shape[1]] = jax.nn.sigmoid(z).astype(BF16)
        return put

    gate_cols = [(c0, c0 + GATE_CHUNK) for c0 in range(_C_GBR, _C_END, GATE_CHUNK)]
    light = [((_C_U, _C_Q), put_u), ((_C_Q, _C_KC), put_q), ((_C_KC, _C_KVS), put_cmp),
             ((_C_KVS, _C_KVW), put_keys(kvs_ref)), ((_C_KVW, _C_GN), put_keys(kvw_ref)),
             ((_C_GN, _C_GBR), put_head_gates)]
    stages = []
    for n in range(max(len(gate_cols), len(light))):
        if n < len(gate_cols):
            stages.append((gate_cols[n], put_branch_gates(gate_cols[n][0])))
        if n < len(light):
            stages.append(light[n])
    pending = proj(*stages[0][0])
    for n, (_, sink) in enumerate(stages):
        z = pending
        if n + 1 < len(stages):
            pending = proj(*stages[n + 1][0])
        sink(z)


def _ffn_mixin(x, shift0, scale0, gate0, g0, b0, g1, b1, w_in, w_out, shift1, scale1, w_mix, *, pre_ln):
    bsz, s, d = x.shape
    tm = min(ROW_TILE, s)

    def row(n, dt):
        return pl.BlockSpec((1, tm, n), lambda b, i: (b, i, 0)), jax.ShapeDtypeStruct((bsz, s, n), dt)

    def grouped(n, dt):
        return (pl.BlockSpec((1, N_KV, tm, n), lambda b, i: (b, 0, i, 0)),
                jax.ShapeDtypeStruct((bsz, N_KV, s, n), dt))

    outs = [row(d, F32), row(POOL_WIDTH, F32), row(Q_WIDTH, BF16), row(KV_WIDTH, F32), row(KV_WIDTH, F32),
            grouped(KEY_WIDTH, BF16), grouped(KEY_WIDTH, BF16), grouped(LANES, F32), row(2 * D_MODEL, BF16)]
    mod = pl.BlockSpec((1, 1, d), lambda b, i: (b, 0, 0))
    vec = _const_spec((1, d))
    return pl.pallas_call(
        functools.partial(_ffn_mixin_kernel, pre_ln=pre_ln),
        grid=(bsz, s // tm),
        in_specs=[row(d, F32)[0], mod, mod, mod, vec, vec, vec, vec,
                  _const_spec(w_in.shape), _const_spec(w_out.shape), mod, mod, _const_spec(w_mix.shape)],
        out_specs=[o[0] for o in outs],
        out_shape=[o[1] for o in outs],
        compiler_params=_params(2),
        name="ffn_mixer_in",
    )(x, shift0, scale0, gate0, g0.reshape(1, d), b0.reshape(1, d), g1.reshape(1, d), b1.reshape(1, d),
      w_in, w_out, shift1, scale1, w_mix)


def _compress_kernel(k_ref, v_ref, posk_ref, posv_ref, w1k_ref, w1v_ref, w2k_ref, w2v_ref, o_ref, nxt_ref):
    r = k_ref.shape[1] // CMP_STRIDE

    def hidden(x_ref, pos_ref, w1_ref):
        x = jnp.concatenate([x_ref[0, pl.ds(j, r, stride=CMP_STRIDE), :] for j in range(CMP_STRIDE)], axis=1)
        first = _dot((x + pos_ref[0:1, :]).astype(BF16), w1_ref[0])
        nxt_ref[0:r, :] = _dot((x + pos_ref[1:2, :]).astype(BF16), w1_ref[1])
        nxt_ref[r:r + 8, :] = jnp.zeros((8, nxt_ref.shape[1]), F32)
        return jax.nn.gelu(first + nxt_ref[1:r + 1, :]).astype(BF16)

    hk = hidden(k_ref, posk_ref, w1k_ref)
    hv = hidden(v_ref, posv_ref, w1v_ref)
    o_ref[0] = (_dot(hk, w2k_ref[...]) + _dot(hv, w2v_ref[...])).astype(BF16)


def _compress(kc, vc, posk, posv, w1k, w1v, w2k, w2v):
    bsz, s, n = kc.shape
    r = s // CMP_STRIDE
    hid = w1k.shape[2]
    row = pl.BlockSpec((1, s, n), lambda b: (b, 0, 0))
    return pl.pallas_call(
        _compress_kernel,
        grid=(bsz,),
        in_specs=[row, row, _const_spec(posk.shape), _const_spec(posv.shape),
                  _const_spec(w1k.shape), _const_spec(w1v.shape),
                  _const_spec(w2k.shape), _const_spec(w2v.shape)],
        out_specs=pl.BlockSpec((1, r, N_KV * KV_PACK), lambda b: (b, 0, 0)),
        out_shape=jax.ShapeDtypeStruct((bsz, r, N_KV * KV_PACK), BF16),
        scratch_shapes=[pltpu.VMEM((r + 8, hid), F32)],
        compiler_params=_params(1),
        name="compress_mlp",
    )(kc, vc, posk, posv, w1k, w1v, w2k, w2v)


def _alibi_slope(head):
    return 2.0 ** (-ALIBI_MAX * (head + 1) / N_HEADS)


def _padded_heads(q_tile):
    lane = lax.broadcasted_iota(jnp.int32, (Q_BLOCK, LANES), 1)
    qf = q_tile.astype(F32)
    heads = []
    for hd in range(N_HEADS):
        slab = qf[:, (hd // 2) * LANES:(hd // 2 + 1) * LANES]
        if hd % 2:
            slab = pltpu.roll(slab, HEAD_DIM, 1)
        heads.append(jnp.where(lane < HEAD_DIM, slab, 0.0))
    return heads


def _pack_heads(o_rows):
    lane = lax.broadcasted_iota(jnp.int32, (Q_BLOCK, LANES), 1)
    pairs = []
    for j in range(HEADS_PER_KV // 2):
        even = o_rows[(2 * j) * Q_BLOCK:(2 * j + 1) * Q_BLOCK]
        odd = o_rows[(2 * j + 1) * Q_BLOCK:(2 * j + 2) * Q_BLOCK]
        pairs.append(jnp.where(lane < HEAD_DIM, pltpu.roll(even, HEAD_DIM, 1), odd))
    return jnp.concatenate(pairs, axis=1)


def _feature_rows(qb, hd):
    lane_row = lax.broadcasted_iota(jnp.int32, (1, LANES), 1)
    slope = _alibi_slope(hd)
    tail = jnp.where(lane_row == FEAT_OFFSET_LANE, slope, 0.0)
    block_bias = slope * SLC_BLOCK * (lane_row - qb).astype(F32)
    in_window = (lane_row >= qb - WINDOW // SLC_BLOCK) & (lane_row <= qb)
    slc_row = jnp.where(lane_row < SLC_BLOCK, block_bias, tail)
    win_row = jnp.where(lane_row < SLC_BLOCK, jnp.where(in_window, block_bias, -BIG), tail)
    return slc_row, win_row


def _window_kernel(q_ref, kpw_ref, triw_ref, o_ref):
    gw = HEADS_PER_KV * HEAD_DIM
    chains = [(k, g) for k in range(WIN_NQ) for g in range(N_KV)]

    def scores(k, g):
        qb = pl.program_id(1) * WIN_NQ + k
        heads = _padded_heads(q_ref[0, k * Q_BLOCK:(k + 1) * Q_BLOCK, :])
        w0 = pl.multiple_of(jnp.maximum((qb + 1) * Q_BLOCK - WIN_KEYS, 0), Q_BLOCK)
        tri = jnp.concatenate([triw_ref[jnp.minimum(qb, WIN_KEYS // SLC_BLOCK - 1)]] * HEADS_PER_KV, axis=0)
        qw = [jnp.concatenate([heads[hd], jnp.broadcast_to(_feature_rows(qb, hd)[1], (Q_BLOCK, LANES))], axis=1)
              for hd in range(g * HEADS_PER_KV, (g + 1) * HEADS_PER_KV)]
        kp = kpw_ref[0, g, pl.ds(w0, WIN_KEYS), :]
        return _dot_nt(jnp.concatenate(qw, axis=0).astype(BF16), kp) + tri, kp

    def finish(k, g, o_raw, inv_l):
        o_ref[0, k * Q_BLOCK:(k + 1) * Q_BLOCK, g * gw:(g + 1) * gw] = _pack_heads(o_raw * inv_l).astype(BF16)

    pending = [scores(*chains[n]) for n in range(WIN_AHEAD)]
    unfinished = None
    for n, (k, g) in enumerate(chains):
        s, kp = pending.pop(0)
        if n + WIN_AHEAD < len(chains):
            pending.append(scores(*chains[n + WIN_AHEAD]))
        e = jnp.exp(s - jnp.max(s, axis=-1, keepdims=True))
        o_raw = _dot(e.astype(BF16), kp[:, 0:KV_PACK])
        inv_l = 1.0 / jnp.sum(e, axis=-1, keepdims=True)
        if unfinished is not None:
            finish(*unfinished)
        unfinished = (k, g, o_raw, inv_l)
    finish(*unfinished)


def _window(q, kpw, tri_win):
    bsz, s, _ = q.shape
    qt = WIN_NQ * Q_BLOCK
    return pl.pallas_call(
        _window_kernel,
        grid=(bsz, s // qt),
        in_specs=[pl.BlockSpec((1, qt, Q_WIDTH), lambda b, i: (b, i, 0)),
                  pl.BlockSpec((1, N_KV, s, KEY_WIDTH), lambda b, i: (b, 0, 0, 0)),
                  _const_spec(tri_win.shape)],
        out_specs=pl.BlockSpec((1, qt, Q_WIDTH), lambda b, i: (b, i, 0)),
        out_shape=jax.ShapeDtypeStruct((bsz, s, Q_WIDTH), BF16),
        compiler_params=_params(2),
        name="window_attention",
    )(q, kpw, tri_win)


def _select_kernel(q_ref, kvc_ref, cbias_ref, ovl_ref, ocmp_ref, selneg_ref, first_ref, score_ref, *, n_sel):
    step = pl.program_id(1)
    rows = SEL_NQ * HEADS_PER_KV * Q_BLOCK
    n_slc = ovl_ref.shape[0]
    r = kvc_ref.shape[1]
    lanes_q = SEL_NQ * N_KV * Q_BLOCK
    heads = [_padded_heads(q_ref[0, k * Q_BLOCK:(k + 1) * Q_BLOCK, :]) for k in range(SEL_NQ)]

    row = lax.broadcasted_iota(jnp.int32, (rows, 1), 0)
    t = (step * SEL_NQ + row // (HEADS_PER_KV * Q_BLOCK)) * Q_BLOCK + (row & (Q_BLOCK - 1))
    last_cmp = (t - (CMP_BLOCK - 1)) >> 4
    visible = lax.broadcasted_iota(jnp.int32, (rows, r), 1) <= last_cmp
    p_sums = [[None] * N_KV for _ in range(SEL_NQ)]
    gw = HEADS_PER_KV * HEAD_DIM
    def cmp_scores(g):
        q_pad = jnp.concatenate([heads[k][g * HEADS_PER_KV + h] for k in range(SEL_NQ)
                                 for h in range(HEADS_PER_KV)], axis=0).astype(BF16)
        return _dot_nt(q_pad, kvc_ref[0, :, g * KV_PACK:(g + 1) * KV_PACK])

    raw = [cmp_scores(g) for g in range(N_KV)]
    for g in range(N_KV):
        kvc = kvc_ref[0, :, g * KV_PACK:(g + 1) * KV_PACK]
        p_cmp = _masked_softmax(raw[g] + cbias_ref[g], visible)
        o_cmp = _dot(p_cmp.astype(BF16), kvc)
        for k in range(SEL_NQ):
            base = k * HEADS_PER_KV * Q_BLOCK
            ocmp_ref[0, k * Q_BLOCK:(k + 1) * Q_BLOCK, g * gw:(g + 1) * gw] = _pack_heads(
                o_cmp[base:base + HEADS_PER_KV * Q_BLOCK]).astype(BF16)
            p_sum = p_cmp[base:base + Q_BLOCK]
            for h in range(1, HEADS_PER_KV):
                p_sum = p_sum + p_cmp[base + h * Q_BLOCK:base + (h + 1) * Q_BLOCK]
            p_sums[k][g] = p_sum

    p_all = jnp.concatenate([p_sums[k][g] for k in range(SEL_NQ) for g in range(N_KV)], axis=0)
    p_hi = p_all.astype(BF16)
    p_lo = (p_all - p_hi.astype(F32)).astype(BF16)
    ovl = ovl_ref[...]
    imp_t = _dot_nt(ovl, p_hi) + _dot_nt(ovl, p_lo)

    blk = lax.broadcasted_iota(jnp.int32, (n_slc, lanes_q), 0)
    qb = step * SEL_NQ + lax.broadcasted_iota(jnp.int32, (n_slc, lanes_q), 1) // (N_KV * Q_BLOCK)
    forced = (blk == 0) | (blk == qb) | (blk == qb - 1)
    score = jnp.where(blk > qb, NEG, jnp.where(forced, FORCE, jnp.where(imp_t > 0.0, imp_t, 0.0)))
    key = lax.bitcast_convert_type(score, jnp.int32)
    key_prev = key - 1
    score_ref[...] = key

    def rank_step(i, ranks):
        out = []
        for u in range(RANK_UNROLL):
            row_i = i * RANK_UNROLL + u
            ri = score_ref[pl.ds(row_i, 1), :]
            out.append(ranks[u] + jnp.where(ri > jnp.where(blk > row_i, key_prev, key), 1.0, 0.0))
        return tuple(out)

    ranks = lax.fori_loop(0, (step * SEL_NQ + SEL_NQ - 1) // RANK_UNROLL + 1, rank_step,
                          (jnp.zeros((n_slc, lanes_q), F32),) * RANK_UNROLL)
    chosen = (functools.reduce(jnp.add, ranks) < n_sel) & (blk <= qb)

    oldest = jnp.where(chosen & (blk >= 1), blk, n_slc).astype(F32)
    sel_t = jnp.where(chosen, 1.0, 0.0).astype(BF16)
    sel_t = jnp.concatenate([sel_t, jnp.zeros((LANES - n_slc, lanes_q), BF16)], axis=0)
    eye = jnp.where(lax.broadcasted_iota(jnp.int32, (lanes_q, lanes_q), 0)
                    == lax.broadcasted_iota(jnp.int32, (lanes_q, lanes_q), 1), 1.0, 0.0).astype(BF16)
    sel = _dot_nt(eye, sel_t)
    lane = lax.broadcasted_iota(jnp.int32, (Q_BLOCK, LANES), 1)
    for k in range(SEL_NQ):
        lanes_k = slice(k * N_KV * Q_BLOCK, (k + 1) * N_KV * Q_BLOCK)
        first = jnp.min(jnp.min(oldest[:, lanes_k], axis=1, keepdims=True), axis=0, keepdims=True)
        first_ref[0, k] = jnp.broadcast_to(first, first_ref.shape[2:]).astype(jnp.int32)
        for g in range(N_KV):
            base = (k * N_KV + g) * Q_BLOCK
            selneg_ref[0, g, k * Q_BLOCK:(k + 1) * Q_BLOCK, :] = jnp.where(
                lane < SLC_BLOCK, (sel[base:base + Q_BLOCK] - 1.0) * BIG, 0.0).astype(BF16)


def _select(q, kvc, cbias, overlap_t, n_sel):
    bsz, s, _ = q.shape
    r = kvc.shape[1]
    n_slc = overlap_t.shape[0]
    qt = SEL_NQ * Q_BLOCK
    return pl.pallas_call(
        functools.partial(_select_kernel, n_sel=n_sel),
        grid=(bsz, s // qt),
        in_specs=[pl.BlockSpec((1, qt, Q_WIDTH), lambda b, i: (b, i, 0)),
                  pl.BlockSpec((1, r, N_KV * KV_PACK), lambda b, i: (b, 0, 0)),
                  _const_spec(cbias.shape), _const_spec(overlap_t.shape)],
        out_specs=[pl.BlockSpec((1, qt, Q_WIDTH), lambda b, i: (b, i, 0)),
                   pl.BlockSpec((1, N_KV, qt, LANES), lambda b, i: (b, 0, i, 0)),
                   pl.BlockSpec((1, SEL_NQ, 8, LANES), lambda b, i: (b, i, 0, 0))],
        out_shape=[jax.ShapeDtypeStruct((bsz, s, Q_WIDTH), BF16),
                   jax.ShapeDtypeStruct((bsz, N_KV, s, LANES), BF16),
                   jax.ShapeDtypeStruct((bsz, s // Q_BLOCK, 8, LANES), jnp.int32)],
        scratch_shapes=[pltpu.VMEM((n_slc, SEL_NQ * N_KV * Q_BLOCK), jnp.int32)],
        compiler_params=_params(2),
        name="compressed_attention_select",
    )(q, kvc, cbias, overlap_t)


def _selected_kernel(first_ref, q_ref, selneg_ref, gate_ref, ocmp_ref, owin_ref, kps_ref, tris_ref,
                     o_ref, qs_ref, s_ref, mt_ref, m_ref, lt_ref, acc_ref):
    qbs = [pl.program_id(1) * SLC_NQ + k for k in range(SLC_NQ)]
    rows = SLC_NQ * HEADS_PER_KV * Q_BLOCK
    heads = [_padded_heads(q_ref[0, k * Q_BLOCK:(k + 1) * Q_BLOCK, :]) for k in range(SLC_NQ)]
    for g in range(N_KV):
        qs = []
        for k in range(SLC_NQ):
            masked_out = selneg_ref[0, g, k * Q_BLOCK:(k + 1) * Q_BLOCK, :].astype(F32)
            qs += [jnp.concatenate([heads[k][hd], masked_out + _feature_rows(qbs[k], hd)[0]], axis=1)
                   for hd in range(g * HEADS_PER_KV, (g + 1) * HEADS_PER_KV)]
        qs_ref[g] = jnp.concatenate(qs, axis=0).astype(BF16)

    unit_blocks = SLC_UNIT // SLC_BLOCK
    diag_unit = qbs[0] // unit_blocks
    first = functools.reduce(jnp.minimum, [first_ref[pl.program_id(0), qb] for qb in qbs])
    first_unit = jnp.minimum(first, qbs[0]) // unit_blocks
    lead = jnp.minimum(first_unit, 1)
    n_units = diag_unit - first_unit + 1 + lead

    def unit_of(i):
        return jnp.where((i < lead) | (i >= n_units), 0, first_unit + i - lead)

    def unit_start(i):
        return pl.multiple_of(unit_of(i) * SLC_UNIT, SLC_UNIT)

    def slot_start(i):
        return pl.multiple_of(i * SLC_UNIT, SLC_UNIT)

    def trip_chains(first_slot, width):
        return [(g, first_slot + u) for u in range(width) for g in range(N_KV)]

    def score_matmul(g, i):
        return _dot_nt(qs_ref[g], kps_ref[0, g, pl.ds(unit_start(i), SLC_UNIT), :])

    def score_finish(g, i, raw):
        tri = []
        for qb in qbs:
            variant = jnp.where(i >= n_units, unit_blocks + 1,
                                jnp.where(unit_of(i) == diag_unit, qb % unit_blocks, unit_blocks))
            tri += [tris_ref[variant]] * HEADS_PER_KV
        s = (raw + jnp.concatenate(tri, axis=0)) * LOG2E
        s_ref[g, :, pl.ds(slot_start(i), SLC_UNIT)] = s
        mt_ref[g] = jnp.maximum(jnp.maximum(mt_ref[g], s[:, 0:LANES]), s[:, LANES:SLC_UNIT])

    def score_trip(first_slot, width):
        chains = trip_chains(first_slot, width)
        pending = score_matmul(*chains[0])
        for n, (g, i) in enumerate(chains):
            raw = pending
            if n + 1 < len(chains):
                pending = score_matmul(*chains[n + 1])
            score_finish(g, i, raw)

    def exponentials(g, i):
        m = m_ref[g]
        es = [jnp.exp2(s_ref[g, :, pl.ds(slot_start(i) + j * LANES, LANES)] - m) for j in range(SLC_UNIT // LANES)]
        lt_ref[g] = lt_ref[g] + functools.reduce(jnp.add, es)
        return jnp.concatenate(es, axis=1).astype(BF16)

    def value_trip(first_slot, width):
        chains = trip_chains(first_slot, width)
        pending = exponentials(*chains[0])
        for n, (g, i) in enumerate(chains):
            e = pending
            if n + 1 < len(chains):
                pending = exponentials(*chains[n + 1])
            acc_ref[g] = acc_ref[g] + _dot(e, kps_ref[0, g, pl.ds(unit_start(i), SLC_UNIT), 0:KV_PACK])

    def sweep(trip):
        done = 0
        for tier, width in enumerate(TRIP_WIDTHS):
            left = n_units - done
            if tier + 1 < len(TRIP_WIDTHS):
                count = left // width + jnp.where(left % width > TRIP_WIDTHS[tier + 1], 1, 0)
            else:
                count = (left + width - 1) // width
            count = jnp.maximum(count, 0)

            def body(p, carry, base=done, width=width):
                trip(base + p * width, width)
                return carry

            lax.fori_loop(0, count, body, 0)
            done = done + count * width

    mt_ref[...] = jnp.full(mt_ref.shape, M_INIT, F32)
    sweep(score_trip)
    for g in range(N_KV):
        m_ref[g] = jnp.broadcast_to(jnp.max(mt_ref[g], axis=-1, keepdims=True), (rows, LANES))
    lt_ref[...] = jnp.zeros(lt_ref.shape, F32)
    acc_ref[...] = jnp.zeros(acc_ref.shape, F32)
    sweep(value_trip)

    lane = lax.broadcasted_iota(jnp.int32, (Q_BLOCK, LANES), 1)
    for g in range(N_KV):
        l = jnp.sum(lt_ref[g], axis=-1, keepdims=True)
        o_rows = acc_ref[g] * jnp.where(l > 0.0, 1.0 / l, 0.0)
        for k in range(SLC_NQ):
            qr = slice(k * Q_BLOCK, (k + 1) * Q_BLOCK)
            o_slc = _pack_heads(o_rows[k * HEADS_PER_KV * Q_BLOCK:(k + 1) * HEADS_PER_KV * Q_BLOCK])
            gt = gate_ref[0, g, qr, :]
            for j in range(HEADS_PER_KV // 2):
                cols = slice((g * HEADS_PER_KV // 2 + j) * LANES, (g * HEADS_PER_KV // 2 + j + 1) * LANES)
                branches = (ocmp_ref[0, qr, cols].astype(F32), o_slc[:, j * LANES:(j + 1) * LANES],
                            owin_ref[0, qr, cols].astype(F32))
                total = None
                for c, branch in enumerate(branches):
                    even, odd = 3 * (2 * j) + c, 3 * (2 * j + 1) + c
                    gate = jnp.take_along_axis(gt, jnp.where(lane < HEAD_DIM, even, odd), axis=1)
                    total = gate * branch if total is None else total + gate * branch
                o_ref[0, qr, cols] = total.astype(BF16)


def _selected(first, q, selneg, gates, o_cmp, o_win, kps, tri_slc):
    bsz, s, _ = q.shape
    qt = SLC_NQ * Q_BLOCK
    rows = HEADS_PER_KV * qt
    assert (SLC_UNIT // SLC_BLOCK) % SLC_NQ == 0 and s % qt == 0

    def row(n):
        return pl.BlockSpec((1, qt, n), lambda b, i, first_ref: (b, i, 0))

    def grouped(n):
        return pl.BlockSpec((1, N_KV, qt, n), lambda b, i, first_ref: (b, 0, i, 0))

    grid_spec = pltpu.PrefetchScalarGridSpec(
        num_scalar_prefetch=1,
        grid=(bsz, s // qt),
        in_specs=[row(Q_WIDTH), grouped(LANES), grouped(LANES), row(Q_WIDTH), row(Q_WIDTH),
                  pl.BlockSpec((1, N_KV, s, KEY_WIDTH), lambda b, i, first_ref: (b, 0, 0, 0)),
                  _const_spec(tri_slc.shape)],
        out_specs=row(Q_WIDTH),
        scratch_shapes=[pltpu.VMEM((N_KV, rows, KEY_WIDTH), BF16),
                        pltpu.VMEM((N_KV, rows, s + SLC_UNIT), F32)]
        + [pltpu.VMEM((N_KV, rows, LANES), F32)] * 4)
    return pl.pallas_call(
        _selected_kernel,
        grid_spec=grid_spec,
        out_shape=jax.ShapeDtypeStruct((bsz, s, Q_WIDTH), BF16),
        compiler_params=_params(2),
        name="selected_attention",
    )(first, q, selneg, gates, o_cmp, o_win, kps, tri_slc)


def _attention(q, gates, kvc, kps, kpw):
    bsz, s, _ = q.shape
    r = kvc.shape[1]
    n_cmp = r - 1
    n_slc = s // SLC_BLOCK
    n_sel = min(N_SELECT, n_slc)
    rows = HEADS_PER_KV * Q_BLOCK
    assert n_slc <= SLC_BLOCK and n_slc % RANK_UNROLL == 0 and s % SLC_UNIT == 0 and s >= WIN_KEYS
    assert s % (WIN_NQ * Q_BLOCK) == 0 and s % (SEL_NQ * Q_BLOCK) == 0

    slopes = np.array([_alibi_slope(hd) for hd in range(N_HEADS)])
    slope_rows = np.repeat(slopes.reshape(N_KV, HEADS_PER_KV), Q_BLOCK, axis=1).reshape(N_KV, rows, 1)
    cbias = np.tile(slope_rows * (CMP_STRIDE * np.arange(r))[None, None, :], (1, SEL_NQ, 1))
    start = np.arange(r)[None, :] * CMP_STRIDE
    blk = np.arange(n_slc)[:, None] * SLC_BLOCK
    overlap_t = ((start < blk + SLC_BLOCK) & (start + CMP_BLOCK > blk) & (np.arange(r)[None, :] < n_cmp))

    ql = np.arange(Q_BLOCK)[:, None]
    kl = np.arange(SLC_BLOCK)[None, :]
    lower = np.where(kl > ql, -BIG, 0.0)
    upper = np.where(kl <= ql, -BIG, 0.0)
    unit_blocks = SLC_UNIT // SLC_BLOCK
    tri_slc = np.zeros((unit_blocks + 2, Q_BLOCK, SLC_UNIT))
    for j in range(unit_blocks):
        tri_slc[j, :, j * SLC_BLOCK:(j + 1) * SLC_BLOCK] = lower
    tri_slc[unit_blocks + 1] = -BIG
    win_blocks = WINDOW // SLC_BLOCK
    lead = WIN_KEYS // SLC_BLOCK - 1
    tri_win = np.zeros((lead + 1, Q_BLOCK, WIN_KEYS))
    for v in range(lead + 1):
        diag = v
        tri_win[v, :, diag * SLC_BLOCK:(diag + 1) * SLC_BLOCK] = lower
        if diag >= win_blocks:
            old = diag - win_blocks
            tri_win[v, :, old * SLC_BLOCK:(old + 1) * SLC_BLOCK] = upper

    o_win = _window(q, kpw, jnp.asarray(tri_win, F32))
    o_cmp, selneg, first = _select(q, kvc, jnp.asarray(cbias, F32), jnp.asarray(overlap_t, BF16), n_sel)
    return _selected(first[:, :, 0, 0], q, selneg, gates, o_cmp, o_win, kps, jnp.asarray(tri_slc, F32))


def _merge_ffn_kernel(x_ref, gate1_ref, uprev_ref, u_ref, o_ref, gbr_ref, pw_ref, ps_ref, wa_ref, wb_ref, wo_ref,
                      g2_ref, b2_ref, shift2_ref, scale2_ref, gate2_ref, g3_ref, b3_ref, w_in_ref, w_out_ref,
                      out_ref, ubuf_ref, sums_ref):
    i = pl.program_id(1)
    tm = u_ref.shape[1]
    ubuf_ref[0:POOL_HALO, :] = jnp.where(i == 0, 0.0, uprev_ref[0])
    ubuf_ref[POOL_HALO:POOL_HALO + tm, :] = u_ref[0]
    t = i * tm + lax.broadcasted_iota(jnp.int32, (tm, 1), 0)

    y_b = _dot(o_ref[0], wb_ref[...])

    end = POOL_HALO + tm
    totals = []
    for level in range(len(POOL_WINDOWS)):
        width = 1 << level
        start = 8 * (level + 1)
        c0 = level * POOL_GROUP
        src = (lambda r0, r1, c=c0: ubuf_ref[r0:r1, c:POOL_WIDTH]) if level == 0 else (
            lambda r0, r1, c=c0, n=level - 1: sums_ref[n, r0:r1, c:POOL_WIDTH])
        doubled = src(start, end) + src(start - width, end - width)
        if level + 1 < len(POOL_WINDOWS):
            sums_ref[level, start:end, c0:POOL_WIDTH] = doubled
        totals.append(doubled[POOL_HALO - start:, 0:POOL_GROUP])

    mixed = []
    for gi, w in enumerate(POOL_WINDOWS):
        assert w == 2 << gi
        cs = slice(gi * POOL_GROUP, (gi + 1) * POOL_GROUP)
        cur = ubuf_ref[POOL_HALO:POOL_HALO + tm, cs]
        total = totals[gi]
        inv_cnt = 1.0 / jnp.minimum(t + 1, w).astype(F32)
        delta = (total * inv_cnt - cur).astype(BF16)
        mixed.append((_dot(delta, pw_ref[gi]) * ps_ref[:, cs]).astype(BF16))
    y_a = _dot(jnp.concatenate(mixed, axis=1), wa_ref[...])
    d = y_a.shape[1]
    y = (gbr_ref[0, :, 0:d].astype(F32) * y_a + gbr_ref[0, :, d:2 * d].astype(F32) * y_b).astype(BF16)
    y = _dot(y, wo_ref[...])
    x2 = _layer_norm(ALPHA * x_ref[0] + gate1_ref[0] * y, g2_ref[...], b2_ref[...])
    out_ref[0] = _swiglu_block(x2, shift2_ref[0], scale2_ref[0], gate2_ref[0], w_in_ref, w_out_ref,
                               g3_ref[...], b3_ref[...])


def _merge_ffn(x, gate1, u, o, gbr, pool_w, pool_scale, w_a, w_b, w_o, g2, b2,
               shift2, scale2, gate2, g3, b3, w_in, w_out):
    bsz, s, d = x.shape
    tm = min(ROW_TILE, s)
    halo_blocks = tm // POOL_HALO

    def row(n):
        return pl.BlockSpec((1, tm, n), lambda bi, i: (bi, i, 0))

    mod = pl.BlockSpec((1, 1, d), lambda bi, i: (bi, 0, 0))
    vec = _const_spec((1, d))
    return pl.pallas_call(
        _merge_ffn_kernel,
        grid=(bsz, s // tm),
        in_specs=[row(d), mod,
                  pl.BlockSpec((1, POOL_HALO, POOL_WIDTH),
                               lambda bi, i: (bi, jnp.maximum(i * halo_blocks - 1, 0), 0)),
                  row(POOL_WIDTH), row(Q_WIDTH), row(2 * d),
                  _const_spec(pool_w.shape), _const_spec((1, POOL_WIDTH)),
                  _const_spec(w_a.shape), _const_spec(w_b.shape), _const_spec(w_o.shape),
                  vec, vec, mod, mod, mod, vec, vec, _const_spec(w_in.shape), _const_spec(w_out.shape)],
        out_specs=row(d),
        out_shape=jax.ShapeDtypeStruct((bsz, s, d), F32),
        scratch_shapes=[pltpu.VMEM((POOL_HALO + tm, POOL_WIDTH), F32),
                        pltpu.VMEM((len(POOL_WINDOWS) - 1, POOL_HALO + tm, POOL_WIDTH), F32)],
        compiler_params=_params(2),
        name="pool_merge_ffn",
    )(x, gate1, u, u, o, gbr, pool_w, pool_scale.reshape(1, POOL_WIDTH), w_a, w_b, w_o,
      g2.reshape(1, d), b2.reshape(1, d), shift2, scale2, gate2, g3.reshape(1, d), b3.reshape(1, d), w_in, w_out)


def _mixer_in_weights(w):
    sizes = (POOL_WIDTH, Q_WIDTH) + (KV_WIDTH,) * 6 + (3 * N_HEADS, 2 * D_MODEL)
    offs = np.concatenate([[0], np.cumsum(sizes)])
    u, q, k_cmp, v_cmp, k_slc, v_slc, k_win, v_win, g_nsa, g_br = [w[:, offs[i]:offs[i + 1]] for i in range(10)]
    cols = [u, q, k_cmp, v_cmp]
    for k, v in ((k_slc, v_slc), (k_win, v_win)):
        for g in range(N_KV):
            cols += [k[:, g * HEAD_DIM:(g + 1) * HEAD_DIM], v[:, g * HEAD_DIM:(g + 1) * HEAD_DIM]]
    per_g = 3 * HEADS_PER_KV
    for g in range(N_KV):
        cols += [g_nsa[:, g * per_g:(g + 1) * per_g], jnp.zeros((w.shape[0], LANES - per_g), w.dtype)]
    cols.append(g_br)
    return jnp.concatenate(cols, axis=1).astype(BF16)


def _compress_weights(pos, w1, w2, value_slot):
    same_group = np.eye(N_KV, dtype=np.float32)

    def expand_w1(half):
        wh = half.reshape(CMP_STRIDE, 1, HEAD_DIM, 1, CMP_HIDDEN)
        z = wh * same_group.reshape(1, N_KV, 1, N_KV, 1)
        return z.reshape(CMP_STRIDE * KV_WIDTH, N_KV * CMP_HIDDEN)

    half_rows = CMP_STRIDE * HEAD_DIM
    w1_big = jnp.stack([expand_w1(w1[:half_rows]), expand_w1(w1[half_rows:])]).astype(BF16)
    slot = np.eye(2, dtype=np.float32)[value_slot]
    w2_big = (w2.reshape(1, CMP_HIDDEN, 1, 1, HEAD_DIM) * same_group.reshape(N_KV, 1, N_KV, 1, 1)
              * slot.reshape(1, 1, 1, 2, 1))
    w2_big = w2_big.reshape(N_KV * CMP_HIDDEN, N_KV * KV_PACK).astype(BF16)
    pos_rows = jnp.broadcast_to(pos.reshape(2, CMP_STRIDE, 1, HEAD_DIM), (2, CMP_STRIDE, N_KV, HEAD_DIM))
    return pos_rows.reshape(2, CMP_STRIDE * KV_WIDTH), w1_big, w2_big


def kernel(x, c, ln_in_g, ln_in_b, w_ada, b_ada, ffn1_w_in, ffn1_w_out, ln1_g, ln1_b, w_mix_in, pool_w, pool_scale,
           cmp_pos_k, cmp_k_w1, cmp_k_w2, cmp_pos_v, cmp_v_w1, cmp_v_w2, w_branch_a, w_branch_b, w_mix_out,
           ln2_g, ln2_b, ffn2_w_in, ffn2_w_out, ln3_g, ln3_b):
    bsz, s, d = x.shape
    for l in range(DEPTH):
        ada = _ada(c, w_ada[l], b_ada[l]).reshape(bsz, 3, 3, 1, d)
        mod = lambda i, j: ada[:, i, j]

        x, u, q, kc, vc, kvs, kvw, gates, gbr = _ffn_mixin(
            x, mod(0, 0), mod(0, 1), mod(0, 2), ln_in_g, ln_in_b, ln1_g[l], ln1_b[l],
            ffn1_w_in[l].astype(BF16), ffn1_w_out[l].astype(BF16),
            mod(1, 0), mod(1, 1), _mixer_in_weights(w_mix_in[l]), pre_ln=l == 0)
        posk, w1k, w2k = _compress_weights(cmp_pos_k[l], cmp_k_w1[l], cmp_k_w2[l], 0)
        posv, w1v, w2v = _compress_weights(cmp_pos_v[l], cmp_v_w1[l], cmp_v_w2[l], 1)
        kvc = _compress(kc, vc, posk, posv, w1k, w1v, w2k, w2v)
        o = _attention(q, gates, kvc, kvs, kvw)
        x = _merge_ffn(x, mod(1, 2), u, o, gbr, pool_w[l].astype(BF16), pool_scale[l],
                       w_branch_a[l].astype(BF16), w_branch_b[l].astype(BF16), w_mix_out[l].astype(BF16),
                       ln2_g[l], ln2_b[l], mod(2, 0), mod(2, 1), mod(2, 2), ln3_g[l], ln3_b[l],
                       ffn2_w_in[l].astype(BF16), ffn2_w_out[l].astype(BF16))
    return x
```

```python
import functools

import numpy as np
import jax
import jax.numpy as jnp
from jax import lax
from jax.experimental import pallas as pl
from jax.experimental.pallas import tpu as pltpu

F32 = jnp.float32
BF16 = jnp.bfloat16

D_MODEL = 1024
POOL_WIDTH = D_MODEL // 2
POOL_WINDOWS = (2, 4, 8, 16)
POOL_GROUP = POOL_WIDTH // len(POOL_WINDOWS)
POOL_HALO = 32
HEAD_DIM = 64
N_HEADS = (D_MODEL // 2) // HEAD_DIM
N_KV = 2
HEADS_PER_KV = N_HEADS // N_KV
Q_WIDTH = N_HEADS * HEAD_DIM
KV_WIDTH = N_KV * HEAD_DIM
CMP_STRIDE = 16
CMP_BLOCK = 2 * CMP_STRIDE
CMP_HIDDEN = 4 * HEAD_DIM
SLC_BLOCK = 64
N_SELECT = 16
WINDOW = 512
Q_BLOCK = SLC_BLOCK
ALIBI_MAX = 8.0
D_FF = 2816
DEPTH = 1
ALPHA = (2.0 * DEPTH) ** 0.25
LN_EPS = 1e-5
NEG = -1e30
FORCE = 1e9

LANES = 128
KV_PACK = 2 * HEAD_DIM
KEY_WIDTH = 2 * KV_PACK
FEAT_OFFSET_LANE = SLC_BLOCK
BIG = 1e30
M_INIT = -3e38
FF_CHUNK = 256
GATE_CHUNK = 512
ROW_TILE = 512
SLC_UNIT = 256
RANK_UNROLL = 1
WIN_NQ = 8
WIN_AHEAD = 1
SEL_NQ = 4
SLC_NQ = 2
TRIP_WIDTHS = (8, 6, 4, 2)
LOG2E = 1.4426950408889634
WIN_KEYS = WINDOW + 2 * Q_BLOCK
VMEM_LIMIT = 56 * 1024 * 1024

_C_U = 0
_C_Q = _C_U + POOL_WIDTH
_C_KC = _C_Q + Q_WIDTH
_C_VC = _C_KC + KV_WIDTH
_C_KVS = _C_VC + KV_WIDTH
_C_KVW = _C_KVS + N_KV * KV_PACK
_C_GN = _C_KVW + N_KV * KV_PACK
_C_GBR = _C_GN + N_KV * LANES
_C_END = _C_GBR + 2 * D_MODEL


def _dot(a, b):
    return jnp.dot(a, b, preferred_element_type=F32)


def _dot_nt(a, b):
    return lax.dot_general(a, b, (((1,), (1,)), ((), ())), preferred_element_type=F32)


def _layer_norm(x, g, b):
    mu = jnp.mean(x, axis=-1, keepdims=True)
    xc = x - mu
    var = jnp.mean(xc * xc, axis=-1, keepdims=True)
    return xc * lax.rsqrt(var + LN_EPS) * g + b


def _masked_softmax(s, mask, any_valid):
    sm = jnp.where(mask, s, NEG)
    m = jnp.max(sm, axis=-1, keepdims=True)
    e = jnp.exp(sm - m)
    return e * jnp.where(any_valid, 1.0 / jnp.sum(e, axis=-1, keepdims=True), 0.0)


def _const_spec(shape):
    nd = len(shape)
    return pl.BlockSpec(shape, lambda *_: (0,) * nd, pipeline_mode=pl.Buffered(1))


def _params(n_grid):
    return pltpu.CompilerParams(dimension_semantics=("parallel",) * n_grid, vmem_limit_bytes=VMEM_LIMIT)


def _ada_kernel(c_ref, w_ref, b_ref, o_ref):
    c = c_ref[...]
    c_act = (c * jax.nn.sigmoid(c)).astype(BF16)
    o_ref[...] = _dot(c_act, w_ref[...].astype(BF16)) + b_ref[...]


def _ada(c, w, b):
    bsz, d = c.shape
    n = w.shape[1]
    tn = D_MODEL
    return pl.pallas_call(
        _ada_kernel,
        grid=(n // tn,),
        in_specs=[pl.BlockSpec((bsz, d), lambda j: (0, 0)),
                  pl.BlockSpec((d, tn), lambda j: (0, j)),
                  pl.BlockSpec((1, tn), lambda j: (0, j))],
        out_specs=pl.BlockSpec((bsz, tn), lambda j: (0, j)),
        out_shape=jax.ShapeDtypeStruct((bsz, n), F32),
        compiler_params=_params(1),
        name="ada_proj",
    )(c, w, b.reshape(1, n))


def _swiglu(h, w_in_ref, w_out_ref):
    acc = jnp.zeros((h.shape[0], w_out_ref.shape[1]), F32)
    for j in range(D_FF // FF_CHUNK):
        c0 = j * FF_CHUNK
        gt = _dot(h, w_in_ref[:, c0:c0 + FF_CHUNK])
        up = _dot(h, w_in_ref[:, D_FF + c0:D_FF + c0 + FF_CHUNK])
        act = (gt * jax.nn.sigmoid(gt) * up).astype(BF16)
        acc = acc + _dot(act, w_out_ref[c0:c0 + FF_CHUNK, :])
    return acc


def _swiglu_block(x, shift, scale, gate, w_in_ref, w_out_ref, g, b):
    h = (x * (1.0 + scale) + shift).astype(BF16)
    return _layer_norm(ALPHA * x + 0.5 * gate * _swiglu(h, w_in_ref, w_out_ref), g, b)


def _ffn_mixin_kernel(x_ref, shift0_ref, scale0_ref, gate0_ref, g0_ref, b0_ref, g1_ref, b1_ref, w_in_ref, w_out_ref,
                      shift1_ref, scale1_ref, w_ref,
                      x1_ref, u_ref, q_ref, kc_ref, vc_ref, kvs_ref, kvw_ref, gn_ref, gbr_ref, *, pre_ln):
    x = x_ref[0]
    if pre_ln:
        x = _layer_norm(x, g0_ref[...], b0_ref[...])
    x1 = _swiglu_block(x, shift0_ref[0], scale0_ref[0], gate0_ref[0], w_in_ref, w_out_ref, g1_ref[...], b1_ref[...])
    x1_ref[0] = x1

    h = (x1 * (1.0 + scale1_ref[0]) + shift1_ref[0]).astype(BF16)

    def proj(c0, c1):
        return _dot(h, w_ref[:, c0:c1])

    tm = x_ref.shape[1]
    pos = pl.program_id(1) * tm + lax.broadcasted_iota(jnp.int32, (tm, LANES), 0)
    lane = lax.broadcasted_iota(jnp.int32, (tm, LANES), 1)
    feat = jnp.where(lane == pos // SLC_BLOCK, 1.0,
                     jnp.where(lane == FEAT_OFFSET_LANE, (pos % SLC_BLOCK).astype(F32), 0.0)).astype(BF16)

    def put_u(z):
        u_ref[0] = z

    def put_q(z):
        q_ref[0] = (z * HEAD_DIM ** -0.5).astype(BF16)

    def put_cmp(z):
        kc_ref[0] = z[:, 0:KV_WIDTH]
        vc_ref[0] = z[:, KV_WIDTH:2 * KV_WIDTH]

    def put_keys(ref):
        def put(z):
            zb = z.astype(BF16)
            for g in range(N_KV):
                ref[0, g, :, 0:KV_PACK] = zb[:, g * KV_PACK:(g + 1) * KV_PACK]
                ref[0, g, :, KV_PACK:KEY_WIDTH] = feat
        return put

    def put_head_gates(z):
        sg = jax.nn.sigmoid(z)
        for g in range(N_KV):
            gn_ref[0, g] = sg[:, g * LANES:(g + 1) * LANES]

    def put_branch_gates(c0):
        def put(z):
            gbr_ref[0, :, c0 - _C_GBR:c0 - _C_GBR + z.shape[1]] = jax.nn.sigmoid(z).astype(BF16)
        return put

    gate_cols = [(c0, c0 + GATE_CHUNK) for c0 in range(_C_GBR, _C_END, GATE_CHUNK)]
    light = [((_C_U, _C_Q), put_u), ((_C_Q, _C_KC), put_q), ((_C_KC, _C_KVS), put_cmp),
             ((_C_KVS, _C_KVW), put_keys(kvs_ref)), ((_C_KVW, _C_GN), put_keys(kvw_ref)),
             ((_C_GN, _C_GBR), put_head_gates)]
    stages = []
    for n in range(max(len(gate_cols), len(light))):
        if n < len(gate_cols):
            stages.append((gate_cols[n], put_branch_gates(gate_cols[n][0])))
        if n < len(light):
            stages.append(light[n])
    pending = proj(*stages[0][0])
    for n, (_, sink) in enumerate(stages):
        z = pending
        if n + 1 < len(stages):
            pending = proj(*stages[n + 1][0])
        sink(z)


def _ffn_mixin(x, shift0, scale0, gate0, g0, b0, g1, b1, w_in, w_out, shift1, scale1, w_mix, *, pre_ln):
    bsz, s, d = x.shape
    tm = min(ROW_TILE, s)

    def row(n, dt):
        return pl.BlockSpec((1, tm, n), lambda b, i: (b, i, 0)), jax.ShapeDtypeStruct((bsz, s, n), dt)

    def grouped(n, dt):
        return (pl.BlockSpec((1, N_KV, tm, n), lambda b, i: (b, 0, i, 0)),
                jax.ShapeDtypeStruct((bsz, N_KV, s, n), dt))

    outs = [row(d, F32), row(POOL_WIDTH, F32), row(Q_WIDTH, BF16), row(KV_WIDTH, F32), row(KV_WIDTH, F32),
            grouped(KEY_WIDTH, BF16), grouped(KEY_WIDTH, BF16), grouped(LANES, F32), row(2 * D_MODEL, BF16)]
    mod = pl.BlockSpec((1, 1, d), lambda b, i: (b, 0, 0))
    vec = _const_spec((1, d))
    return pl.pallas_call(
        functools.partial(_ffn_mixin_kernel, pre_ln=pre_ln),
        grid=(bsz, s // tm),
        in_specs=[row(d, F32)[0], mod, mod, mod, vec, vec, vec, vec,
                  _const_spec(w_in.shape), _const_spec(w_out.shape), mod, mod, _const_spec(w_mix.shape)],
        out_specs=[o[0] for o in outs],
        out_shape=[o[1] for o in outs],
        compiler_params=_params(2),
        name="ffn_mixer_in",
    )(x, shift0, scale0, gate0, g0.reshape(1, d), b0.reshape(1, d), g1.reshape(1, d), b1.reshape(1, d),
      w_in, w_out, shift1, scale1, w_mix)


def _compress_kernel(k_ref, v_ref, posk_ref, posv_ref, w1k_ref, w1v_ref, w2k_ref, w2v_ref, o_ref, nxt_ref):
    r = k_ref.shape[1] // CMP_STRIDE

    def hidden(x_ref, pos_ref, w1_ref):
        x = jnp.concatenate([x_ref[0, pl.ds(j, r, stride=CMP_STRIDE), :] for j in range(CMP_STRIDE)], axis=1)
        first = _dot((x + pos_ref[0:1, :]).astype(BF16), w1_ref[0])
        nxt_ref[0:r, :] = _dot((x + pos_ref[1:2, :]).astype(BF16), w1_ref[1])
        nxt_ref[r:r + 8, :] = jnp.zeros((8, nxt_ref.shape[1]), F32)
        return jax.nn.gelu(first + nxt_ref[1:r + 1, :]).astype(BF16)

    hk = hidden(k_ref, posk_ref, w1k_ref)
    hv = hidden(v_ref, posv_ref, w1v_ref)
    o_ref[0] = (_dot(hk, w2k_ref[...]) + _dot(hv, w2v_ref[...])).astype(BF16)


def _compress(kc, vc, posk, posv, w1k, w1v, w2k, w2v):
    bsz, s, n = kc.shape
    r = s // CMP_STRIDE
    hid = w1k.shape[2]
    row = pl.BlockSpec((1, s, n), lambda b: (b, 0, 0))
    return pl.pallas_call(
        _compress_kernel,
        grid=(bsz,),
        in_specs=[row, row, _const_spec(posk.shape), _const_spec(posv.shape),
                  _const_spec(w1k.shape), _const_spec(w1v.shape),
                  _const_spec(w2k.shape), _const_spec(w2v.shape)],
        out_specs=pl.BlockSpec((1, r, N_KV * KV_PACK), lambda b: (b, 0, 0)),
        out_shape=jax.ShapeDtypeStruct((bsz, r, N_KV * KV_PACK), BF16),
        scratch_shapes=[pltpu.VMEM((r + 8, hid), F32)],
        compiler_params=_params(1),
        name="compress_mlp",
    )(kc, vc, posk, posv, w1k, w1v, w2k, w2v)


def _alibi_slope(head):
    return 2.0 ** (-ALIBI_MAX * (head + 1) / N_HEADS)


def _padded_heads(q_tile):
    lane = lax.broadcasted_iota(jnp.int32, (Q_BLOCK, LANES), 1)
    qf = q_tile.astype(F32)
    heads = []
    for hd in range(N_HEADS):
        slab = qf[:, (hd // 2) * LANES:(hd // 2 + 1) * LANES]
        if hd % 2:
            slab = pltpu.roll(slab, HEAD_DIM, 1)
        heads.append(jnp.where(lane < HEAD_DIM, slab, 0.0))
    return heads


def _pack_heads(o_rows):
    lane = lax.broadcasted_iota(jnp.int32, (Q_BLOCK, LANES), 1)
    pairs = []
    for j in range(HEADS_PER_KV // 2):
        even = o_rows[(2 * j) * Q_BLOCK:(2 * j + 1) * Q_BLOCK]
        odd = o_rows[(2 * j + 1) * Q_BLOCK:(2 * j + 2) * Q_BLOCK]
        pairs.append(jnp.where(lane < HEAD_DIM, pltpu.roll(even, HEAD_DIM, 1), odd))
    return jnp.concatenate(pairs, axis=1)


def _feature_rows(qb, hd):
    lane_row = lax.broadcasted_iota(jnp.int32, (1, LANES), 1)
    slope = _alibi_slope(hd)
    tail = jnp.where(lane_row == FEAT_OFFSET_LANE, slope, 0.0)
    block_bias = slope * SLC_BLOCK * (lane_row - qb).astype(F32)
    in_window = (lane_row >= qb - WINDOW // SLC_BLOCK) & (lane_row <= qb)
    slc_row = jnp.where(lane_row < SLC_BLOCK, block_bias, tail)
    win_row = jnp.where(lane_row < SLC_BLOCK, jnp.where(in_window, block_bias, -BIG), tail)
    return slc_row, win_row


def _window_kernel(q_ref, kpw_ref, triw_ref, o_ref):
    gw = HEADS_PER_KV * HEAD_DIM
    chains = [(k, g) for k in range(WIN_NQ) for g in range(N_KV)]

    def scores(k, g):
        qb = pl.program_id(1) * WIN_NQ + k
        heads = _padded_heads(q_ref[0, k * Q_BLOCK:(k + 1) * Q_BLOCK, :])
        w0 = pl.multiple_of(jnp.maximum((qb + 1) * Q_BLOCK - WIN_KEYS, 0), Q_BLOCK)
        tri = jnp.concatenate([triw_ref[jnp.minimum(qb, WIN_KEYS // SLC_BLOCK - 1)]] * HEADS_PER_KV, axis=0)
        qw = [jnp.concatenate([heads[hd], jnp.broadcast_to(_feature_rows(qb, hd)[1], (Q_BLOCK, LANES))], axis=1)
              for hd in range(g * HEADS_PER_KV, (g + 1) * HEADS_PER_KV)]
        kp = kpw_ref[0, g, pl.ds(w0, WIN_KEYS), :]
        return _dot_nt(jnp.concatenate(qw, axis=0).astype(BF16), kp) + tri, kp

    def finish(k, g, o_raw, inv_l):
        o_ref[0, k * Q_BLOCK:(k + 1) * Q_BLOCK, g * gw:(g + 1) * gw] = _pack_heads(o_raw * inv_l).astype(BF16)

    pending = [scores(*chains[n]) for n in range(WIN_AHEAD)]
    unfinished = None
    for n, (k, g) in enumerate(chains):
        s, kp = pending.pop(0)
        if n + WIN_AHEAD < len(chains):
            pending.append(scores(*chains[n + WIN_AHEAD]))
        e = jnp.exp(s - jnp.max(s, axis=-1, keepdims=True))
        o_raw = _dot(e.astype(BF16), kp[:, 0:KV_PACK])
        inv_l = 1.0 / jnp.sum(e, axis=-1, keepdims=True)
        if unfinished is not None:
            finish(*unfinished)
        unfinished = (k, g, o_raw, inv_l)
    finish(*unfinished)


def _window(q, kpw, tri_win):
    bsz, s, _ = q.shape
    qt = WIN_NQ * Q_BLOCK
    return pl.pallas_call(
        _window_kernel,
        grid=(bsz, s // qt),
        in_specs=[pl.BlockSpec((1, qt, Q_WIDTH), lambda b, i: (b, i, 0)),
                  pl.BlockSpec((1, N_KV, s, KEY_WIDTH), lambda b, i: (b, 0, 0, 0)),
                  _const_spec(tri_win.shape)],
        out_specs=pl.BlockSpec((1, qt, Q_WIDTH), lambda b, i: (b, i, 0)),
        out_shape=jax.ShapeDtypeStruct((bsz, s, Q_WIDTH), BF16),
        compiler_params=_params(2),
        name="window_attention",
    )(q, kpw, tri_win)


def _select_kernel(q_ref, kvc_ref, cbias_ref, ovl_ref, eye_ref, ocmp_ref, selneg_ref, first_ref, score_ref, *, n_sel):
    step = pl.program_id(1)
    rows = SEL_NQ * HEADS_PER_KV * Q_BLOCK
    n_slc = ovl_ref.shape[0]
    r = kvc_ref.shape[1]
    lanes_q = SEL_NQ * N_KV * Q_BLOCK
    heads = [_padded_heads(q_ref[0, k * Q_BLOCK:(k + 1) * Q_BLOCK, :]) for k in range(SEL_NQ)]

    row = lax.broadcasted_iota(jnp.int32, (rows, 1), 0)
    t = (step * SEL_NQ + row // (HEADS_PER_KV * Q_BLOCK)) * Q_BLOCK + (row & (Q_BLOCK - 1))
    last_cmp = (t - (CMP_BLOCK - 1)) >> 4
    visible = lax.broadcasted_iota(jnp.int32, (rows, r), 1) <= last_cmp
    p_sums = [[None] * N_KV for _ in range(SEL_NQ)]
    gw = HEADS_PER_KV * HEAD_DIM
    def cmp_scores(g):
        q_pad = jnp.concatenate([heads[k][g * HEADS_PER_KV + h] for k in range(SEL_NQ)
                                 for h in range(HEADS_PER_KV)], axis=0).astype(BF16)
        return _dot_nt(q_pad, kvc_ref[0, :, g * KV_PACK:(g + 1) * KV_PACK])

    raw = [cmp_scores(g) for g in range(N_KV)]
    for g in range(N_KV):
        kvc = kvc_ref[0, :, g * KV_PACK:(g + 1) * KV_PACK]
        p_cmp = _masked_softmax(raw[g] + cbias_ref[g], visible, last_cmp >= 0)
        o_cmp = _dot(p_cmp.astype(BF16), kvc)
        for k in range(SEL_NQ):
            base = k * HEADS_PER_KV * Q_BLOCK
            ocmp_ref[0, k * Q_BLOCK:(k + 1) * Q_BLOCK, g * gw:(g + 1) * gw] = _pack_heads(
                o_cmp[base:base + HEADS_PER_KV * Q_BLOCK]).astype(BF16)
            p_sum = p_cmp[base:base + Q_BLOCK]
            for h in range(1, HEADS_PER_KV):
                p_sum = p_sum + p_cmp[base + h * Q_BLOCK:base + (h + 1) * Q_BLOCK]
            p_sums[k][g] = p_sum

    p_all = jnp.concatenate([p_sums[k][g] for k in range(SEL_NQ) for g in range(N_KV)], axis=0)
    p_hi = p_all.astype(BF16)
    p_lo = (p_all - p_hi.astype(F32)).astype(BF16)
    ovl = ovl_ref[...]
    imp_t = _dot_nt(ovl, p_hi) + _dot_nt(ovl, p_lo)

    blk = lax.broadcasted_iota(jnp.int32, (n_slc, lanes_q), 0)
    qb = step * SEL_NQ + lax.broadcasted_iota(jnp.int32, (n_slc, lanes_q), 1) // (N_KV * Q_BLOCK)
    forced = (blk == 0) | (blk == qb) | (blk == qb - 1)
    score = jnp.where(blk > qb, NEG, jnp.where(forced, FORCE, jnp.where(imp_t > 0.0, imp_t, 0.0)))
    key = lax.bitcast_convert_type(score, jnp.int32)
    key_prev = key - 1
    score_ref[...] = key

    def rank_step(i, ranks):
        out = []
        for u in range(RANK_UNROLL):
            row_i = i * RANK_UNROLL + u
            ri = score_ref[pl.ds(row_i, 1), :]
            out.append(ranks[u] + jnp.where(ri > jnp.where(blk > row_i, key_prev, key), 1.0, 0.0))
        return tuple(out)

    ranks = lax.fori_loop(0, (step * SEL_NQ + SEL_NQ - 1) // RANK_UNROLL + 1, rank_step,
                          (jnp.zeros((n_slc, lanes_q), F32),) * RANK_UNROLL)
    chosen = (functools.reduce(jnp.add, ranks) < n_sel) & (blk <= qb)

    oldest = jnp.where(chosen & (blk >= 1), blk, n_slc).astype(F32)
    sel_t = jnp.where(chosen, 1.0, 0.0).astype(BF16)
    sel_t = jnp.concatenate([sel_t, jnp.zeros((LANES - n_slc, lanes_q), BF16)], axis=0)
    sel = _dot_nt(eye_ref[...], sel_t)
    lane = lax.broadcasted_iota(jnp.int32, (Q_BLOCK, LANES), 1)
    for k in range(SEL_NQ):
        lanes_k = slice(k * N_KV * Q_BLOCK, (k + 1) * N_KV * Q_BLOCK)
        first = jnp.min(jnp.min(oldest[:, lanes_k], axis=1, keepdims=True), axis=0, keepdims=True)
        first_ref[0, k] = jnp.broadcast_to(first, first_ref.shape[2:]).astype(jnp.int32)
        for g in range(N_KV):
            base = (k * N_KV + g) * Q_BLOCK
            selneg_ref[0, g, k * Q_BLOCK:(k + 1) * Q_BLOCK, :] = jnp.where(
                lane < SLC_BLOCK, (sel[base:base + Q_BLOCK] - 1.0) * BIG, 0.0).astype(BF16)


def _select(q, kvc, cbias, overlap_t, n_sel):
    bsz, s, _ = q.shape
    r = kvc.shape[1]
    n_slc = overlap_t.shape[0]
    qt = SEL_NQ * Q_BLOCK
    eye = jnp.asarray(np.eye(N_KV * qt), BF16)
    return pl.pallas_call(
        functools.partial(_select_kernel, n_sel=n_sel),
        grid=(bsz, s // qt),
        in_specs=[pl.BlockSpec((1, qt, Q_WIDTH), lambda b, i: (b, i, 0)),
                  pl.BlockSpec((1, r, N_KV * KV_PACK), lambda b, i: (b, 0, 0)),
                  _const_spec(cbias.shape), _const_spec(overlap_t.shape), _const_spec(eye.shape)],
        out_specs=[pl.BlockSpec((1, qt, Q_WIDTH), lambda b, i: (b, i, 0)),
                   pl.BlockSpec((1, N_KV, qt, LANES), lambda b, i: (b, 0, i, 0)),
                   pl.BlockSpec((1, SEL_NQ, 8, LANES), lambda b, i: (b, i, 0, 0))],
        out_shape=[jax.ShapeDtypeStruct((bsz, s, Q_WIDTH), BF16),
                   jax.ShapeDtypeStruct((bsz, N_KV, s, LANES), BF16),
                   jax.ShapeDtypeStruct((bsz, s // Q_BLOCK, 8, LANES), jnp.int32)],
        scratch_shapes=[pltpu.VMEM((n_slc, SEL_NQ * N_KV * Q_BLOCK), jnp.int32)],
        compiler_params=_params(2),
        name="compressed_attention_select",
    )(q, kvc, cbias, overlap_t, eye)


def _selected_kernel(first_ref, q_ref, selneg_ref, gate_ref, ocmp_ref, owin_ref, kps_ref, tris_ref,
                     o_ref, qs_ref, s_ref, mt_ref, m_ref, lt_ref, acc_ref):
    qbs = [pl.program_id(1) * SLC_NQ + k for k in range(SLC_NQ)]
    rows = SLC_NQ * HEADS_PER_KV * Q_BLOCK
    heads = [_padded_heads(q_ref[0, k * Q_BLOCK:(k + 1) * Q_BLOCK, :]) for k in range(SLC_NQ)]
    for g in range(N_KV):
        qs = []
        for k in range(SLC_NQ):
            masked_out = selneg_ref[0, g, k * Q_BLOCK:(k + 1) * Q_BLOCK, :].astype(F32)
            qs += [jnp.concatenate([heads[k][hd], masked_out + _feature_rows(qbs[k], hd)[0]], axis=1)
                   for hd in range(g * HEADS_PER_KV, (g + 1) * HEADS_PER_KV)]
        qs_ref[g] = jnp.concatenate(qs, axis=0).astype(BF16)

    unit_blocks = SLC_UNIT // SLC_BLOCK
    diag_unit = qbs[0] // unit_blocks
    first = functools.reduce(jnp.minimum, [first_ref[pl.program_id(0), qb] for qb in qbs])
    first_unit = jnp.minimum(first, qbs[0]) // unit_blocks
    lead = jnp.minimum(first_unit, 1)
    n_units = diag_unit - first_unit + 1 + lead

    def unit_of(i):
        return jnp.where((i < lead) | (i >= n_units), 0, first_unit + i - lead)

    def unit_start(i):
        return pl.multiple_of(unit_of(i) * SLC_UNIT, SLC_UNIT)

    def slot_start(i):
        return pl.multiple_of(i * SLC_UNIT, SLC_UNIT)

    def trip_chains(first_slot, width):
        return [(g, first_slot + u) for u in range(width) for g in range(N_KV)]

    def score_matmul(g, i):
        return _dot_nt(qs_ref[g], kps_ref[0, g, pl.ds(unit_start(i), SLC_UNIT), :])

    def score_finish(g, i, raw):
        tri = []
        for qb in qbs:
            variant = jnp.where(i >= n_units, unit_blocks + 1,
                                jnp.where(unit_of(i) == diag_unit, qb % unit_blocks, unit_blocks))
            tri += [tris_ref[variant]] * HEADS_PER_KV
        s = (raw + jnp.concatenate(tri, axis=0)) * LOG2E
        s_ref[g, :, pl.ds(slot_start(i), SLC_UNIT)] = s
        mt_ref[g] = jnp.maximum(jnp.maximum(mt_ref[g], s[:, 0:LANES]), s[:, LANES:SLC_UNIT])

    def score_trip(first_slot, width):
        chains = trip_chains(first_slot, width)
        pending = score_matmul(*chains[0])
        for n, (g, i) in enumerate(chains):
            raw = pending
            if n + 1 < len(chains):
                pending = score_matmul(*chains[n + 1])
            score_finish(g, i, raw)

    def exponentials(g, i):
        m = m_ref[g]
        es = [jnp.exp2(s_ref[g, :, pl.ds(slot_start(i) + j * LANES, LANES)] - m) for j in range(SLC_UNIT // LANES)]
        lt_ref[g] = lt_ref[g] + functools.reduce(jnp.add, es)
        return jnp.concatenate(es, axis=1).astype(BF16)

    def value_trip(first_slot, width):
        chains = trip_chains(first_slot, width)
        pending = exponentials(*chains[0])
        for n, (g, i) in enumerate(chains):
            e = pending
            if n + 1 < len(chains):
                pending = exponentials(*chains[n + 1])
            acc_ref[g] = acc_ref[g] + _dot(e, kps_ref[0, g, pl.ds(unit_start(i), SLC_UNIT), 0:KV_PACK])

    def sweep(trip):
        done = 0
        for tier, width in enumerate(TRIP_WIDTHS):
            left = n_units - done
            if tier + 1 < len(TRIP_WIDTHS):
                count = left // width + jnp.where(left % width > TRIP_WIDTHS[tier + 1], 1, 0)
            else:
                count = (left + width - 1) // width
            count = jnp.maximum(count, 0)

            def body(p, carry, base=done, width=width):
                trip(base + p * width, width)
                return carry

            lax.fori_loop(0, count, body, 0)
            done = done + count * width

    mt_ref[...] = jnp.full(mt_ref.shape, M_INIT, F32)
    sweep(score_trip)
    for g in range(N_KV):
        m_ref[g] = jnp.broadcast_to(jnp.max(mt_ref[g], axis=-1, keepdims=True), (rows, LANES))
    lt_ref[...] = jnp.zeros(lt_ref.shape, F32)
    acc_ref[...] = jnp.zeros(acc_ref.shape, F32)
    sweep(value_trip)

    lane = lax.broadcasted_iota(jnp.int32, (Q_BLOCK, LANES), 1)
    for g in range(N_KV):
        l = jnp.sum(lt_ref[g], axis=-1, keepdims=True)
        o_rows = acc_ref[g] * jnp.where(l > 0.0, 1.0 / l, 0.0)
        for k in range(SLC_NQ):
            qr = slice(k * Q_BLOCK, (k + 1) * Q_BLOCK)
            o_slc = _pack_heads(o_rows[k * HEADS_PER_KV * Q_BLOCK:(k + 1) * HEADS_PER_KV * Q_BLOCK])
            gt = gate_ref[0, g, qr, :]
            for j in range(HEADS_PER_KV // 2):
                cols = slice((g * HEADS_PER_KV // 2 + j) * LANES, (g * HEADS_PER_KV // 2 + j + 1) * LANES)
                branches = (ocmp_ref[0, qr, cols].astype(F32), o_slc[:, j * LANES:(j + 1) * LANES],
                            owin_ref[0, qr, cols].astype(F32))
                total = None
                for c, branch in enumerate(branches):
                    even, odd = 3 * (2 * j) + c, 3 * (2 * j + 1) + c
                    gate = jnp.take_along_axis(gt, jnp.where(lane < HEAD_DIM, even, odd), axis=1)
                    total = gate * branch if total is None else total + gate * branch
                o_ref[0, qr, cols] = total.astype(BF16)


def _selected(first, q, selneg, gates, o_cmp, o_win, kps, tri_slc):
    bsz, s, _ = q.shape
    qt = SLC_NQ * Q_BLOCK
    rows = HEADS_PER_KV * qt
    assert (SLC_UNIT // SLC_BLOCK) % SLC_NQ == 0 and s % qt == 0

    def row(n):
        return pl.BlockSpec((1, qt, n), lambda b, i, first_ref: (b, i, 0))

    def grouped(n):
        return pl.BlockSpec((1, N_KV, qt, n), lambda b, i, first_ref: (b, 0, i, 0))

    grid_spec = pltpu.PrefetchScalarGridSpec(
        num_scalar_prefetch=1,
        grid=(bsz, s // qt),
        in_specs=[row(Q_WIDTH), grouped(LANES), grouped(LANES), row(Q_WIDTH), row(Q_WIDTH),
                  pl.BlockSpec((1, N_KV, s, KEY_WIDTH), lambda b, i, first_ref: (b, 0, 0, 0)),
                  _const_spec(tri_slc.shape)],
        out_specs=row(Q_WIDTH),
        scratch_shapes=[pltpu.VMEM((N_KV, rows, KEY_WIDTH), BF16),
                        pltpu.VMEM((N_KV, rows, s + SLC_UNIT), F32)]
        + [pltpu.VMEM((N_KV, rows, LANES), F32)] * 4)
    return pl.pallas_call(
        _selected_kernel,
        grid_spec=grid_spec,
        out_shape=jax.ShapeDtypeStruct((bsz, s, Q_WIDTH), BF16),
        compiler_params=_params(2),
        name="selected_attention",
    )(first, q, selneg, gates, o_cmp, o_win, kps, tri_slc)


def _attention(q, gates, kvc, kps, kpw):
    bsz, s, _ = q.shape
    r = kvc.shape[1]
    n_cmp = r - 1
    n_slc = s // SLC_BLOCK
    n_sel = min(N_SELECT, n_slc)
    rows = HEADS_PER_KV * Q_BLOCK
    assert n_slc <= SLC_BLOCK and n_slc % RANK_UNROLL == 0 and s % SLC_UNIT == 0 and s >= WIN_KEYS
    assert s % (WIN_NQ * Q_BLOCK) == 0 and s % (SEL_NQ * Q_BLOCK) == 0

    slopes = np.array([_alibi_slope(hd) for hd in range(N_HEADS)])
    slope_rows = np.repeat(slopes.reshape(N_KV, HEADS_PER_KV), Q_BLOCK, axis=1).reshape(N_KV, rows, 1)
    cbias = np.tile(slope_rows * (CMP_STRIDE * np.arange(r))[None, None, :], (1, SEL_NQ, 1))
    start = np.arange(r)[None, :] * CMP_STRIDE
    blk = np.arange(n_slc)[:, None] * SLC_BLOCK
    overlap_t = ((start < blk + SLC_BLOCK) & (start + CMP_BLOCK > blk) & (np.arange(r)[None, :] < n_cmp))

    ql = np.arange(Q_BLOCK)[:, None]
    kl = np.arange(SLC_BLOCK)[None, :]
    lower = np.where(kl > ql, -BIG, 0.0)
    upper = np.where(kl <= ql, -BIG, 0.0)
    unit_blocks = SLC_UNIT // SLC_BLOCK
    tri_slc = np.zeros((unit_blocks + 2, Q_BLOCK, SLC_UNIT))
    for j in range(unit_blocks):
        tri_slc[j, :, j * SLC_BLOCK:(j + 1) * SLC_BLOCK] = lower
    tri_slc[unit_blocks + 1] = -BIG
    win_blocks = WINDOW // SLC_BLOCK
    lead = WIN_KEYS // SLC_BLOCK - 1
    tri_win = np.zeros((lead + 1, Q_BLOCK, WIN_KEYS))
    for v in range(lead + 1):
        diag = v
        tri_win[v, :, diag * SLC_BLOCK:(diag + 1) * SLC_BLOCK] = lower
        if diag >= win_blocks:
            old = diag - win_blocks
            tri_win[v, :, old * SLC_BLOCK:(old + 1) * SLC_BLOCK] = upper

    o_win = _window(q, kpw, jnp.asarray(tri_win, F32))
    o_cmp, selneg, first = _select(q, kvc, jnp.asarray(cbias, F32), jnp.asarray(overlap_t, BF16), n_sel)
    return _selected(first[:, :, 0, 0], q, selneg, gates, o_cmp, o_win, kps, jnp.asarray(tri_slc, F32))


def _merge_ffn_kernel(x_ref, gate1_ref, uprev_ref, u_ref, o_ref, gbr_ref, pw_ref, ps_ref, wa_ref, wb_ref, wo_ref,
                      g2_ref, b2_ref, shift2_ref, scale2_ref, gate2_ref, g3_ref, b3_ref, w_in_ref, w_out_ref,
                      out_ref, ubuf_ref, sums_ref):
    i = pl.program_id(1)
    tm = u_ref.shape[1]
    ubuf_ref[0:POOL_HALO, :] = jnp.where(i == 0, 0.0, uprev_ref[0])
    ubuf_ref[POOL_HALO:POOL_HALO + tm, :] = u_ref[0]
    t = i * tm + lax.broadcasted_iota(jnp.int32, (tm, 1), 0)

    y_b = _dot(o_ref[0], wb_ref[...])

    end = POOL_HALO + tm
    totals = []
    for level in range(len(POOL_WINDOWS)):
        width = 1 << level
        start = 8 * (level + 1)
        c0 = level * POOL_GROUP
        src = (lambda r0, r1, c=c0: ubuf_ref[r0:r1, c:POOL_WIDTH]) if level == 0 else (
            lambda r0, r1, c=c0, n=level - 1: sums_ref[n, r0:r1, c:POOL_WIDTH])
        doubled = src(start, end) + src(start - width, end - width)
        if level + 1 < len(POOL_WINDOWS):
            sums_ref[level, start:end, c0:POOL_WIDTH] = doubled
        totals.append(doubled[POOL_HALO - start:, 0:POOL_GROUP])

    mixed = []
    for gi, w in enumerate(POOL_WINDOWS):
        assert w == 2 << gi
        cs = slice(gi * POOL_GROUP, (gi + 1) * POOL_GROUP)
        cur = ubuf_ref[POOL_HALO:POOL_HALO + tm, cs]
        total = totals[gi]
        inv_cnt = 1.0 / jnp.minimum(t + 1, w).astype(F32)
        delta = (total * inv_cnt - cur).astype(BF16)
        mixed.append((_dot(delta, pw_ref[gi]) * ps_ref[:, cs]).astype(BF16))
    y_a = _dot(jnp.concatenate(mixed, axis=1), wa_ref[...])
    d = y_a.shape[1]
    y = (gbr_ref[0, :, 0:d].astype(F32) * y_a + gbr_ref[0, :, d:2 * d].astype(F32) * y_b).astype(BF16)
    y = _dot(y, wo_ref[...])
    x2 = _layer_norm(ALPHA * x_ref[0] + gate1_ref[0] * y, g2_ref[...], b2_ref[...])
    out_ref[0] = _swiglu_block(x2, shift2_ref[0], scale2_ref[0], gate2_ref[0], w_in_ref, w_out_ref,
                               g3_ref[...], b3_ref[...])


def _merge_ffn(x, gate1, u, o, gbr, pool_w, pool_scale, w_a, w_b, w_o, g2, b2,
               shift2, scale2, gate2, g3, b3, w_in, w_out):
    bsz, s, d = x.shape
    tm = min(ROW_TILE, s)
    halo_blocks = tm // POOL_HALO

    def row(n):
        return pl.BlockSpec((1, tm, n), lambda bi, i: (bi, i, 0))

    mod = pl.BlockSpec((1, 1, d), lambda bi, i: (bi, 0, 0))
    vec = _const_spec((1, d))
    return pl.pallas_call(
        _merge_ffn_kernel,
        grid=(bsz, s // tm),
        in_specs=[row(d), mod,
                  pl.BlockSpec((1, POOL_HALO, POOL_WIDTH),
                               lambda bi, i: (bi, jnp.maximum(i * halo_blocks - 1, 0), 0)),
                  row(POOL_WIDTH), row(Q_WIDTH), row(2 * d),
                  _const_spec(pool_w.shape), _const_spec((1, POOL_WIDTH)),
                  _const_spec(w_a.shape), _const_spec(w_b.shape), _const_spec(w_o.shape),
                  vec, vec, mod, mod, mod, vec, vec, _const_spec(w_in.shape), _const_spec(w_out.shape)],
        out_specs=row(d),
        out_shape=jax.ShapeDtypeStruct((bsz, s, d), F32),
        scratch_shapes=[pltpu.VMEM((POOL_HALO + tm, POOL_WIDTH), F32),
                        pltpu.VMEM((len(POOL_WINDOWS) - 1, POOL_HALO + tm, POOL_WIDTH), F32)],
        compiler_params=_params(2),
        name="pool_merge_ffn",
    )(x, gate1, u, u, o, gbr, pool_w, pool_scale.reshape(1, POOL_WIDTH), w_a, w_b, w_o,
      g2.reshape(1, d), b2.reshape(1, d), shift2, scale2, gate2, g3.reshape(1, d), b3.reshape(1, d), w_in, w_out)


def _mixer_in_weights(w):
    sizes = (POOL_WIDTH, Q_WIDTH) + (KV_WIDTH,) * 6 + (3 * N_HEADS, 2 * D_MODEL)
    offs = np.concatenate([[0], np.cumsum(sizes)])
    u, q, k_cmp, v_cmp, k_slc, v_slc, k_win, v_win, g_nsa, g_br = [w[:, offs[i]:offs[i + 1]] for i in range(10)]
    cols = [u, q, k_cmp, v_cmp]
    for k, v in ((k_slc, v_slc), (k_win, v_win)):
        for g in range(N_KV):
            cols += [k[:, g * HEAD_DIM:(g + 1) * HEAD_DIM], v[:, g * HEAD_DIM:(g + 1) * HEAD_DIM]]
    per_g = 3 * HEADS_PER_KV
    for g in range(N_KV):
        cols += [g_nsa[:, g * per_g:(g + 1) * per_g], jnp.zeros((w.shape[0], LANES - per_g), w.dtype)]
    cols.append(g_br)
    return jnp.concatenate(cols, axis=1).astype(BF16)


def _compress_weights(pos, w1, w2, value_slot):
    same_group = np.eye(N_KV, dtype=np.float32)

    def expand_w1(half):
        wh = half.reshape(CMP_STRIDE, 1, HEAD_DIM, 1, CMP_HIDDEN)
        z = wh * same_group.reshape(1, N_KV, 1, N_KV, 1)
        return z.reshape(CMP_STRIDE * KV_WIDTH, N_KV * CMP_HIDDEN)

    half_rows = CMP_STRIDE * HEAD_DIM
    w1_big = jnp.stack([expand_w1(w1[:half_rows]), expand_w1(w1[half_rows:])]).astype(BF16)
    slot = np.eye(2, dtype=np.float32)[value_slot]
    w2_big = (w2.reshape(1, CMP_HIDDEN, 1, 1, HEAD_DIM) * same_group.reshape(N_KV, 1, N_KV, 1, 1)
              * slot.reshape(1, 1, 1, 2, 1))
    w2_big = w2_big.reshape(N_KV * CMP_HIDDEN, N_KV * KV_PACK).astype(BF16)
    pos_rows = jnp.broadcast_to(pos.reshape(2, CMP_STRIDE, 1, HEAD_DIM), (2, CMP_STRIDE, N_KV, HEAD_DIM))
    return pos_rows.reshape(2, CMP_STRIDE * KV_WIDTH), w1_big, w2_big


def kernel(x, c, ln_in_g, ln_in_b, w_ada, b_ada, ffn1_w_in, ffn1_w_out, ln1_g, ln1_b, w_mix_in, pool_w, pool_scale,
           cmp_pos_k, cmp_k_w1, cmp_k_w2, cmp_pos_v, cmp_v_w1, cmp_v_w2, w_branch_a, w_branch_b, w_mix_out,
           ln2_g, ln2_b, ffn2_w_in, ffn2_w_out, ln3_g, ln3_b):
    bsz, s, d = x.shape
    for l in range(DEPTH):
        ada = _ada(c, w_ada[l], b_ada[l]).reshape(bsz, 3, 3, 1, d)
        mod = lambda i, j: ada[:, i, j]

        x, u, q, kc, vc, kvs, kvw, gates, gbr = _ffn_mixin(
            x, mod(0, 0), mod(0, 1), mod(0, 2), ln_in_g, ln_in_b, ln1_g[l], ln1_b[l],
            ffn1_w_in[l].astype(BF16), ffn1_w_out[l].astype(BF16),
            mod(1, 0), mod(1, 1), _mixer_in_weights(w_mix_in[l]), pre_ln=l == 0)
        posk, w1k, w2k = _compress_weights(cmp_pos_k[l], cmp_k_w1[l], cmp_k_w2[l], 0)
        posv, w1v, w2v = _compress_weights(cmp_pos_v[l], cmp_v_w1[l], cmp_v_w2[l], 1)
        kvc = _compress(kc, vc, posk, posv, w1k, w1v, w2k, w2v)
        o = _attention(q, gates, kvc, kvs, kvw)
        x = _merge_ffn(x, mod(1, 2), u, o, gbr, pool_w[l].astype(BF16), pool_scale[l],
                       w_branch_a[l].astype(BF16), w_branch_b[l].astype(BF16), w_mix_out[l].astype(BF16),
                       ln2_g[l], ln2_b[l], mod(2, 0), mod(2, 1), mod(2, 2), ln3_g[l], ln3_b[l],
                       ffn2_w_in[l].astype(BF16), ffn2_w_out[l].astype(BF16))
    return x
```

```python
import functools

import numpy as np
import jax
import jax.numpy as jnp
from jax import lax
from jax.experimental import pallas as pl
from jax.experimental.pallas import tpu as pltpu

F32 = jnp.float32
BF16 = jnp.bfloat16

D_MODEL = 1024
POOL_WIDTH = D_MODEL // 2
POOL_WINDOWS = (2, 4, 8, 16)
POOL_GROUP = POOL_WIDTH // len(POOL_WINDOWS)
POOL_HALO = 32
HEAD_DIM = 64
N_HEADS = (D_MODEL // 2) // HEAD_DIM
N_KV = 2
HEADS_PER_KV = N_HEADS // N_KV
Q_WIDTH = N_HEADS * HEAD_DIM
KV_WIDTH = N_KV * HEAD_DIM
CMP_STRIDE = 16
CMP_BLOCK = 2 * CMP_STRIDE
CMP_HIDDEN = 4 * HEAD_DIM
SLC_BLOCK = 64
N_SELECT = 16
WINDOW = 512
Q_BLOCK = SLC_BLOCK
ALIBI_MAX = 8.0
D_FF = 2816
DEPTH = 1
ALPHA = (2.0 * DEPTH) ** 0.25
LN_EPS = 1e-5
NEG = -1e30
FORCE = 1e9

LANES = 128
KV_PACK = 2 * HEAD_DIM
KEY_WIDTH = 2 * KV_PACK
FEAT_OFFSET_LANE = SLC_BLOCK
BIG = 1e30
M_INIT = -3e38
FF_CHUNK = 256
GATE_CHUNK = 512
ROW_TILE = 512
SLC_UNIT = 256
RANK_ROWS = 2
WIN_NQ = 8
WIN_AHEAD = 1
SEL_NQ = 4
SLC_NQ = 2
TRIP_WIDTHS = (8, 6, 4, 2)
LOG2E = 1.4426950408889634
WIN_KEYS = WINDOW + 2 * Q_BLOCK
VMEM_LIMIT = 56 * 1024 * 1024

_C_U = 0
_C_Q = _C_U + POOL_WIDTH
_C_KC = _C_Q + Q_WIDTH
_C_VC = _C_KC + KV_WIDTH
_C_KVS = _C_VC + KV_WIDTH
_C_KVW = _C_KVS + N_KV * KV_PACK
_C_GN = _C_KVW + N_KV * KV_PACK
_C_GBR = _C_GN + N_KV * LANES
_C_END = _C_GBR + 2 * D_MODEL


def _dot(a, b):
    return jnp.dot(a, b, preferred_element_type=F32)


def _dot_nt(a, b):
    return lax.dot_general(a, b, (((1,), (1,)), ((), ())), preferred_element_type=F32)


def _layer_norm(x, g, b):
    mu = jnp.mean(x, axis=-1, keepdims=True)
    xc = x - mu
    var = jnp.mean(xc * xc, axis=-1, keepdims=True)
    return xc * lax.rsqrt(var + LN_EPS) * g + b


def _masked_softmax(s, mask, any_valid):
    sm = jnp.where(mask, s, NEG)
    m = jnp.max(sm, axis=-1, keepdims=True)
    e = jnp.exp(sm - m)
    return e * jnp.where(any_valid, 1.0 / jnp.sum(e, axis=-1, keepdims=True), 0.0)


def _const_spec(shape):
    nd = len(shape)
    return pl.BlockSpec(shape, lambda *_: (0,) * nd, pipeline_mode=pl.Buffered(1))


def _params(n_grid):
    return pltpu.CompilerParams(dimension_semantics=("parallel",) * n_grid, vmem_limit_bytes=VMEM_LIMIT)


def _ada_kernel(c_ref, w_ref, b_ref, o_ref):
    c = c_ref[...]
    c_act = (c * jax.nn.sigmoid(c)).astype(BF16)
    o_ref[...] = _dot(c_act, w_ref[...].astype(BF16)) + b_ref[...]


def _ada(c, w, b):
    bsz, d = c.shape
    n = w.shape[1]
    tn = D_MODEL
    return pl.pallas_call(
        _ada_kernel,
        grid=(n // tn,),
        in_specs=[pl.BlockSpec((bsz, d), lambda j: (0, 0)),
                  pl.BlockSpec((d, tn), lambda j: (0, j)),
                  pl.BlockSpec((1, tn), lambda j: (0, j))],
        out_specs=pl.BlockSpec((bsz, tn), lambda j: (0, j)),
        out_shape=jax.ShapeDtypeStruct((bsz, n), F32),
        compiler_params=_params(1),
        name="ada_proj",
    )(c, w, b.reshape(1, n))


def _swiglu(h, w_in_ref, w_out_ref):
    acc = jnp.zeros((h.shape[0], w_out_ref.shape[1]), F32)
    for j in range(D_FF // FF_CHUNK):
        c0 = j * FF_CHUNK
        gt = _dot(h, w_in_ref[:, c0:c0 + FF_CHUNK])
        up = _dot(h, w_in_ref[:, D_FF + c0:D_FF + c0 + FF_CHUNK])
        act = (gt * jax.nn.sigmoid(gt) * up).astype(BF16)
        acc = acc + _dot(act, w_out_ref[c0:c0 + FF_CHUNK, :])
    return acc


def _swiglu_block(x, shift, scale, gate, w_in_ref, w_out_ref, g, b):
    h = (x * (1.0 + scale) + shift).astype(BF16)
    return _layer_norm(ALPHA * x + 0.5 * gate * _swiglu(h, w_in_ref, w_out_ref), g, b)


def _ffn_mixin_kernel(x_ref, shift0_ref, scale0_ref, gate0_ref, g0_ref, b0_ref, g1_ref, b1_ref, w_in_ref, w_out_ref,
                      shift1_ref, scale1_ref, w_ref,
                      x1_ref, u_ref, q_ref, kc_ref, vc_ref, kvs_ref, kvw_ref, gn_ref, gbr_ref, *, pre_ln):
    x = x_ref[0]
    if pre_ln:
        x = _layer_norm(x, g0_ref[...], b0_ref[...])
    x1 = _swiglu_block(x, shift0_ref[0], scale0_ref[0], gate0_ref[0], w_in_ref, w_out_ref, g1_ref[...], b1_ref[...])
    x1_ref[0] = x1

    h = (x1 * (1.0 + scale1_ref[0]) + shift1_ref[0]).astype(BF16)

    def proj(c0, c1):
        return _dot(h, w_ref[:, c0:c1])

    tm = x_ref.shape[1]
    pos = pl.program_id(1) * tm + lax.broadcasted_iota(jnp.int32, (tm, LANES), 0)
    lane = lax.broadcasted_iota(jnp.int32, (tm, LANES), 1)
    feat = jnp.where(lane == pos // SLC_BLOCK, 1.0,
                     jnp.where(lane == FEAT_OFFSET_LANE, (pos % SLC_BLOCK).astype(F32), 0.0)).astype(BF16)

    def put_u(z):
        u_ref[0] = z

    def put_q(z):
        q_ref[0] = (z * HEAD_DIM ** -0.5).astype(BF16)

    def put_cmp(z):
        kc_ref[0] = z[:, 0:KV_WIDTH]
        vc_ref[0] = z[:, KV_WIDTH:2 * KV_WIDTH]

    def put_keys(ref):
        def put(z):
            zb = z.astype(BF16)
            for g in range(N_KV):
                ref[0, g, :, 0:KV_PACK] = zb[:, g * KV_PACK:(g + 1) * KV_PACK]
                ref[0, g, :, KV_PACK:KEY_WIDTH] = feat
        return put

    def put_head_gates(z):
        sg = jax.nn.sigmoid(z)
        for g in range(N_KV):
            gn_ref[0, g] = sg[:, g * LANES:(g + 1) * LANES]

    def put_branch_gates(c0):
        def put(z):
            gbr_ref[0, :, c0 - _C_GBR:c0 - _C_GBR + z.shape[1]] = jax.nn.sigmoid(z).astype(BF16)
        return put

    gate_cols = [(c0, c0 + GATE_CHUNK) for c0 in range(_C_GBR, _C_END, GATE_CHUNK)]
    light = [((_C_U, _C_Q), put_u), ((_C_Q, _C_KC), put_q), ((_C_KC, _C_KVS), put_cmp),
             ((_C_KVS, _C_KVW), put_keys(kvs_ref)), ((_C_KVW, _C_GN), put_keys(kvw_ref)),
             ((_C_GN, _C_GBR), put_head_gates)]
    stages = []
    for n in range(max(len(gate_cols), len(light))):
        if n < len(gate_cols):
            stages.append((gate_cols[n], put_branch_gates(gate_cols[n][0])))
        if n < len(light):
            stages.append(light[n])
    pending = proj(*stages[0][0])
    for n, (_, sink) in enumerate(stages):
        z = pending
        if n + 1 < len(stages):
            pending = proj(*stages[n + 1][0])
        sink(z)


def _ffn_mixin(x, shift0, scale0, gate0, g0, b0, g1, b1, w_in, w_out, shift1, scale1, w_mix, *, pre_ln):
    bsz, s, d = x.shape
    tm = min(ROW_TILE, s)

    def row(n, dt):
        return pl.BlockSpec((1, tm, n), lambda b, i: (b, i, 0)), jax.ShapeDtypeStruct((bsz, s, n), dt)

    def grouped(n, dt):
        return (pl.BlockSpec((1, N_KV, tm, n), lambda b, i: (b, 0, i, 0)),
                jax.ShapeDtypeStruct((bsz, N_KV, s, n), dt))

    outs = [row(d, F32), row(POOL_WIDTH, F32), row(Q_WIDTH, BF16), row(KV_WIDTH, F32), row(KV_WIDTH, F32),
            grouped(KEY_WIDTH, BF16), grouped(KEY_WIDTH, BF16), grouped(LANES, F32), row(2 * D_MODEL, BF16)]
    mod = pl.BlockSpec((1, 1, d), lambda b, i: (b, 0, 0))
    vec = _const_spec((1, d))
    return pl.pallas_call(
        functools.partial(_ffn_mixin_kernel, pre_ln=pre_ln),
        grid=(bsz, s // tm),
        in_specs=[row(d, F32)[0], mod, mod, mod, vec, vec, vec, vec,
                  _const_spec(w_in.shape), _const_spec(w_out.shape), mod, mod, _const_spec(w_mix.shape)],
        out_specs=[o[0] for o in outs],
        out_shape=[o[1] for o in outs],
        compiler_params=_params(2),
        name="ffn_mixer_in",
    )(x, shift0, scale0, gate0, g0.reshape(1, d), b0.reshape(1, d), g1.reshape(1, d), b1.reshape(1, d),
      w_in, w_out, shift1, scale1, w_mix)


def _compress_kernel(k_ref, v_ref, posk_ref, posv_ref, w1k_ref, w1v_ref, w2k_ref, w2v_ref, o_ref, nxt_ref):
    r = k_ref.shape[1] // CMP_STRIDE

    def hidden(x_ref, pos_ref, w1_ref):
        x = jnp.concatenate([x_ref[0, pl.ds(j, r, stride=CMP_STRIDE), :] for j in range(CMP_STRIDE)], axis=1)
        first = _dot((x + pos_ref[0:1, :]).astype(BF16), w1_ref[0])
        nxt_ref[0:r, :] = _dot((x + pos_ref[1:2, :]).astype(BF16), w1_ref[1])
        nxt_ref[r:r + 8, :] = jnp.zeros((8, nxt_ref.shape[1]), F32)
        return jax.nn.gelu(first + nxt_ref[1:r + 1, :]).astype(BF16)

    hk = hidden(k_ref, posk_ref, w1k_ref)
    hv = hidden(v_ref, posv_ref, w1v_ref)
    o_ref[0] = (_dot(hk, w2k_ref[...]) + _dot(hv, w2v_ref[...])).astype(BF16)


def _compress(kc, vc, posk, posv, w1k, w1v, w2k, w2v):
    bsz, s, n = kc.shape
    r = s // CMP_STRIDE
    hid = w1k.shape[2]
    row = pl.BlockSpec((1, s, n), lambda b: (b, 0, 0))
    return pl.pallas_call(
        _compress_kernel,
        grid=(bsz,),
        in_specs=[row, row, _const_spec(posk.shape), _const_spec(posv.shape),
                  _const_spec(w1k.shape), _const_spec(w1v.shape),
                  _const_spec(w2k.shape), _const_spec(w2v.shape)],
        out_specs=pl.BlockSpec((1, r, N_KV * KV_PACK), lambda b: (b, 0, 0)),
        out_shape=jax.ShapeDtypeStruct((bsz, r, N_KV * KV_PACK), BF16),
        scratch_shapes=[pltpu.VMEM((r + 8, hid), F32)],
        compiler_params=_params(1),
        name="compress_mlp",
    )(kc, vc, posk, posv, w1k, w1v, w2k, w2v)


def _alibi_slope(head):
    return 2.0 ** (-ALIBI_MAX * (head + 1) / N_HEADS)


def _padded_heads(q_tile):
    lane = lax.broadcasted_iota(jnp.int32, (Q_BLOCK, LANES), 1)
    qf = q_tile.astype(F32)
    heads = []
    for hd in range(N_HEADS):
        slab = qf[:, (hd // 2) * LANES:(hd // 2 + 1) * LANES]
        if hd % 2:
            slab = pltpu.roll(slab, HEAD_DIM, 1)
        heads.append(jnp.where(lane < HEAD_DIM, slab, 0.0))
    return heads


def _pack_heads(o_rows):
    lane = lax.broadcasted_iota(jnp.int32, (Q_BLOCK, LANES), 1)
    pairs = []
    for j in range(HEADS_PER_KV // 2):
        even = o_rows[(2 * j) * Q_BLOCK:(2 * j + 1) * Q_BLOCK]
        odd = o_rows[(2 * j + 1) * Q_BLOCK:(2 * j + 2) * Q_BLOCK]
        pairs.append(jnp.where(lane < HEAD_DIM, pltpu.roll(even, HEAD_DIM, 1), odd))
    return jnp.concatenate(pairs, axis=1)


def _feature_rows(qb, hd):
    lane_row = lax.broadcasted_iota(jnp.int32, (1, LANES), 1)
    slope = _alibi_slope(hd)
    tail = jnp.where(lane_row == FEAT_OFFSET_LANE, slope, 0.0)
    block_bias = slope * SLC_BLOCK * (lane_row - qb).astype(F32)
    in_window = (lane_row >= qb - WINDOW // SLC_BLOCK) & (lane_row <= qb)
    slc_row = jnp.where(lane_row < SLC_BLOCK, block_bias, tail)
    win_row = jnp.where(lane_row < SLC_BLOCK, jnp.where(in_window, block_bias, -BIG), tail)
    return slc_row, win_row


def _window_kernel(q_ref, kpw_ref, triw_ref, o_ref):
    gw = HEADS_PER_KV * HEAD_DIM
    chains = [(k, g) for k in range(WIN_NQ) for g in range(N_KV)]

    def scores(k, g):
        qb = pl.program_id(1) * WIN_NQ + k
        heads = _padded_heads(q_ref[0, k * Q_BLOCK:(k + 1) * Q_BLOCK, :])
        w0 = pl.multiple_of(jnp.maximum((qb + 1) * Q_BLOCK - WIN_KEYS, 0), Q_BLOCK)
        tri = jnp.concatenate([triw_ref[jnp.minimum(qb, WIN_KEYS // SLC_BLOCK - 1)]] * HEADS_PER_KV, axis=0)
        qw = [jnp.concatenate([heads[hd], jnp.broadcast_to(_feature_rows(qb, hd)[1], (Q_BLOCK, LANES))], axis=1)
              for hd in range(g * HEADS_PER_KV, (g + 1) * HEADS_PER_KV)]
        kp = kpw_ref[0, g, pl.ds(w0, WIN_KEYS), :]
        return _dot_nt(jnp.concatenate(qw, axis=0).astype(BF16), kp) + tri, kp

    def finish(k, g, o_raw, inv_l):
        o_ref[0, k * Q_BLOCK:(k + 1) * Q_BLOCK, g * gw:(g + 1) * gw] = _pack_heads(o_raw * inv_l).astype(BF16)

    pending = [scores(*chains[n]) for n in range(WIN_AHEAD)]
    unfinished = None
    for n, (k, g) in enumerate(chains):
        s, kp = pending.pop(0)
        if n + WIN_AHEAD < len(chains):
            pending.append(scores(*chains[n + WIN_AHEAD]))
        e = jnp.exp(s - jnp.max(s, axis=-1, keepdims=True))
        o_raw = _dot(e.astype(BF16), kp[:, 0:KV_PACK])
        inv_l = 1.0 / jnp.sum(e, axis=-1, keepdims=True)
        if unfinished is not None:
            finish(*unfinished)
        unfinished = (k, g, o_raw, inv_l)
    finish(*unfinished)


def _window(q, kpw, tri_win):
    bsz, s, _ = q.shape
    qt = WIN_NQ * Q_BLOCK
    return pl.pallas_call(
        _window_kernel,
        grid=(bsz, s // qt),
        in_specs=[pl.BlockSpec((1, qt, Q_WIDTH), lambda b, i: (b, i, 0)),
                  pl.BlockSpec((1, N_KV, s, KEY_WIDTH), lambda b, i: (b, 0, 0, 0)),
                  _const_spec(tri_win.shape)],
        out_specs=pl.BlockSpec((1, qt, Q_WIDTH), lambda b, i: (b, i, 0)),
        out_shape=jax.ShapeDtypeStruct((bsz, s, Q_WIDTH), BF16),
        compiler_params=_params(2),
        name="window_attention",
    )(q, kpw, tri_win)


def _select_kernel(q_ref, kvc_ref, cbias_ref, ovl_ref, eye_ref, ocmp_ref, selneg_ref, first_ref, score_ref, *, n_sel):
    step = pl.program_id(1)
    rows = SEL_NQ * HEADS_PER_KV * Q_BLOCK
    n_slc = ovl_ref.shape[0]
    r = kvc_ref.shape[1]
    lanes_q = SEL_NQ * N_KV * Q_BLOCK
    heads = [_padded_heads(q_ref[0, k * Q_BLOCK:(k + 1) * Q_BLOCK, :]) for k in range(SEL_NQ)]

    row = lax.broadcasted_iota(jnp.int32, (rows, 1), 0)
    t = (step * SEL_NQ + row // (HEADS_PER_KV * Q_BLOCK)) * Q_BLOCK + (row & (Q_BLOCK - 1))
    last_cmp = (t - (CMP_BLOCK - 1)) >> 4
    visible = lax.broadcasted_iota(jnp.int32, (rows, r), 1) <= last_cmp
    p_sums = [[None] * N_KV for _ in range(SEL_NQ)]
    gw = HEADS_PER_KV * HEAD_DIM
    def cmp_scores(g):
        q_pad = jnp.concatenate([heads[k][g * HEADS_PER_KV + h] for k in range(SEL_NQ)
                                 for h in range(HEADS_PER_KV)], axis=0).astype(BF16)
        return _dot_nt(q_pad, kvc_ref[0, :, g * KV_PACK:(g + 1) * KV_PACK])

    raw = [cmp_scores(g) for g in range(N_KV)]
    for g in range(N_KV):
        kvc = kvc_ref[0, :, g * KV_PACK:(g + 1) * KV_PACK]
        p_cmp = _masked_softmax(raw[g] + cbias_ref[g], visible, last_cmp >= 0)
        o_cmp = _dot(p_cmp.astype(BF16), kvc)
        for k in range(SEL_NQ):
            base = k * HEADS_PER_KV * Q_BLOCK
            ocmp_ref[0, k * Q_BLOCK:(k + 1) * Q_BLOCK, g * gw:(g + 1) * gw] = _pack_heads(
                o_cmp[base:base + HEADS_PER_KV * Q_BLOCK]).astype(BF16)
            p_sum = p_cmp[base:base + Q_BLOCK]
            for h in range(1, HEADS_PER_KV):
                p_sum = p_sum + p_cmp[base + h * Q_BLOCK:base + (h + 1) * Q_BLOCK]
            p_sums[k][g] = p_sum

    p_all = jnp.concatenate([p_sums[k][g] for k in range(SEL_NQ) for g in range(N_KV)], axis=0)
    p_hi = p_all.astype(BF16)
    p_lo = (p_all - p_hi.astype(F32)).astype(BF16)
    ovl = ovl_ref[...]
    imp_t = _dot_nt(ovl, p_hi) + _dot_nt(ovl, p_lo)

    blk = lax.broadcasted_iota(jnp.int32, (n_slc, lanes_q), 0)
    qb = step * SEL_NQ + lax.broadcasted_iota(jnp.int32, (n_slc, lanes_q), 1) // (N_KV * Q_BLOCK)
    forced = (blk == 0) | (blk == qb) | (blk == qb - 1)
    score = jnp.where(blk > qb, NEG, jnp.where(forced, FORCE, jnp.where(imp_t > 0.0, imp_t, 0.0)))
    key = lax.bitcast_convert_type(score, jnp.int32)
    key_prev = key - 1
    score_ref[...] = key

    def rank_step(i, rank):
        wins = []
        for u in range(RANK_ROWS):
            row_i = i * RANK_ROWS + u
            ri = score_ref[pl.ds(row_i, 1), :]
            wins.append(jnp.where(ri > jnp.where(blk > row_i, key_prev, key), 1.0, 0.0))
        return rank + functools.reduce(jnp.add, wins)

    rank = lax.fori_loop(0, (step * SEL_NQ + SEL_NQ - 1) // RANK_ROWS + 1, rank_step,
                         jnp.zeros((n_slc, lanes_q), F32))
    chosen = (rank < n_sel) & (blk <= qb)

    oldest = jnp.where(chosen & (blk >= 1), blk, n_slc).astype(F32)
    sel_t = jnp.where(chosen, 1.0, 0.0).astype(BF16)
    sel_t = jnp.concatenate([sel_t, jnp.zeros((LANES - n_slc, lanes_q), BF16)], axis=0)
    sel = _dot_nt(eye_ref[...], sel_t)
    lane = lax.broadcasted_iota(jnp.int32, (Q_BLOCK, LANES), 1)
    for k in range(SEL_NQ):
        lanes_k = slice(k * N_KV * Q_BLOCK, (k + 1) * N_KV * Q_BLOCK)
        first = jnp.min(jnp.min(oldest[:, lanes_k], axis=1, keepdims=True), axis=0, keepdims=True)
        first_ref[0, k] = jnp.broadcast_to(first, first_ref.shape[2:]).astype(jnp.int32)
        for g in range(N_KV):
            base = (k * N_KV + g) * Q_BLOCK
            selneg_ref[0, g, k * Q_BLOCK:(k + 1) * Q_BLOCK, :] = jnp.where(
                lane < SLC_BLOCK, (sel[base:base + Q_BLOCK] - 1.0) * BIG, 0.0).astype(BF16)


def _select(q, kvc, cbias, overlap_t, n_sel):
    bsz, s, _ = q.shape
    r = kvc.shape[1]
    n_slc = overlap_t.shape[0]
    qt = SEL_NQ * Q_BLOCK
    eye = jnp.asarray(np.eye(N_KV * qt), BF16)
    return pl.pallas_call(
        functools.partial(_select_kernel, n_sel=n_sel),
        grid=(bsz, s // qt),
        in_specs=[pl.BlockSpec((1, qt, Q_WIDTH), lambda b, i: (b, i, 0)),
                  pl.BlockSpec((1, r, N_KV * KV_PACK), lambda b, i: (b, 0, 0)),
                  _const_spec(cbias.shape), _const_spec(overlap_t.shape), _const_spec(eye.shape)],
        out_specs=[pl.BlockSpec((1, qt, Q_WIDTH), lambda b, i: (b, i, 0)),
                   pl.BlockSpec((1, N_KV, qt, LANES), lambda b, i: (b, 0, i, 0)),
                   pl.BlockSpec((1, SEL_NQ, 8, LANES), lambda b, i: (b, i, 0, 0))],
        out_shape=[jax.ShapeDtypeStruct((bsz, s, Q_WIDTH), BF16),
                   jax.ShapeDtypeStruct((bsz, N_KV, s, LANES), BF16),
                   jax.ShapeDtypeStruct((bsz, s // Q_BLOCK, 8, LANES), jnp.int32)],
        scratch_shapes=[pltpu.VMEM((n_slc, SEL_NQ * N_KV * Q_BLOCK), jnp.int32)],
        compiler_params=_params(2),
        name="compressed_attention_select",
    )(q, kvc, cbias, overlap_t, eye)


def _selected_kernel(first_ref, q_ref, selneg_ref, gate_ref, ocmp_ref, owin_ref, kps_ref, tris_ref,
                     o_ref, qs_ref, s_ref, mt_ref, m_ref, lt_ref, acc_ref):
    qbs = [pl.program_id(1) * SLC_NQ + k for k in range(SLC_NQ)]
    rows = SLC_NQ * HEADS_PER_KV * Q_BLOCK
    heads = [_padded_heads(q_ref[0, k * Q_BLOCK:(k + 1) * Q_BLOCK, :]) for k in range(SLC_NQ)]
    for g in range(N_KV):
        qs = []
        for k in range(SLC_NQ):
            masked_out = selneg_ref[0, g, k * Q_BLOCK:(k + 1) * Q_BLOCK, :].astype(F32)
            qs += [jnp.concatenate([heads[k][hd], masked_out + _feature_rows(qbs[k], hd)[0]], axis=1)
                   for hd in range(g * HEADS_PER_KV, (g + 1) * HEADS_PER_KV)]
        qs_ref[g] = jnp.concatenate(qs, axis=0).astype(BF16)

    unit_blocks = SLC_UNIT // SLC_BLOCK
    diag_unit = qbs[0] // unit_blocks
    first = functools.reduce(jnp.minimum, [first_ref[pl.program_id(0), qb] for qb in qbs])
    first_unit = jnp.minimum(first, qbs[0]) // unit_blocks
    lead = jnp.minimum(first_unit, 1)
    n_units = diag_unit - first_unit + 1 + lead

    def unit_of(i):
        return jnp.where((i < lead) | (i >= n_units), 0, first_unit + i - lead)

    def unit_start(i):
        return pl.multiple_of(unit_of(i) * SLC_UNIT, SLC_UNIT)

    def slot_start(i):
        return pl.multiple_of(i * SLC_UNIT, SLC_UNIT)

    def trip_chains(first_slot, width):
        return [(g, first_slot + u) for u in range(width) for g in range(N_KV)]

    def score_matmul(g, i):
        return _dot_nt(qs_ref[g], kps_ref[0, g, pl.ds(unit_start(i), SLC_UNIT), :])

    def score_finish(g, i, raw):
        tri = []
        for qb in qbs:
            variant = jnp.where(i >= n_units, unit_blocks + 1,
                                jnp.where(unit_of(i) == diag_unit, qb % unit_blocks, unit_blocks))
            tri += [tris_ref[variant]] * HEADS_PER_KV
        s = (raw + jnp.concatenate(tri, axis=0)) * LOG2E
        s_ref[g, :, pl.ds(slot_start(i), SLC_UNIT)] = s
        mt_ref[g] = jnp.maximum(jnp.maximum(mt_ref[g], s[:, 0:LANES]), s[:, LANES:SLC_UNIT])

    def score_trip(first_slot, width):
        chains = trip_chains(first_slot, width)
        pending = score_matmul(*chains[0])
        for n, (g, i) in enumerate(chains):
            raw = pending
            if n + 1 < len(chains):
                pending = score_matmul(*chains[n + 1])
            score_finish(g, i, raw)

    def exponentials(g, i):
        m = m_ref[g]
        es = [jnp.exp2(s_ref[g, :, pl.ds(slot_start(i) + j * LANES, LANES)] - m) for j in range(SLC_UNIT // LANES)]
        lt_ref[g] = lt_ref[g] + functools.reduce(jnp.add, es)
        return jnp.concatenate(es, axis=1).astype(BF16)

    def value_trip(first_slot, width):
        chains = trip_chains(first_slot, width)
        pending = exponentials(*chains[0])
        for n, (g, i) in enumerate(chains):
            e = pending
            if n + 1 < len(chains):
                pending = exponentials(*chains[n + 1])
            acc_ref[g] = acc_ref[g] + _dot(e, kps_ref[0, g, pl.ds(unit_start(i), SLC_UNIT), 0:KV_PACK])

    def sweep(trip):
        done = 0
        for tier, width in enumerate(TRIP_WIDTHS):
            left = n_units - done
            if tier + 1 < len(TRIP_WIDTHS):
                count = left // width + jnp.where(left % width > TRIP_WIDTHS[tier + 1], 1, 0)
            else:
                count = (left + width - 1) // width
            count = jnp.maximum(count, 0)

            def body(p, carry, base=done, width=width):
                trip(base + p * width, width)
                return carry

            lax.fori_loop(0, count, body, 0)
            done = done + count * width

    mt_ref[...] = jnp.full(mt_ref.shape, M_INIT, F32)
    sweep(score_trip)
    for g in range(N_KV):
        m_ref[g] = jnp.broadcast_to(jnp.max(mt_ref[g], axis=-1, keepdims=True), (rows, LANES))
    lt_ref[...] = jnp.zeros(lt_ref.shape, F32)
    acc_ref[...] = jnp.zeros(acc_ref.shape, F32)
    sweep(value_trip)

    lane = lax.broadcasted_iota(jnp.int32, (Q_BLOCK, LANES), 1)
    for g in range(N_KV):
        l = jnp.sum(lt_ref[g], axis=-1, keepdims=True)
        o_rows = acc_ref[g] * jnp.where(l > 0.0, 1.0 / l, 0.0)
        for k in range(SLC_NQ):
            qr = slice(k * Q_BLOCK, (k + 1) * Q_BLOCK)
            o_slc = _pack_heads(o_rows[k * HEADS_PER_KV * Q_BLOCK:(k + 1) * HEADS_PER_KV * Q_BLOCK])
            gt = gate_ref[0, g, qr, :]
            for j in range(HEADS_PER_KV // 2):
                cols = slice((g * HEADS_PER_KV // 2 + j) * LANES, (g * HEADS_PER_KV // 2 + j + 1) * LANES)
                branches = (ocmp_ref[0, qr, cols].astype(F32), o_slc[:, j * LANES:(j + 1) * LANES],
                            owin_ref[0, qr, cols].astype(F32))
                total = None
                for c, branch in enumerate(branches):
                    even, odd = 3 * (2 * j) + c, 3 * (2 * j + 1) + c
                    gate = jnp.take_along_axis(gt, jnp.where(lane < HEAD_DIM, even, odd), axis=1)
                    total = gate * branch if total is None else total + gate * branch
                o_ref[0, qr, cols] = total.astype(BF16)


def _selected(first, q, selneg, gates, o_cmp, o_win, kps, tri_slc):
    bsz, s, _ = q.shape
    qt = SLC_NQ * Q_BLOCK
    rows = HEADS_PER_KV * qt
    assert (SLC_UNIT // SLC_BLOCK) % SLC_NQ == 0 and s % qt == 0

    def row(n):
        return pl.BlockSpec((1, qt, n), lambda b, i, first_ref: (b, i, 0))

    def grouped(n):
        return pl.BlockSpec((1, N_KV, qt, n), lambda b, i, first_ref: (b, 0, i, 0))

    grid_spec = pltpu.PrefetchScalarGridSpec(
        num_scalar_prefetch=1,
        grid=(bsz, s // qt),
        in_specs=[row(Q_WIDTH), grouped(LANES), grouped(LANES), row(Q_WIDTH), row(Q_WIDTH),
                  pl.BlockSpec((1, N_KV, s, KEY_WIDTH), lambda b, i, first_ref: (b, 0, 0, 0)),
                  _const_spec(tri_slc.shape)],
        out_specs=row(Q_WIDTH),
        scratch_shapes=[pltpu.VMEM((N_KV, rows, KEY_WIDTH), BF16),
                        pltpu.VMEM((N_KV, rows, s + SLC_UNIT), F32)]
        + [pltpu.VMEM((N_KV, rows, LANES), F32)] * 4)
    return pl.pallas_call(
        _selected_kernel,
        grid_spec=grid_spec,
        out_shape=jax.ShapeDtypeStruct((bsz, s, Q_WIDTH), BF16),
        compiler_params=_params(2),
        name="selected_attention",
    )(first, q, selneg, gates, o_cmp, o_win, kps, tri_slc)


def _attention(q, gates, kvc, kps, kpw):
    bsz, s, _ = q.shape
    r = kvc.shape[1]
    n_cmp = r - 1
    n_slc = s // SLC_BLOCK
    n_sel = min(N_SELECT, n_slc)
    rows = HEADS_PER_KV * Q_BLOCK
    assert n_slc <= SLC_BLOCK and n_slc % RANK_ROWS == 0 and s % SLC_UNIT == 0 and s >= WIN_KEYS
    assert s % (WIN_NQ * Q_BLOCK) == 0 and s % (SEL_NQ * Q_BLOCK) == 0

    slopes = np.array([_alibi_slope(hd) for hd in range(N_HEADS)])
    slope_rows = np.repeat(slopes.reshape(N_KV, HEADS_PER_KV), Q_BLOCK, axis=1).reshape(N_KV, rows, 1)
    cbias = np.tile(slope_rows * (CMP_STRIDE * np.arange(r))[None, None, :], (1, SEL_NQ, 1))
    start = np.arange(r)[None, :] * CMP_STRIDE
    blk = np.arange(n_slc)[:, None] * SLC_BLOCK
    overlap_t = ((start < blk + SLC_BLOCK) & (start + CMP_BLOCK > blk) & (np.arange(r)[None, :] < n_cmp))

    ql = np.arange(Q_BLOCK)[:, None]
    kl = np.arange(SLC_BLOCK)[None, :]
    lower = np.where(kl > ql, -BIG, 0.0)
    upper = np.where(kl <= ql, -BIG, 0.0)
    unit_blocks = SLC_UNIT // SLC_BLOCK
    tri_slc = np.zeros((unit_blocks + 2, Q_BLOCK, SLC_UNIT))
    for j in range(unit_blocks):
        tri_slc[j, :, j * SLC_BLOCK:(j + 1) * SLC_BLOCK] = lower
    tri_slc[unit_blocks + 1] = -BIG
    win_blocks = WINDOW // SLC_BLOCK
    lead = WIN_KEYS // SLC_BLOCK - 1
    tri_win = np.zeros((lead + 1, Q_BLOCK, WIN_KEYS))
    for v in range(lead + 1):
        diag = v
        tri_win[v, :, diag * SLC_BLOCK:(diag + 1) * SLC_BLOCK] = lower
        if diag >= win_blocks:
            old = diag - win_blocks
            tri_win[v, :, old * SLC_BLOCK:(old + 1) * SLC_BLOCK] = upper

    o_win = _window(q, kpw, jnp.asarray(tri_win, F32))
    o_cmp, selneg, first = _select(q, kvc, jnp.asarray(cbias, F32), jnp.asarray(overlap_t, BF16), n_sel)
    return _selected(first[:, :, 0, 0], q, selneg, gates, o_cmp, o_win, kps, jnp.asarray(tri_slc, F32))


def _merge_ffn_kernel(x_ref, gate1_ref, uprev_ref, u_ref, o_ref, gbr_ref, pw_ref, ps_ref, wa_ref, wb_ref, wo_ref,
                      g2_ref, b2_ref, shift2_ref, scale2_ref, gate2_ref, g3_ref, b3_ref, w_in_ref, w_out_ref,
                      out_ref, ubuf_ref, sums_ref):
    i = pl.program_id(1)
    tm = u_ref.shape[1]
    ubuf_ref[0:POOL_HALO, :] = jnp.where(i == 0, 0.0, uprev_ref[0])
    ubuf_ref[POOL_HALO:POOL_HALO + tm, :] = u_ref[0]
    t = i * tm + lax.broadcasted_iota(jnp.int32, (tm, 1), 0)

    y_b = _dot(o_ref[0], wb_ref[...])

    end = POOL_HALO + tm
    totals = []
    for level in range(len(POOL_WINDOWS)):
        width = 1 << level
        start = 8 * (level + 1)
        c0 = level * POOL_GROUP
        src = (lambda r0, r1, c=c0: ubuf_ref[r0:r1, c:POOL_WIDTH]) if level == 0 else (
            lambda r0, r1, c=c0, n=level - 1: sums_ref[n, r0:r1, c:POOL_WIDTH])
        doubled = src(start, end) + src(start - width, end - width)
        if level + 1 < len(POOL_WINDOWS):
            sums_ref[level, start:end, c0:POOL_WIDTH] = doubled
        totals.append(doubled[POOL_HALO - start:, 0:POOL_GROUP])

    mixed = []
    for gi, w in enumerate(POOL_WINDOWS):
        assert w == 2 << gi
        cs = slice(gi * POOL_GROUP, (gi + 1) * POOL_GROUP)
        cur = ubuf_ref[POOL_HALO:POOL_HALO + tm, cs]
        total = totals[gi]
        inv_cnt = 1.0 / jnp.minimum(t + 1, w).astype(F32)
        delta = (total * inv_cnt - cur).astype(BF16)
        mixed.append((_dot(delta, pw_ref[gi]) * ps_ref[:, cs]).astype(BF16))
    y_a = _dot(jnp.concatenate(mixed, axis=1), wa_ref[...])
    d = y_a.shape[1]
    y = (gbr_ref[0, :, 0:d].astype(F32) * y_a + gbr_ref[0, :, d:2 * d].astype(F32) * y_b).astype(BF16)
    y = _dot(y, wo_ref[...])
    x2 = _layer_norm(ALPHA * x_ref[0] + gate1_ref[0] * y, g2_ref[...], b2_ref[...])
    out_ref[0] = _swiglu_block(x2, shift2_ref[0], scale2_ref[0], gate2_ref[0], w_in_ref, w_out_ref,
                               g3_ref[...], b3_ref[...])


def _merge_ffn(x, gate1, u, o, gbr, pool_w, pool_scale, w_a, w_b, w_o, g2, b2,
               shift2, scale2, gate2, g3, b3, w_in, w_out):
    bsz, s, d = x.shape
    tm = min(ROW_TILE, s)
    halo_blocks = tm // POOL_HALO

    def row(n):
        return pl.BlockSpec((1, tm, n), lambda bi, i: (bi, i, 0))

    mod = pl.BlockSpec((1, 1, d), lambda bi, i: (bi, 0, 0))
    vec = _const_spec((1, d))
    return pl.pallas_call(
        _merge_ffn_kernel,
        grid=(bsz, s // tm),
        in_specs=[row(d), mod,
                  pl.BlockSpec((1, POOL_HALO, POOL_WIDTH),
                               lambda bi, i: (bi, jnp.maximum(i * halo_blocks - 1, 0), 0)),
                  row(POOL_WIDTH), row(Q_WIDTH), row(2 * d),
                  _const_spec(pool_w.shape), _const_spec((1, POOL_WIDTH)),
                  _const_spec(w_a.shape), _const_spec(w_b.shape), _const_spec(w_o.shape),
                  vec, vec, mod, mod, mod, vec, vec, _const_spec(w_in.shape), _const_spec(w_out.shape)],
        out_specs=row(d),
        out_shape=jax.ShapeDtypeStruct((bsz, s, d), F32),
        scratch_shapes=[pltpu.VMEM((POOL_HALO + tm, POOL_WIDTH), F32),
                        pltpu.VMEM((len(POOL_WINDOWS) - 1, POOL_HALO + tm, POOL_WIDTH), F32)],
        compiler_params=_params(2),
        name="pool_merge_ffn",
    )(x, gate1, u, u, o, gbr, pool_w, pool_scale.reshape(1, POOL_WIDTH), w_a, w_b, w_o,
      g2.reshape(1, d), b2.reshape(1, d), shift2, scale2, gate2, g3.reshape(1, d), b3.reshape(1, d), w_in, w_out)


def _mixer_in_weights(w):
    sizes = (POOL_WIDTH, Q_WIDTH) + (KV_WIDTH,) * 6 + (3 * N_HEADS, 2 * D_MODEL)
    offs = np.concatenate([[0], np.cumsum(sizes)])
    u, q, k_cmp, v_cmp, k_slc, v_slc, k_win, v_win, g_nsa, g_br = [w[:, offs[i]:offs[i + 1]] for i in range(10)]
    cols = [u, q, k_cmp, v_cmp]
    for k, v in ((k_slc, v_slc), (k_win, v_win)):
        for g in range(N_KV):
            cols += [k[:, g * HEAD_DIM:(g + 1) * HEAD_DIM], v[:, g * HEAD_DIM:(g + 1) * HEAD_DIM]]
    per_g = 3 * HEADS_PER_KV
    for g in range(N_KV):
        cols += [g_nsa[:, g * per_g:(g + 1) * per_g], jnp.zeros((w.shape[0], LANES - per_g), w.dtype)]
    cols.append(g_br)
    return jnp.concatenate(cols, axis=1).astype(BF16)


def _compress_weights(pos, w1, w2, value_slot):
    same_group = np.eye(N_KV, dtype=np.float32)

    def expand_w1(half):
        wh = half.reshape(CMP_STRIDE, 1, HEAD_DIM, 1, CMP_HIDDEN)
        z = wh * same_group.reshape(1, N_KV, 1, N_KV, 1)
        return z.reshape(CMP_STRIDE * KV_WIDTH, N_KV * CMP_HIDDEN)

    half_rows = CMP_STRIDE * HEAD_DIM
    w1_big = jnp.stack([expand_w1(w1[:half_rows]), expand_w1(w1[half_rows:])]).astype(BF16)
    slot = np.eye(2, dtype=np.float32)[value_slot]
    w2_big = (w2.reshape(1, CMP_HIDDEN, 1, 1, HEAD_DIM) * same_group.reshape(N_KV, 1, N_KV, 1, 1)
              * slot.reshape(1, 1, 1, 2, 1))
    w2_big = w2_big.reshape(N_KV * CMP_HIDDEN, N_KV * KV_PACK).astype(BF16)
    pos_rows = jnp.broadcast_to(pos.reshape(2, CMP_STRIDE, 1, HEAD_DIM), (2, CMP_STRIDE, N_KV, HEAD_DIM))
    return pos_rows.reshape(2, CMP_STRIDE * KV_WIDTH), w1_big, w2_big


def kernel(x, c, ln_in_g, ln_in_b, w_ada, b_ada, ffn1_w_in, ffn1_w_out, ln1_g, ln1_b, w_mix_in, pool_w, pool_scale,
           cmp_pos_k, cmp_k_w1, cmp_k_w2, cmp_pos_v, cmp_v_w1, cmp_v_w2, w_branch_a, w_branch_b, w_mix_out,
           ln2_g, ln2_b, ffn2_w_in, ffn2_w_out, ln3_g, ln3_b):
    bsz, s, d = x.shape
    for l in range(DEPTH):
        ada = _ada(c, w_ada[l], b_ada[l]).reshape(bsz, 3, 3, 1, d)
        mod = lambda i, j: ada[:, i, j]

        x, u, q, kc, vc, kvs, kvw, gates, gbr = _ffn_mixin(
            x, mod(0, 0), mod(0, 1), mod(0, 2), ln_in_g, ln_in_b, ln1_g[l], ln1_b[l],
            ffn1_w_in[l].astype(BF16), ffn1_w_out[l].astype(BF16),
            mod(1, 0), mod(1, 1), _mixer_in_weights(w_mix_in[l]), pre_ln=l == 0)
        posk, w1k, w2k = _compress_weights(cmp_pos_k[l], cmp_k_w1[l], cmp_k_w2[l], 0)
        posv, w1v, w2v = _compress_weights(cmp_pos_v[l], cmp_v_w1[l], cmp_v_w2[l], 1)
        kvc = _compress(kc, vc, posk, posv, w1k, w1v, w2k, w2v)
        o = _attention(q, gates, kvc, kvs, kvw)
        x = _merge_ffn(x, mod(1, 2), u, o, gbr, pool_w[l].astype(BF16), pool_scale[l],
                       w_branch_a[l].astype(BF16), w_branch_b[l].astype(BF16), w_mix_out[l].astype(BF16),
                       ln2_g[l], ln2_b[l], mod(2, 0), mod(2, 1), mod(2, 2), ln3_g[l], ln3_b[l],
                       ffn2_w_in[l].astype(BF16), ffn2_w_out[l].astype(BF16))
    return x
```

```python
import functools

import numpy as np
import jax
import jax.numpy as jnp
from jax import lax
from jax.experimental import pallas as pl
from jax.experimental.pallas import tpu as pltpu

F32 = jnp.float32
BF16 = jnp.bfloat16

D_MODEL = 1024
POOL_WIDTH = D_MODEL // 2
POOL_WINDOWS = (2, 4, 8, 16)
POOL_GROUP = POOL_WIDTH // len(POOL_WINDOWS)
POOL_HALO = 32
HEAD_DIM = 64
N_HEADS = (D_MODEL // 2) // HEAD_DIM
N_KV = 2
HEADS_PER_KV = N_HEADS // N_KV
Q_WIDTH = N_HEADS * HEAD_DIM
KV_WIDTH = N_KV * HEAD_DIM
CMP_STRIDE = 16
CMP_BLOCK = 2 * CMP_STRIDE
CMP_HIDDEN = 4 * HEAD_DIM
SLC_BLOCK = 64
N_SELECT = 16
WINDOW = 512
Q_BLOCK = SLC_BLOCK
ALIBI_MAX = 8.0
D_FF = 2816
DEPTH = 1
ALPHA = (2.0 * DEPTH) ** 0.25
LN_EPS = 1e-5
NEG = -1e30
FORCE = 1e9

LANES = 128
KV_PACK = 2 * HEAD_DIM
KEY_WIDTH = 2 * KV_PACK
FEAT_OFFSET_LANE = SLC_BLOCK
BIG = 1e30
M_INIT = -3e38
FF_CHUNK = 256
GATE_CHUNK = 512
ROW_TILE = 512
SLC_UNIT = 256
RANK_ROWS = 2
WIN_NQ = 16
WIN_AHEAD = 1
SEL_NQ = 4
SLC_NQ = 2
TRIP_WIDTHS = (8, 6, 4, 2)
LOG2E = 1.4426950408889634
WIN_KEYS = WINDOW + 2 * Q_BLOCK
VMEM_LIMIT = 56 * 1024 * 1024

_C_U = 0
_C_Q = _C_U + POOL_WIDTH
_C_KC = _C_Q + Q_WIDTH
_C_VC = _C_KC + KV_WIDTH
_C_KVS = _C_VC + KV_WIDTH
_C_KVW = _C_KVS + N_KV * KV_PACK
_C_GN = _C_KVW + N_KV * KV_PACK
_C_GBR = _C_GN + N_KV * LANES
_C_END = _C_GBR + 2 * D_MODEL


def _dot(a, b):
    return jnp.dot(a, b, preferred_element_type=F32)


def _dot_nt(a, b):
    return lax.dot_general(a, b, (((1,), (1,)), ((), ())), preferred_element_type=F32)


def _layer_norm(x, g, b):
    mu = jnp.mean(x, axis=-1, keepdims=True)
    xc = x - mu
    var = jnp.mean(xc * xc, axis=-1, keepdims=True)
    return xc * lax.rsqrt(var + LN_EPS) * g + b


def _masked_softmax(s, mask, any_valid):
    sm = jnp.where(mask, s, NEG)
    m = jnp.max(sm, axis=-1, keepdims=True)
    e = jnp.exp(sm - m)
    return e * jnp.where(any_valid, 1.0 / jnp.sum(e, axis=-1, keepdims=True), 0.0)


def _const_spec(shape):
    nd = len(shape)
    return pl.BlockSpec(shape, lambda *_: (0,) * nd, pipeline_mode=pl.Buffered(1))


def _params(n_grid):
    return pltpu.CompilerParams(dimension_semantics=("parallel",) * n_grid, vmem_limit_bytes=VMEM_LIMIT)


def _ada_kernel(c_ref, w_ref, b_ref, o_ref):
    c = c_ref[...]
    c_act = (c * jax.nn.sigmoid(c)).astype(BF16)
    o_ref[...] = _dot(c_act, w_ref[...].astype(BF16)) + b_ref[...]


def _ada(c, w, b):
    bsz, d = c.shape
    n = w.shape[1]
    tn = D_MODEL
    return pl.pallas_call(
        _ada_kernel,
        grid=(n // tn,),
        in_specs=[pl.BlockSpec((bsz, d), lambda j: (0, 0)),
                  pl.BlockSpec((d, tn), lambda j: (0, j)),
                  pl.BlockSpec((1, tn), lambda j: (0, j))],
        out_specs=pl.BlockSpec((bsz, tn), lambda j: (0, j)),
        out_shape=jax.ShapeDtypeStruct((bsz, n), F32),
        compiler_params=_params(1),
        name="ada_proj",
    )(c, w, b.reshape(1, n))


def _swiglu(h, w_in_ref, w_out_ref):
    acc = jnp.zeros((h.shape[0], w_out_ref.shape[1]), F32)
    for j in range(D_FF // FF_CHUNK):
        c0 = j * FF_CHUNK
        gt = _dot(h, w_in_ref[:, c0:c0 + FF_CHUNK])
        up = _dot(h, w_in_ref[:, D_FF + c0:D_FF + c0 + FF_CHUNK])
        act = (gt * jax.nn.sigmoid(gt) * up).astype(BF16)
        acc = acc + _dot(act, w_out_ref[c0:c0 + FF_CHUNK, :])
    return acc


def _swiglu_block(x, shift, scale, gate, w_in_ref, w_out_ref, g, b):
    h = (x * (1.0 + scale) + shift).astype(BF16)
    return _layer_norm(ALPHA * x + 0.5 * gate * _swiglu(h, w_in_ref, w_out_ref), g, b)


def _ffn_mixin_kernel(x_ref, shift0_ref, scale0_ref, gate0_ref, g0_ref, b0_ref, g1_ref, b1_ref, w_in_ref, w_out_ref,
                      shift1_ref, scale1_ref, w_ref,
                      x1_ref, u_ref, q_ref, kc_ref, vc_ref, kvs_ref, kvw_ref, gn_ref, gbr_ref, *, pre_ln):
    x = x_ref[0]
    if pre_ln:
        x = _layer_norm(x, g0_ref[...], b0_ref[...])
    x1 = _swiglu_block(x, shift0_ref[0], scale0_ref[0], gate0_ref[0], w_in_ref, w_out_ref, g1_ref[...], b1_ref[...])
    x1_ref[0] = x1

    h = (x1 * (1.0 + scale1_ref[0]) + shift1_ref[0]).astype(BF16)

    def proj(c0, c1):
        return _dot(h, w_ref[:, c0:c1])

    tm = x_ref.shape[1]
    pos = pl.program_id(1) * tm + lax.broadcasted_iota(jnp.int32, (tm, LANES), 0)
    lane = lax.broadcasted_iota(jnp.int32, (tm, LANES), 1)
    feat = jnp.where(lane == pos // SLC_BLOCK, 1.0,
                     jnp.where(lane == FEAT_OFFSET_LANE, (pos % SLC_BLOCK).astype(F32), 0.0)).astype(BF16)

    def put_u(z):
        u_ref[0] = z

    def put_q(z):
        q_ref[0] = (z * HEAD_DIM ** -0.5).astype(BF16)

    def put_cmp(z):
        kc_ref[0] = z[:, 0:KV_WIDTH]
        vc_ref[0] = z[:, KV_WIDTH:2 * KV_WIDTH]

    def put_keys(ref):
        def put(z):
            zb = z.astype(BF16)
            for g in range(N_KV):
                ref[0, g, :, 0:KV_PACK] = zb[:, g * KV_PACK:(g + 1) * KV_PACK]
                ref[0, g, :, KV_PACK:KEY_WIDTH] = feat
        return put

    def put_head_gates(z):
        sg = jax.nn.sigmoid(z)
        for g in range(N_KV):
            gn_ref[0, g] = sg[:, g * LANES:(g + 1) * LANES]

    def put_branch_gates(c0):
        def put(z):
            gbr_ref[0, :, c0 - _C_GBR:c0 - _C_GBR + z.shape[1]] = jax.nn.sigmoid(z).astype(BF16)
        return put

    gate_cols = [(c0, c0 + GATE_CHUNK) for c0 in range(_C_GBR, _C_END, GATE_CHUNK)]
    light = [((_C_U, _C_Q), put_u), ((_C_Q, _C_KC), put_q), ((_C_KC, _C_KVS), put_cmp),
             ((_C_KVS, _C_KVW), put_keys(kvs_ref)), ((_C_KVW, _C_GN), put_keys(kvw_ref)),
             ((_C_GN, _C_GBR), put_head_gates)]
    stages = []
    for n in range(max(len(gate_cols), len(light))):
        if n < len(gate_cols):
            stages.append((gate_cols[n], put_branch_gates(gate_cols[n][0])))
        if n < len(light):
            stages.append(light[n])
    pending = proj(*stages[0][0])
    for n, (_, sink) in enumerate(stages):
        z = pending
        if n + 1 < len(stages):
            pending = proj(*stages[n + 1][0])
        sink(z)


def _ffn_mixin(x, shift0, scale0, gate0, g0, b0, g1, b1, w_in, w_out, shift1, scale1, w_mix, *, pre_ln):
    bsz, s, d = x.shape
    tm = min(ROW_TILE, s)

    def row(n, dt):
        return pl.BlockSpec((1, tm, n), lambda b, i: (b, i, 0)), jax.ShapeDtypeStruct((bsz, s, n), dt)

    def grouped(n, dt):
        return (pl.BlockSpec((1, N_KV, tm, n), lambda b, i: (b, 0, i, 0)),
                jax.ShapeDtypeStruct((bsz, N_KV, s, n), dt))

    outs = [row(d, F32), row(POOL_WIDTH, F32), row(Q_WIDTH, BF16), row(KV_WIDTH, F32), row(KV_WIDTH, F32),
            grouped(KEY_WIDTH, BF16), grouped(KEY_WIDTH, BF16), grouped(LANES, F32), row(2 * D_MODEL, BF16)]
    mod = pl.BlockSpec((1, 1, d), lambda b, i: (b, 0, 0))
    vec = _const_spec((1, d))
    return pl.pallas_call(
        functools.partial(_ffn_mixin_kernel, pre_ln=pre_ln),
        grid=(bsz, s // tm),
        in_specs=[row(d, F32)[0], mod, mod, mod, vec, vec, vec, vec,
                  _const_spec(w_in.shape), _const_spec(w_out.shape), mod, mod, _const_spec(w_mix.shape)],
        out_specs=[o[0] for o in outs],
        out_shape=[o[1] for o in outs],
        compiler_params=_params(2),
        name="ffn_mixer_in",
    )(x, shift0, scale0, gate0, g0.reshape(1, d), b0.reshape(1, d), g1.reshape(1, d), b1.reshape(1, d),
      w_in, w_out, shift1, scale1, w_mix)


def _compress_kernel(k_ref, v_ref, posk_ref, posv_ref, w1k_ref, w1v_ref, w2k_ref, w2v_ref, o_ref, nxt_ref):
    r = k_ref.shape[1] // CMP_STRIDE

    def hidden(x_ref, pos_ref, w1_ref):
        x = jnp.concatenate([x_ref[0, pl.ds(j, r, stride=CMP_STRIDE), :] for j in range(CMP_STRIDE)], axis=1)
        first = _dot((x + pos_ref[0:1, :]).astype(BF16), w1_ref[0])
        nxt_ref[0:r, :] = _dot((x + pos_ref[1:2, :]).astype(BF16), w1_ref[1])
        nxt_ref[r:r + 8, :] = jnp.zeros((8, nxt_ref.shape[1]), F32)
        return jax.nn.gelu(first + nxt_ref[1:r + 1, :]).astype(BF16)

    hk = hidden(k_ref, posk_ref, w1k_ref)
    hv = hidden(v_ref, posv_ref, w1v_ref)
    o_ref[0] = (_dot(hk, w2k_ref[...]) + _dot(hv, w2v_ref[...])).astype(BF16)


def _compress(kc, vc, posk, posv, w1k, w1v, w2k, w2v):
    bsz, s, n = kc.shape
    r = s // CMP_STRIDE
    hid = w1k.shape[2]
    row = pl.BlockSpec((1, s, n), lambda b: (b, 0, 0))
    return pl.pallas_call(
        _compress_kernel,
        grid=(bsz,),
        in_specs=[row, row, _const_spec(posk.shape), _const_spec(posv.shape),
                  _const_spec(w1k.shape), _const_spec(w1v.shape),
                  _const_spec(w2k.shape), _const_spec(w2v.shape)],
        out_specs=pl.BlockSpec((1, r, N_KV * KV_PACK), lambda b: (b, 0, 0)),
        out_shape=jax.ShapeDtypeStruct((bsz, r, N_KV * KV_PACK), BF16),
        scratch_shapes=[pltpu.VMEM((r + 8, hid), F32)],
        compiler_params=_params(1),
        name="compress_mlp",
    )(kc, vc, posk, posv, w1k, w1v, w2k, w2v)


def _alibi_slope(head):
    return 2.0 ** (-ALIBI_MAX * (head + 1) / N_HEADS)


def _padded_heads(q_tile):
    lane = lax.broadcasted_iota(jnp.int32, (Q_BLOCK, LANES), 1)
    qf = q_tile.astype(F32)
    heads = []
    for hd in range(N_HEADS):
        slab = qf[:, (hd // 2) * LANES:(hd // 2 + 1) * LANES]
        if hd % 2:
            slab = pltpu.roll(slab, HEAD_DIM, 1)
        heads.append(jnp.where(lane < HEAD_DIM, slab, 0.0))
    return heads


def _pack_heads(o_rows):
    lane = lax.broadcasted_iota(jnp.int32, (Q_BLOCK, LANES), 1)
    pairs = []
    for j in range(HEADS_PER_KV // 2):
        even = o_rows[(2 * j) * Q_BLOCK:(2 * j + 1) * Q_BLOCK]
        odd = o_rows[(2 * j + 1) * Q_BLOCK:(2 * j + 2) * Q_BLOCK]
        pairs.append(jnp.where(lane < HEAD_DIM, pltpu.roll(even, HEAD_DIM, 1), odd))
    return jnp.concatenate(pairs, axis=1)


def _feature_rows(qb, hd):
    lane_row = lax.broadcasted_iota(jnp.int32, (1, LANES), 1)
    slope = _alibi_slope(hd)
    tail = jnp.where(lane_row == FEAT_OFFSET_LANE, slope, 0.0)
    block_bias = slope * SLC_BLOCK * (lane_row - qb).astype(F32)
    in_window = (lane_row >= qb - WINDOW // SLC_BLOCK) & (lane_row <= qb)
    slc_row = jnp.where(lane_row < SLC_BLOCK, block_bias, tail)
    win_row = jnp.where(lane_row < SLC_BLOCK, jnp.where(in_window, block_bias, -BIG), tail)
    return slc_row, win_row


def _window_kernel(q_ref, kpw_ref, triw_ref, o_ref):
    gw = HEADS_PER_KV * HEAD_DIM
    chains = [(k, g) for k in range(WIN_NQ) for g in range(N_KV)]

    def scores(k, g):
        qb = pl.program_id(1) * WIN_NQ + k
        heads = _padded_heads(q_ref[0, k * Q_BLOCK:(k + 1) * Q_BLOCK, :])
        w0 = pl.multiple_of(jnp.maximum((qb + 1) * Q_BLOCK - WIN_KEYS, 0), Q_BLOCK)
        tri = jnp.concatenate([triw_ref[jnp.minimum(qb, WIN_KEYS // SLC_BLOCK - 1)]] * HEADS_PER_KV, axis=0)
        qw = [jnp.concatenate([heads[hd], jnp.broadcast_to(_feature_rows(qb, hd)[1], (Q_BLOCK, LANES))], axis=1)
              for hd in range(g * HEADS_PER_KV, (g + 1) * HEADS_PER_KV)]
        kp = kpw_ref[0, g, pl.ds(w0, WIN_KEYS), :]
        return _dot_nt(jnp.concatenate(qw, axis=0).astype(BF16), kp) + tri, kp

    def finish(k, g, o_raw, inv_l):
        o_ref[0, k * Q_BLOCK:(k + 1) * Q_BLOCK, g * gw:(g + 1) * gw] = _pack_heads(o_raw * inv_l).astype(BF16)

    pending = [scores(*chains[n]) for n in range(WIN_AHEAD)]
    unfinished = None
    for n, (k, g) in enumerate(chains):
        s, kp = pending.pop(0)
        if n + WIN_AHEAD < len(chains):
            pending.append(scores(*chains[n + WIN_AHEAD]))
        e = jnp.exp(s - jnp.max(s, axis=-1, keepdims=True))
        o_raw = _dot(e.astype(BF16), kp[:, 0:KV_PACK])
        inv_l = 1.0 / jnp.sum(e, axis=-1, keepdims=True)
        if unfinished is not None:
            finish(*unfinished)
        unfinished = (k, g, o_raw, inv_l)
    finish(*unfinished)


def _window(q, kpw, tri_win):
    bsz, s, _ = q.shape
    qt = WIN_NQ * Q_BLOCK
    return pl.pallas_call(
        _window_kernel,
        grid=(bsz, s // qt),
        in_specs=[pl.BlockSpec((1, qt, Q_WIDTH), lambda b, i: (b, i, 0)),
                  pl.BlockSpec((1, N_KV, s, KEY_WIDTH), lambda b, i: (b, 0, 0, 0)),
                  _const_spec(tri_win.shape)],
        out_specs=pl.BlockSpec((1, qt, Q_WIDTH), lambda b, i: (b, i, 0)),
        out_shape=jax.ShapeDtypeStruct((bsz, s, Q_WIDTH), BF16),
        compiler_params=_params(2),
        name="window_attention",
    )(q, kpw, tri_win)


def _select_kernel(q_ref, kvc_ref, cbias_ref, ovl_ref, eye_ref, ocmp_ref, selneg_ref, first_ref, score_ref, *, n_sel):
    step = pl.program_id(1)
    rows = SEL_NQ * HEADS_PER_KV * Q_BLOCK
    n_slc = ovl_ref.shape[0]
    r = kvc_ref.shape[1]
    lanes_q = SEL_NQ * N_KV * Q_BLOCK
    heads = [_padded_heads(q_ref[0, k * Q_BLOCK:(k + 1) * Q_BLOCK, :]) for k in range(SEL_NQ)]

    row = lax.broadcasted_iota(jnp.int32, (rows, 1), 0)
    t = (step * SEL_NQ + row // (HEADS_PER_KV * Q_BLOCK)) * Q_BLOCK + (row & (Q_BLOCK - 1))
    last_cmp = (t - (CMP_BLOCK - 1)) >> 4
    visible = lax.broadcasted_iota(jnp.int32, (rows, r), 1) <= last_cmp
    p_sums = [[None] * N_KV for _ in range(SEL_NQ)]
    gw = HEADS_PER_KV * HEAD_DIM
    def cmp_scores(g):
        q_pad = jnp.concatenate([heads[k][g * HEADS_PER_KV + h] for k in range(SEL_NQ)
                                 for h in range(HEADS_PER_KV)], axis=0).astype(BF16)
        return _dot_nt(q_pad, kvc_ref[0, :, g * KV_PACK:(g + 1) * KV_PACK])

    raw = [cmp_scores(g) for g in range(N_KV)]
    for g in range(N_KV):
        kvc = kvc_ref[0, :, g * KV_PACK:(g + 1) * KV_PACK]
        p_cmp = _masked_softmax(raw[g] + cbias_ref[g], visible, last_cmp >= 0)
        o_cmp = _dot(p_cmp.astype(BF16), kvc)
        for k in range(SEL_NQ):
            base = k * HEADS_PER_KV * Q_BLOCK
            ocmp_ref[0, k * Q_BLOCK:(k + 1) * Q_BLOCK, g * gw:(g + 1) * gw] = _pack_heads(
                o_cmp[base:base + HEADS_PER_KV * Q_BLOCK]).astype(BF16)
            p_sum = p_cmp[base:base + Q_BLOCK]
            for h in range(1, HEADS_PER_KV):
                p_sum = p_sum + p_cmp[base + h * Q_BLOCK:base + (h + 1) * Q_BLOCK]
            p_sums[k][g] = p_sum

    p_all = jnp.concatenate([p_sums[k][g] for k in range(SEL_NQ) for g in range(N_KV)], axis=0)
    p_hi = p_all.astype(BF16)
    p_lo = (p_all - p_hi.astype(F32)).astype(BF16)
    ovl = ovl_ref[...]
    imp_t = _dot_nt(ovl, p_hi) + _dot_nt(ovl, p_lo)

    blk = lax.broadcasted_iota(jnp.int32, (n_slc, lanes_q), 0)
    qb = step * SEL_NQ + lax.broadcasted_iota(jnp.int32, (n_slc, lanes_q), 1) // (N_KV * Q_BLOCK)
    forced = (blk == 0) | (blk == qb) | (blk == qb - 1)
    score = jnp.where(blk > qb, NEG, jnp.where(forced, FORCE, jnp.where(imp_t > 0.0, imp_t, 0.0)))
    key = lax.bitcast_convert_type(score, jnp.int32)
    key_prev = key - 1
    score_ref[...] = key

    def rank_step(i, rank):
        wins = []
        for u in range(RANK_ROWS):
            row_i = i * RANK_ROWS + u
            ri = score_ref[pl.ds(row_i, 1), :]
            wins.append(jnp.where(ri > jnp.where(blk > row_i, key_prev, key), 1.0, 0.0))
        return rank + functools.reduce(jnp.add, wins)

    rank = lax.fori_loop(0, (step * SEL_NQ + SEL_NQ - 1) // RANK_ROWS + 1, rank_step,
                         jnp.zeros((n_slc, lanes_q), F32))
    chosen = (rank < n_sel) & (blk <= qb)

    oldest = jnp.where(chosen & (blk >= 1), blk, n_slc).astype(F32)
    sel_t = jnp.where(chosen, 1.0, 0.0).astype(BF16)
    sel_t = jnp.concatenate([sel_t, jnp.zeros((LANES - n_slc, lanes_q), BF16)], axis=0)
    sel = _dot_nt(eye_ref[...], sel_t)
    lane = lax.broadcasted_iota(jnp.int32, (Q_BLOCK, LANES), 1)
    for k in range(SEL_NQ):
        lanes_k = slice(k * N_KV * Q_BLOCK, (k + 1) * N_KV * Q_BLOCK)
        first = jnp.min(jnp.min(oldest[:, lanes_k], axis=1, keepdims=True), axis=0, keepdims=True)
        first_ref[0, k] = jnp.broadcast_to(first, first_ref.shape[2:]).astype(jnp.int32)
        for g in range(N_KV):
            base = (k * N_KV + g) * Q_BLOCK
            selneg_ref[0, g, k * Q_BLOCK:(k + 1) * Q_BLOCK, :] = jnp.where(
                lane < SLC_BLOCK, (sel[base:base + Q_BLOCK] - 1.0) * BIG, 0.0).astype(BF16)


def _select(q, kvc, cbias, overlap_t, n_sel):
    bsz, s, _ = q.shape
    r = kvc.shape[1]
    n_slc = overlap_t.shape[0]
    qt = SEL_NQ * Q_BLOCK
    eye = jnp.asarray(np.eye(N_KV * qt), BF16)
    return pl.pallas_call(
        functools.partial(_select_kernel, n_sel=n_sel),
        grid=(bsz, s // qt),
        in_specs=[pl.BlockSpec((1, qt, Q_WIDTH), lambda b, i: (b, i, 0)),
                  pl.BlockSpec((1, r, N_KV * KV_PACK), lambda b, i: (b, 0, 0)),
                  _const_spec(cbias.shape), _const_spec(overlap_t.shape), _const_spec(eye.shape)],
        out_specs=[pl.BlockSpec((1, qt, Q_WIDTH), lambda b, i: (b, i, 0)),
                   pl.BlockSpec((1, N_KV, qt, LANES), lambda b, i: (b, 0, i, 0)),
                   pl.BlockSpec((1, SEL_NQ, 8, LANES), lambda b, i: (b, i, 0, 0))],
        out_shape=[jax.ShapeDtypeStruct((bsz, s, Q_WIDTH), BF16),
                   jax.ShapeDtypeStruct((bsz, N_KV, s, LANES), BF16),
                   jax.ShapeDtypeStruct((bsz, s // Q_BLOCK, 8, LANES), jnp.int32)],
        scratch_shapes=[pltpu.VMEM((n_slc, SEL_NQ * N_KV * Q_BLOCK), jnp.int32)],
        compiler_params=_params(2),
        name="compressed_attention_select",
    )(q, kvc, cbias, overlap_t, eye)


def _selected_kernel(first_ref, q_ref, selneg_ref, gate_ref, ocmp_ref, owin_ref, kps_ref, tris_ref,
                     o_ref, qs_ref, s_ref, mt_ref, m_ref, lt_ref, acc_ref):
    qbs = [pl.program_id(1) * SLC_NQ + k for k in range(SLC_NQ)]
    rows = SLC_NQ * HEADS_PER_KV * Q_BLOCK
    heads = [_padded_heads(q_ref[0, k * Q_BLOCK:(k + 1) * Q_BLOCK, :]) for k in range(SLC_NQ)]
    for g in range(N_KV):
        qs = []
        for k in range(SLC_NQ):
            masked_out = selneg_ref[0, g, k * Q_BLOCK:(k + 1) * Q_BLOCK, :].astype(F32)
            qs += [jnp.concatenate([heads[k][hd], masked_out + _feature_rows(qbs[k], hd)[0]], axis=1)
                   for hd in range(g * HEADS_PER_KV, (g + 1) * HEADS_PER_KV)]
        qs_ref[g] = jnp.concatenate(qs, axis=0).astype(BF16)

    unit_blocks = SLC_UNIT // SLC_BLOCK
    diag_unit = qbs[0] // unit_blocks
    first = functools.reduce(jnp.minimum, [first_ref[pl.program_id(0), qb] for qb in qbs])
    first_unit = jnp.minimum(first, qbs[0]) // unit_blocks
    lead = jnp.minimum(first_unit, 1)
    n_units = diag_unit - first_unit + 1 + lead

    def unit_of(i):
        return jnp.where((i < lead) | (i >= n_units), 0, first_unit + i - lead)

    def unit_start(i):
        return pl.multiple_of(unit_of(i) * SLC_UNIT, SLC_UNIT)

    def slot_start(i):
        return pl.multiple_of(i * SLC_UNIT, SLC_UNIT)

    def trip_chains(first_slot, width):
        return [(g, first_slot + u) for u in range(width) for g in range(N_KV)]

    def score_matmul(g, i):
        return _dot_nt(qs_ref[g], kps_ref[0, g, pl.ds(unit_start(i), SLC_UNIT), :])

    def score_finish(g, i, raw):
        tri = []
        for qb in qbs:
            variant = jnp.where(i >= n_units, unit_blocks + 1,
                                jnp.where(unit_of(i) == diag_unit, qb % unit_blocks, unit_blocks))
            tri += [tris_ref[variant]] * HEADS_PER_KV
        s = (raw + jnp.concatenate(tri, axis=0)) * LOG2E
        s_ref[g, :, pl.ds(slot_start(i), SLC_UNIT)] = s
        mt_ref[g] = jnp.maximum(jnp.maximum(mt_ref[g], s[:, 0:LANES]), s[:, LANES:SLC_UNIT])

    def score_trip(first_slot, width):
        chains = trip_chains(first_slot, width)
        pending = score_matmul(*chains[0])
        for n, (g, i) in enumerate(chains):
            raw = pending
            if n + 1 < len(chains):
                pending = score_matmul(*chains[n + 1])
            score_finish(g, i, raw)

    def exponentials(g, i):
        m = m_ref[g]
        es = [jnp.exp2(s_ref[g, :, pl.ds(slot_start(i) + j * LANES, LANES)] - m) for j in range(SLC_UNIT // LANES)]
        lt_ref[g] = lt_ref[g] + functools.reduce(jnp.add, es)
        return jnp.concatenate(es, axis=1).astype(BF16)

    def value_trip(first_slot, width):
        chains = trip_chains(first_slot, width)
        pending = exponentials(*chains[0])
        for n, (g, i) in enumerate(chains):
            e = pending
            if n + 1 < len(chains):
                pending = exponentials(*chains[n + 1])
            acc_ref[g] = acc_ref[g] + _dot(e, kps_ref[0, g, pl.ds(unit_start(i), SLC_UNIT), 0:KV_PACK])

    def sweep(trip):
        done = 0
        for tier, width in enumerate(TRIP_WIDTHS):
            left = n_units - done
            if tier + 1 < len(TRIP_WIDTHS):
                count = left // width + jnp.where(left % width > TRIP_WIDTHS[tier + 1], 1, 0)
            else:
                count = (left + width - 1) // width
            count = jnp.maximum(count, 0)

            def body(p, carry, base=done, width=width):
                trip(base + p * width, width)
                return carry

            lax.fori_loop(0, count, body, 0)
            done = done + count * width

    mt_ref[...] = jnp.full(mt_ref.shape, M_INIT, F32)
    sweep(score_trip)
    for g in range(N_KV):
        m_ref[g] = jnp.broadcast_to(jnp.max(mt_ref[g], axis=-1, keepdims=True), (rows, LANES))
    lt_ref[...] = jnp.zeros(lt_ref.shape, F32)
    acc_ref[...] = jnp.zeros(acc_ref.shape, F32)
    sweep(value_trip)

    lane = lax.broadcasted_iota(jnp.int32, (Q_BLOCK, LANES), 1)
    for g in range(N_KV):
        l = jnp.sum(lt_ref[g], axis=-1, keepdims=True)
        o_rows = acc_ref[g] * jnp.where(l > 0.0, 1.0 / l, 0.0)
        for k in range(SLC_NQ):
            qr = slice(k * Q_BLOCK, (k + 1) * Q_BLOCK)
            o_slc = _pack_heads(o_rows[k * HEADS_PER_KV * Q_BLOCK:(k + 1) * HEADS_PER_KV * Q_BLOCK])
            gt = gate_ref[0, g, qr, :]
            for j in range(HEADS_PER_KV // 2):
                cols = slice((g * HEADS_PER_KV // 2 + j) * LANES, (g * HEADS_PER_KV // 2 + j + 1) * LANES)
                branches = (ocmp_ref[0, qr, cols].astype(F32), o_slc[:, j * LANES:(j + 1) * LANES],
                            owin_ref[0, qr, cols].astype(F32))
                total = None
                for c, branch in enumerate(branches):
                    even, odd = 3 * (2 * j) + c, 3 * (2 * j + 1) + c
                    gate = jnp.take_along_axis(gt, jnp.where(lane < HEAD_DIM, even, odd), axis=1)
                    total = gate * branch if total is None else total + gate * branch
                o_ref[0, qr, cols] = total.astype(BF16)


def _selected(first, q, selneg, gates, o_cmp, o_win, kps, tri_slc):
    bsz, s, _ = q.shape
    qt = SLC_NQ * Q_BLOCK
    rows = HEADS_PER_KV * qt
    assert (SLC_UNIT // SLC_BLOCK) % SLC_NQ == 0 and s % qt == 0

    def row(n):
        return pl.BlockSpec((1, qt, n), lambda b, i, first_ref: (b, i, 0))

    def grouped(n):
        return pl.BlockSpec((1, N_KV, qt, n), lambda b, i, first_ref: (b, 0, i, 0))

    grid_spec = pltpu.PrefetchScalarGridSpec(
        num_scalar_prefetch=1,
        grid=(bsz, s // qt),
        in_specs=[row(Q_WIDTH), grouped(LANES), grouped(LANES), row(Q_WIDTH), row(Q_WIDTH),
                  pl.BlockSpec((1, N_KV, s, KEY_WIDTH), lambda b, i, first_ref: (b, 0, 0, 0)),
                  _const_spec(tri_slc.shape)],
        out_specs=row(Q_WIDTH),
        scratch_shapes=[pltpu.VMEM((N_KV, rows, KEY_WIDTH), BF16),
                        pltpu.VMEM((N_KV, rows, s + SLC_UNIT), F32)]
        + [pltpu.VMEM((N_KV, rows, LANES), F32)] * 4)
    return pl.pallas_call(
        _selected_kernel,
        grid_spec=grid_spec,
        out_shape=jax.ShapeDtypeStruct((bsz, s, Q_WIDTH), BF16),
        compiler_params=_params(2),
        name="selected_attention",
    )(first, q, selneg, gates, o_cmp, o_win, kps, tri_slc)


def _attention(q, gates, kvc, kps, kpw):
    bsz, s, _ = q.shape
    r = kvc.shape[1]
    n_cmp = r - 1
    n_slc = s // SLC_BLOCK
    n_sel = min(N_SELECT, n_slc)
    rows = HEADS_PER_KV * Q_BLOCK
    assert n_slc <= SLC_BLOCK and n_slc % RANK_ROWS == 0 and s % SLC_UNIT == 0 and s >= WIN_KEYS
    assert s % (WIN_NQ * Q_BLOCK) == 0 and s % (SEL_NQ * Q_BLOCK) == 0

    slopes = np.array([_alibi_slope(hd) for hd in range(N_HEADS)])
    slope_rows = np.repeat(slopes.reshape(N_KV, HEADS_PER_KV), Q_BLOCK, axis=1).reshape(N_KV, rows, 1)
    cbias = np.tile(slope_rows * (CMP_STRIDE * np.arange(r))[None, None, :], (1, SEL_NQ, 1))
    start = np.arange(r)[None, :] * CMP_STRIDE
    blk = np.arange(n_slc)[:, None] * SLC_BLOCK
    overlap_t = ((start < blk + SLC_BLOCK) & (start + CMP_BLOCK > blk) & (np.arange(r)[None, :] < n_cmp))

    ql = np.arange(Q_BLOCK)[:, None]
    kl = np.arange(SLC_BLOCK)[None, :]
    lower = np.where(kl > ql, -BIG, 0.0)
    upper = np.where(kl <= ql, -BIG, 0.0)
    unit_blocks = SLC_UNIT // SLC_BLOCK
    tri_slc = np.zeros((unit_blocks + 2, Q_BLOCK, SLC_UNIT))
    for j in range(unit_blocks):
        tri_slc[j, :, j * SLC_BLOCK:(j + 1) * SLC_BLOCK] = lower
    tri_slc[unit_blocks + 1] = -BIG
    win_blocks = WINDOW // SLC_BLOCK
    lead = WIN_KEYS // SLC_BLOCK - 1
    tri_win = np.zeros((lead + 1, Q_BLOCK, WIN_KEYS))
    for v in range(lead + 1):
        diag = v
        tri_win[v, :, diag * SLC_BLOCK:(diag + 1) * SLC_BLOCK] = lower
        if diag >= win_blocks:
            old = diag - win_blocks
            tri_win[v, :, old * SLC_BLOCK:(old + 1) * SLC_BLOCK] = upper

    o_win = _window(q, kpw, jnp.asarray(tri_win, F32))
    o_cmp, selneg, first = _select(q, kvc, jnp.asarray(cbias, F32), jnp.asarray(overlap_t, BF16), n_sel)
    return _selected(first[:, :, 0, 0], q, selneg, gates, o_cmp, o_win, kps, jnp.asarray(tri_slc, F32))


def _merge_ffn_kernel(x_ref, gate1_ref, uprev_ref, u_ref, o_ref, gbr_ref, pw_ref, ps_ref, wa_ref, wb_ref, wo_ref,
                      g2_ref, b2_ref, shift2_ref, scale2_ref, gate2_ref, g3_ref, b3_ref, w_in_ref, w_out_ref,
                      out_ref, ubuf_ref, sums_ref):
    i = pl.program_id(1)
    tm = u_ref.shape[1]
    ubuf_ref[0:POOL_HALO, :] = jnp.where(i == 0, 0.0, uprev_ref[0])
    ubuf_ref[POOL_HALO:POOL_HALO + tm, :] = u_ref[0]
    t = i * tm + lax.broadcasted_iota(jnp.int32, (tm, 1), 0)

    y_b = _dot(o_ref[0], wb_ref[...])

    end = POOL_HALO + tm
    totals = []
    for level in range(len(POOL_WINDOWS)):
        width = 1 << level
        start = 8 * (level + 1)
        c0 = level * POOL_GROUP
        src = (lambda r0, r1, c=c0: ubuf_ref[r0:r1, c:POOL_WIDTH]) if level == 0 else (
            lambda r0, r1, c=c0, n=level - 1: sums_ref[n, r0:r1, c:POOL_WIDTH])
        doubled = src(start, end) + src(start - width, end - width)
        if level + 1 < len(POOL_WINDOWS):
            sums_ref[level, start:end, c0:POOL_WIDTH] = doubled
        totals.append(doubled[POOL_HALO - start:, 0:POOL_GROUP])

    mixed = []
    for gi, w in enumerate(POOL_WINDOWS):
        assert w == 2 << gi
        cs = slice(gi * POOL_GROUP, (gi + 1) * POOL_GROUP)
        cur = ubuf_ref[POOL_HALO:POOL_HALO + tm, cs]
        total = totals[gi]
        inv_cnt = 1.0 / jnp.minimum(t + 1, w).astype(F32)
        delta = (total * inv_cnt - cur).astype(BF16)
        mixed.append((_dot(delta, pw_ref[gi]) * ps_ref[:, cs]).astype(BF16))
    y_a = _dot(jnp.concatenate(mixed, axis=1), wa_ref[...])
    d = y_a.shape[1]
    y = (gbr_ref[0, :, 0:d].astype(F32) * y_a + gbr_ref[0, :, d:2 * d].astype(F32) * y_b).astype(BF16)
    y = _dot(y, wo_ref[...])
    x2 = _layer_norm(ALPHA * x_ref[0] + gate1_ref[0] * y, g2_ref[...], b2_ref[...])
    out_ref[0] = _swiglu_block(x2, shift2_ref[0], scale2_ref[0], gate2_ref[0], w_in_ref, w_out_ref,
                               g3_ref[...], b3_ref[...])


def _merge_ffn(x, gate1, u, o, gbr, pool_w, pool_scale, w_a, w_b, w_o, g2, b2,
               shift2, scale2, gate2, g3, b3, w_in, w_out):
    bsz, s, d = x.shape
    tm = min(ROW_TILE, s)
    halo_blocks = tm // POOL_HALO

    def row(n):
        return pl.BlockSpec((1, tm, n), lambda bi, i: (bi, i, 0))

    mod = pl.BlockSpec((1, 1, d), lambda bi, i: (bi, 0, 0))
    vec = _const_spec((1, d))
    return pl.pallas_call(
        _merge_ffn_kernel,
        grid=(bsz, s // tm),
        in_specs=[row(d), mod,
                  pl.BlockSpec((1, POOL_HALO, POOL_WIDTH),
                               lambda bi, i: (bi, jnp.maximum(i * halo_blocks - 1, 0), 0)),
                  row(POOL_WIDTH), row(Q_WIDTH), row(2 * d),
                  _const_spec(pool_w.shape), _const_spec((1, POOL_WIDTH)),
                  _const_spec(w_a.shape), _const_spec(w_b.shape), _const_spec(w_o.shape),
                  vec, vec, mod, mod, mod, vec, vec, _const_spec(w_in.shape), _const_spec(w_out.shape)],
        out_specs=row(d),
        out_shape=jax.ShapeDtypeStruct((bsz, s, d), F32),
        scratch_shapes=[pltpu.VMEM((POOL_HALO + tm, POOL_WIDTH), F32),
                        pltpu.VMEM((len(POOL_WINDOWS) - 1, POOL_HALO + tm, POOL_WIDTH), F32)],
        compiler_params=_params(2),
        name="pool_merge_ffn",
    )(x, gate1, u, u, o, gbr, pool_w, pool_scale.reshape(1, POOL_WIDTH), w_a, w_b, w_o,
      g2.reshape(1, d), b2.reshape(1, d), shift2, scale2, gate2, g3.reshape(1, d), b3.reshape(1, d), w_in, w_out)


def _mixer_in_weights(w):
    sizes = (POOL_WIDTH, Q_WIDTH) + (KV_WIDTH,) * 6 + (3 * N_HEADS, 2 * D_MODEL)
    offs = np.concatenate([[0], np.cumsum(sizes)])
    u, q, k_cmp, v_cmp, k_slc, v_slc, k_win, v_win, g_nsa, g_br = [w[:, offs[i]:offs[i + 1]] for i in range(10)]
    cols = [u, q, k_cmp, v_cmp]
    for k, v in ((k_slc, v_slc), (k_win, v_win)):
        for g in range(N_KV):
            cols += [k[:, g * HEAD_DIM:(g + 1) * HEAD_DIM], v[:, g * HEAD_DIM:(g + 1) * HEAD_DIM]]
    per_g = 3 * HEADS_PER_KV
    for g in range(N_KV):
        cols += [g_nsa[:, g * per_g:(g + 1) * per_g], jnp.zeros((w.shape[0], LANES - per_g), w.dtype)]
    cols.append(g_br)
    return jnp.concatenate(cols, axis=1).astype(BF16)


def _compress_weights(pos, w1, w2, value_slot):
    same_group = np.eye(N_KV, dtype=np.float32)

    def expand_w1(half):
        wh = half.reshape(CMP_STRIDE, 1, HEAD_DIM, 1, CMP_HIDDEN)
        z = wh * same_group.reshape(1, N_KV, 1, N_KV, 1)
        return z.reshape(CMP_STRIDE * KV_WIDTH, N_KV * CMP_HIDDEN)

    half_rows = CMP_STRIDE * HEAD_DIM
    w1_big = jnp.stack([expand_w1(w1[:half_rows]), expand_w1(w1[half_rows:])]).astype(BF16)
    slot = np.eye(2, dtype=np.float32)[value_slot]
    w2_big = (w2.reshape(1, CMP_HIDDEN, 1, 1, HEAD_DIM) * same_group.reshape(N_KV, 1, N_KV, 1, 1)
              * slot.reshape(1, 1, 1, 2, 1))
    w2_big = w2_big.reshape(N_KV * CMP_HIDDEN, N_KV * KV_PACK).astype(BF16)
    pos_rows = jnp.broadcast_to(pos.reshape(2, CMP_STRIDE, 1, HEAD_DIM), (2, CMP_STRIDE, N_KV, HEAD_DIM))
    return pos_rows.reshape(2, CMP_STRIDE * KV_WIDTH), w1_big, w2_big


def kernel(x, c, ln_in_g, ln_in_b, w_ada, b_ada, ffn1_w_in, ffn1_w_out, ln1_g, ln1_b, w_mix_in, pool_w, pool_scale,
           cmp_pos_k, cmp_k_w1, cmp_k_w2, cmp_pos_v, cmp_v_w1, cmp_v_w2, w_branch_a, w_branch_b, w_mix_out,
           ln2_g, ln2_b, ffn2_w_in, ffn2_w_out, ln3_g, ln3_b):
    bsz, s, d = x.shape
    for l in range(DEPTH):
        ada = _ada(c, w_ada[l], b_ada[l]).reshape(bsz, 3, 3, 1, d)
        mod = lambda i, j: ada[:, i, j]

        x, u, q, kc, vc, kvs, kvw, gates, gbr = _ffn_mixin(
            x, mod(0, 0), mod(0, 1), mod(0, 2), ln_in_g, ln_in_b, ln1_g[l], ln1_b[l],
            ffn1_w_in[l].astype(BF16), ffn1_w_out[l].astype(BF16),
            mod(1, 0), mod(1, 1), _mixer_in_weights(w_mix_in[l]), pre_ln=l == 0)
        posk, w1k, w2k = _compress_weights(cmp_pos_k[l], cmp_k_w1[l], cmp_k_w2[l], 0)
        posv, w1v, w2v = _compress_weights(cmp_pos_v[l], cmp_v_w1[l], cmp_v_w2[l], 1)
        kvc = _compress(kc, vc, posk, posv, w1k, w1v, w2k, w2v)
        o = _attention(q, gates, kvc, kvs, kvw)
        x = _merge_ffn(x, mod(1, 2), u, o, gbr, pool_w[l].astype(BF16), pool_scale[l],
                       w_branch_a[l].astype(BF16), w_branch_b[l].astype(BF16), w_mix_out[l].astype(BF16),
                       ln2_g[l], ln2_b[l], mod(2, 0), mod(2, 1), mod(2, 2), ln3_g[l], ln3_b[l],
                       ffn2_w_in[l].astype(BF16), ffn2_w_out[l].astype(BF16))
    return x
```
